```python
import math
import jax, jax.numpy as jnp
from jax import lax
import numpy as np

D_MODEL = 1024
BATCH = 8
SEQ = 2048
DEPTH = 4
DEC_BATCH = 128
DEC_SEQ = 8
PAST_LEN = 2048
PAGE_SIZE = 128

N_EVEN = (DEPTH + 1) // 2
N_ODD = DEPTH // 2
RMS_EPS = 1e-6
NEG_BIG = -1e30

GDN_HEADS = 4
GDN_DK = 128
GDN_DV = 128
GDN_CONV = 4
GDN_CHUNK = 64
A_QK = GDN_HEADS * GDN_DK
A_WIDTH = GDN_HEADS * GDN_DV
A_CONV_CH = 2 * A_QK + A_WIDTH

FOX_HEADS = 8
FOX_DH = 64
B_WIDTH = FOX_HEADS * FOX_DH
FOX_Q_BLOCK = 128

NSA_HEADS = 16
NSA_KV_HEADS = 4
NSA_GROUP = NSA_HEADS // NSA_KV_HEADS
NSA_DH = 64
C_WIDTH = NSA_HEADS * NSA_DH
C_KV_WIDTH = NSA_KV_HEADS * NSA_DH
CMP_LEN = 32
CMP_STRIDE = CMP_LEN // 2
CMP_HID = 2 * NSA_DH
SEL_BLOCK = 64
SEL_TOP_N = 16
WINDOW = 512
WIN_Q_BLOCK = 128
SEL_SWEEP_ROWS = 256

ROPE_THETA = 500000.0
ROT_DIM = NSA_DH // 4

MEM_TOKENS = 256
MEM_HEADS = 4
MEM_DH = D_MODEL // MEM_HEADS
MEM_WIDTH = MEM_HEADS * MEM_DH

FFN_HIDDEN = 256 * ((8 * D_MODEL + 3 * 256 - 1) // (3 * 256))

W_IN_EVEN = A_CONV_CH + 2 * GDN_HEADS + A_WIDTH + 3 * B_WIDTH + FOX_HEADS
W_IN_ODD = C_WIDTH + 6 * C_KV_WIDTH + 3 * NSA_HEADS

kernel_name = "hybrid_gdn_fox_nsa_decoder_step"


def _rmsnorm(x, g):
    xf = x.astype(jnp.float32)
    y = xf * lax.rsqrt(jnp.mean(xf * xf, axis=-1, keepdims=True) + RMS_EPS)
    return (y * g.astype(jnp.float32)).astype(x.dtype)


def _l2norm(x):
    xf = x.astype(jnp.float32)
    return (xf * lax.rsqrt(jnp.sum(xf * xf, axis=-1, keepdims=True) + 1e-6)).astype(x.dtype)


def _split(x, sizes):
    return jnp.split(x, np.cumsum(sizes)[:-1].tolist(), axis=-1)


def _pad_rows(x, t_pad):
    return jnp.pad(x, [(0, 0), (0, t_pad - x.shape[1])] + [(0, 0)] * (x.ndim - 2))


def _masked_probs(logits, mask):
    lf = jnp.where(mask, logits.astype(jnp.float32), NEG_BIG)
    m = jnp.max(lf, axis=-1, keepdims=True)
    p = jnp.where(mask, jnp.exp(lf - m), 0.0)
    return p / jnp.maximum(jnp.sum(p, axis=-1, keepdims=True), 1e-30)


def _gather_pages(pool, page_table):
    g = pool[page_table]
    return g.reshape(g.shape[0], g.shape[1] * g.shape[2], *g.shape[3:])


def _query_sweep(fn, block, q_pos, *q_args):
    L = q_pos.shape[0]
    if block >= L or L % block:
        return fn(q_pos, *q_args)
    n = L // block

    def split(a):
        return jnp.moveaxis(a.reshape(a.shape[0], n, block, *a.shape[2:]), 1, 0)

    out = lax.map(lambda xs: fn(*xs), (q_pos.reshape(n, block),) + tuple(split(a) for a in q_args))
    return jnp.moveaxis(out, 0, 1).reshape(out.shape[1], L, *out.shape[3:])


def _sel_rows(L, B):
    r = max(1, min(L, SEL_SWEEP_ROWS // B))
    while L % r:
        r -= 1
    return r


def _partial_rope(x, pos):
    half = ROT_DIM // 2
    inv_freq = ROPE_THETA ** (-jnp.arange(half, dtype=jnp.float32) / half)
    ang = pos.astype(jnp.float32)[:, None] * inv_freq[None, :]
    shape = (1, pos.shape[0]) + (1,) * (x.ndim - 3) + (half,)
    cos, sin = jnp.cos(ang).reshape(shape), jnp.sin(ang).reshape(shape)
    xr = x[..., :ROT_DIM].astype(jnp.float32)
    x1, x2 = xr[..., :half], xr[..., half:]
    rot = jnp.concatenate([x1 * cos - x2 * sin, x2 * cos + x1 * sin], axis=-1)
    return jnp.concatenate([rot.astype(x.dtype), x[..., ROT_DIM:]], axis=-1)


def _short_conv(x, buf, w):
    L = x.shape[1]
    ext = jnp.concatenate([buf, x], axis=1)
    y = sum(ext[:, i:i + L] * w[i] for i in range(GDN_CONV))
    return jax.nn.silu(y), ext[:, L:]


def _gated_delta(q, k, v, g, beta, s0):
    f32 = jnp.float32
    B, L, H, DK = q.shape
    DV = v.shape[-1]
    C = min(GDN_CHUNK, L)
    n = -(-L // C)
    pad = n * C - L

    def blocks(x):
        x = jnp.pad(x.astype(f32), [(0, 0), (0, pad)] + [(0, 0)] * (x.ndim - 2))
        x = x.reshape(B, n, C, *x.shape[2:])
        return jnp.transpose(x, (1, 0, 3, 2) + tuple(range(4, x.ndim)))

    qc = blocks(q) * DK ** -0.5
    kc, vc, bc = blocks(k), blocks(v), blocks(beta)
    gc = jnp.cumsum(blocks(g), axis=-1)
    idx = jnp.arange(C)
    causal = idx[:, None] >= idx[None, :]
    strict = idx[:, None] > idx[None, :]
    decay = jnp.exp(jnp.where(causal, gc[..., :, None] - gc[..., None, :], -jnp.inf))
    kb = kc * bc[..., None]
    a_mat = jnp.where(strict, jnp.einsum('nbhid,nbhjd->nbhij', kb, kc) * decay, 0.0)
    t_mat = a_mat + jnp.eye(C, dtype=f32)
    rhs = jnp.concatenate([vc * bc[..., None], kb * jnp.exp(gc)[..., None]], axis=-1)
    sol = lax.linalg.triangular_solve(t_mat, rhs, left_side=True, lower=True, unit_diagonal=True)
    u, w = sol[..., :DV], sol[..., DV:]
    qk = jnp.where(causal, jnp.einsum('nbhid,nbhjd->nbhij', qc, kc) * decay, 0.0)

    def step(s, xs):
        q_i, k_i, u_i, w_i, g_i, qk_i = xs
        v_new = u_i - jnp.einsum('bhck,bhkv->bhcv', w_i, s)
        o_i = jnp.einsum('bhck,bhkv->bhcv', q_i * jnp.exp(g_i)[..., None], s) + jnp.einsum('bhij,bhjv->bhiv', qk_i, v_new)
        g_last = g_i[..., -1:]
        s = s * jnp.exp(g_last)[..., None] + jnp.einsum('bhck,bhcv->bhkv', k_i * jnp.exp(g_last - g_i)[..., None], v_new)
        return s, o_i

    s_fin, o = lax.scan(step, s0.astype(f32), (qc, kc, u, w, gc, qk))
    o = jnp.transpose(o, (1, 0, 3, 2, 4)).reshape(B, n * C, H, DV)[:, :L]
    return o, s_fin


def _fox_attend(q_pos, q, c_q, k, v, c_k, k_pos):
    logits = jnp.einsum('bqhd,bkhd->bhqk', q, k).astype(jnp.float32) * FOX_DH ** -0.5
    logits = logits + (jnp.transpose(c_q, (0, 2, 1))[..., :, None] - jnp.transpose(c_k, (0, 2, 1))[:, :, None, :])
    mask = (k_pos[None, :] <= q_pos[:, None])[None, None]
    p = _masked_probs(logits, mask)
    return jnp.einsum('bhqk,bkhd->bqhd', p.astype(v.dtype), v)


def _band_attend(q_pos, q, k, v, k_pos):
    logits = jnp.einsum('bqhgd,bkhd->bhgqk', q, k) * NSA_DH ** -0.5
    dist = q_pos[:, None] - k_pos[None, :]
    mask = (dist >= 0) & (dist < WINDOW) & (k_pos[None, :] >= 0)
    p = _masked_probs(logits, mask[None, None, None])
    return jnp.einsum('bhgqk,bkhd->bqhgd', p.astype(v.dtype), v)


def _compress(x, pos_emb, w1, w2):
    B, T, KV, dh = x.shape
    sub = x.reshape(B, T // CMP_STRIDE, CMP_STRIDE, KV, dh)
    blocks = jnp.concatenate([sub[:, :-1], sub[:, 1:]], axis=2) + pos_emb[None, None, :, None, :]
    flat = jnp.transpose(blocks, (0, 1, 3, 2, 4)).reshape(B, T // CMP_STRIDE - 1, KV, CMP_LEN * dh)
    return jax.nn.silu(flat @ w1) @ w2


def _even_mixer(h, q_pos, conv_buf, s0, fox_past, w_in, b_forget, conv_w, a_log, dt_bias, gdn_norm, w_out):
    B, L, _ = h.shape
    qkv_a, a_in, b_in, z, q_b, k_b, v_b, f_in = _split(
        h @ w_in, [A_CONV_CH, GDN_HEADS, GDN_HEADS, A_WIDTH, B_WIDTH, B_WIDTH, B_WIDTH, FOX_HEADS])
    qkv_a, new_buf = _short_conv(qkv_a, conv_buf, conv_w)
    q_a, k_a, v_a = _split(qkv_a, [A_QK, A_QK, A_WIDTH])
    q_a = _l2norm(q_a.reshape(B, L, GDN_HEADS, GDN_DK))
    k_a = _l2norm(k_a.reshape(B, L, GDN_HEADS, GDN_DK))
    v_a = v_a.reshape(B, L, GDN_HEADS, GDN_DV)
    g = -jnp.exp(a_log.astype(jnp.float32)) * jax.nn.softplus(a_in.astype(jnp.float32) + dt_bias.astype(jnp.float32))
    beta = jax.nn.sigmoid(b_in.astype(jnp.float32))
    o_a, s_new = _gated_delta(q_a, k_a, v_a, g, beta, s0)
    o_a = _rmsnorm(o_a, gdn_norm).astype(h.dtype) * jax.nn.silu(z.reshape(B, L, GDN_HEADS, GDN_DV))
    q_b = q_b.reshape(B, L, FOX_HEADS, FOX_DH)
    k_b = k_b.reshape(B, L, FOX_HEADS, FOX_DH)
    v_b = v_b.reshape(B, L, FOX_HEADS, FOX_DH)
    logf = jax.nn.log_sigmoid((f_in + b_forget).astype(jnp.float32))
    if fox_past is None:
        k_all, v_all, logf_all = k_b, v_b, logf
    else:
        pk, pv, plogf = fox_past
        k_all = jnp.concatenate([pk, k_b], axis=1)
        v_all = jnp.concatenate([pv, v_b], axis=1)
        logf_all = jnp.concatenate([plogf.astype(jnp.float32), logf], axis=1)
    c_all = jnp.cumsum(logf_all, axis=1)
    k_pos = jnp.arange(k_all.shape[1], dtype=jnp.int32)
    o_b = _query_sweep(lambda qp, qb, cb: _fox_attend(qp, qb, cb, k_all, v_all, c_all, k_pos),
                       FOX_Q_BLOCK, q_pos, q_b, c_all[:, -L:])
    out = jnp.concatenate([o_a.reshape(B, L, A_WIDTH), o_b.reshape(B, L, B_WIDTH)], axis=-1) @ w_out
    return out, (s_new.astype(h.dtype), new_buf, k_b, v_b, logf)


def _odd_mixer(h, q_pos, nsa_past, w_in, gate_bias, cmp_pos, cmp_w1, cmp_w2, w_out):
    B, L, _ = h.shape
    scale = NSA_DH ** -0.5
    q, kc, vc, ks, vs, kw, vw, gl = _split(h @ w_in, [C_WIDTH] + [C_KV_WIDTH] * 6 + [3 * NSA_HEADS])
    q = q.reshape(B, L, NSA_KV_HEADS, NSA_GROUP, NSA_DH)
    kc, vc, ks, vs, kw, vw = [a.reshape(B, L, NSA_KV_HEADS, NSA_DH) for a in (kc, vc, ks, vs, kw, vw)]
    q_rot = _partial_rope(q, q_pos)
    ks = _partial_rope(ks, q_pos)
    kw = _partial_rope(kw, q_pos)
    gates = jax.nn.sigmoid((gl + gate_bias).astype(jnp.float32)).astype(h.dtype)
    gates = gates.reshape(B, L, NSA_KV_HEADS, NSA_GROUP, 3)
    if nsa_past is None:
        kc_all, vc_all, ks_all, vs_all = kc, vc, ks, vs
    else:
        pkc, pvc, pks, pvs, pwk, pwv = nsa_past
        kc_all = jnp.concatenate([pkc, kc], axis=1)
        vc_all = jnp.concatenate([pvc, vc], axis=1)
        ks_all = jnp.concatenate([pks, ks], axis=1)
        vs_all = jnp.concatenate([pvs, vs], axis=1)
    t_pad = -(-kc_all.shape[1] // SEL_BLOCK) * SEL_BLOCK
    kc_all, vc_all, ks_all, vs_all = [_pad_rows(a, t_pad) for a in (kc_all, vc_all, ks_all, vs_all)]
    n_sel = t_pad // SEL_BLOCK

    kcb = _compress(kc_all, cmp_pos[0], cmp_w1[0], cmp_w2[0])
    vcb = _compress(vc_all, cmp_pos[1], cmp_w1[1], cmp_w2[1])
    n_cmp = kcb.shape[1]
    c_start = jnp.arange(n_cmp, dtype=jnp.int32) * CMP_STRIDE
    cmask = (c_start + CMP_LEN - 1)[None, :] <= q_pos[:, None]
    p_cmp = _masked_probs(jnp.einsum('blhgd,bnhd->blhgn', q, kcb) * scale, cmask[None, :, None, None, :])
    o_cmp = jnp.einsum('blhgn,bnhd->blhgd', p_cmp.astype(vcb.dtype), vcb)

    s_start = jnp.arange(n_sel, dtype=jnp.int32) * SEL_BLOCK
    shared_len = (jnp.minimum(c_start[:, None] + CMP_LEN, s_start[None, :] + SEL_BLOCK)
                  - jnp.maximum(c_start[:, None], s_start[None, :]))
    cmp_to_sel = jnp.clip(shared_len, 0).astype(jnp.float32) / CMP_LEN
    imp = jnp.einsum('blhgn,nj->blhj', p_cmp, cmp_to_sel)
    blk = jnp.arange(n_sel, dtype=jnp.int32)[None, :]
    cur = (q_pos // SEL_BLOCK)[:, None]
    causal_blk = (blk <= cur)[None, :, None, :]
    forced = ((blk == 0) | (blk == cur) | (blk == cur - 1))[None, :, None, :]
    imp = jnp.where(causal_blk, jnp.where(forced, -NEG_BIG, imp), NEG_BIG)
    top_val, top_idx = lax.top_k(imp, min(SEL_TOP_N, n_sel))
    top_ok = top_val > 0.5 * NEG_BIG
    ks_blk = jnp.transpose(ks_all.reshape(B, n_sel, SEL_BLOCK, NSA_KV_HEADS, NSA_DH), (0, 3, 1, 2, 4))
    vs_blk = jnp.transpose(vs_all.reshape(B, n_sel, SEL_BLOCK, NSA_KV_HEADS, NSA_DH), (0, 3, 1, 2, 4))
    b_ix = jnp.arange(B)[:, None, None, None]
    h_ix = jnp.arange(NSA_KV_HEADS)[None, :, None, None]

    def sel_block(qp, qb, ib, okb):
        nq, nk = qb.shape[1], ib.shape[-1]
        ib_t = jnp.transpose(ib, (0, 2, 1, 3))
        kb = ks_blk[b_ix, h_ix, ib_t].reshape(B, NSA_KV_HEADS, nq, nk * SEL_BLOCK, NSA_DH)
        vb = vs_blk[b_ix, h_ix, ib_t].reshape(B, NSA_KV_HEADS, nq, nk * SEL_BLOCK, NSA_DH)
        kpos = (ib_t[..., None] * SEL_BLOCK + jnp.arange(SEL_BLOCK)).reshape(B, NSA_KV_HEADS, nq, nk * SEL_BLOCK)
        ok = jnp.repeat(jnp.transpose(okb, (0, 2, 1, 3)), SEL_BLOCK, axis=-1)
        mask = (ok & (kpos <= qp[None, None, :, None]))[:, :, None]
        logits = jnp.einsum('bqhgd,bhqxd->bhgqx', qb, kb) * scale
        p = _masked_probs(logits, mask)
        return jnp.einsum('bhgqx,bhqxd->bqhgd', p.astype(vb.dtype), vb)

    o_sel = _query_sweep(sel_block, _sel_rows(L, B), q_pos, q_rot, top_idx, top_ok)

    if nsa_past is None:
        kw_pad = jnp.pad(kw, ((0, 0), (WINDOW, 0), (0, 0), (0, 0)))
        vw_pad = jnp.pad(vw, ((0, 0), (WINDOW, 0), (0, 0), (0, 0)))

        def win_block(qp, qb):
            nq = qb.shape[1]
            start = qp[0]
            kb = lax.dynamic_slice_in_dim(kw_pad, start, WINDOW + nq, axis=1)
            vb = lax.dynamic_slice_in_dim(vw_pad, start, WINDOW + nq, axis=1)
            return _band_attend(qp, qb, kb, vb, start - WINDOW + jnp.arange(WINDOW + nq, dtype=jnp.int32))

        o_win = _query_sweep(win_block, WIN_Q_BLOCK, q_pos, q_rot)
        keep = min(WINDOW, L)
        new_wk, new_wv = kw[:, L - keep:], vw[:, L - keep:]
    else:
        kw_all = jnp.concatenate([pwk, kw], axis=1)
        vw_all = jnp.concatenate([pwv, vw], axis=1)
        w_pos = q_pos[0] - pwk.shape[1] + jnp.arange(pwk.shape[1] + L, dtype=jnp.int32)
        o_win = _band_attend(q_pos, q_rot, kw_all, vw_all, w_pos)
        new_wk, new_wv = kw_all[:, L:], vw_all[:, L:]

    o = gates[..., 0:1] * o_cmp + gates[..., 1:2] * o_sel + gates[..., 2:3] * o_win
    out = o.reshape(B, L, C_WIDTH) @ w_out
    return out, (kc, vc, ks, vs, new_wk, new_wv)


def _mem_kv(mem, g, w_kv):
    B, M, _ = mem.shape
    k, v = jnp.split(_rmsnorm(mem, g) @ w_kv, 2, axis=-1)
    return k.reshape(B, M, MEM_HEADS, MEM_DH), v.reshape(B, M, MEM_HEADS, MEM_DH)


def _mem_attend(h, mk, mv, w_q, w_o):
    B, L, _ = h.shape
    q = (h @ w_q).reshape(B, L, MEM_HEADS, MEM_DH)
    logits = jnp.einsum('blhd,bmhd->bhlm', q, mk).astype(jnp.float32) * MEM_DH ** -0.5
    p = jax.nn.softmax(logits, axis=-1).astype(mv.dtype)
    return jnp.einsum('bhlm,bmhd->blhd', p, mv).reshape(B, L, MEM_WIDTH) @ w_o


def _swiglu(h, w_in, w_out):
    gate, up = jnp.split(h @ w_in, 2, axis=-1)
    return (jax.nn.silu(gate) * up) @ w_out


def _stack_layers(states):
    return [jnp.stack(a) for a in zip(*states)]


def setup_inputs(seed: int = 0) -> dict:
    key = jax.random.key(seed)
    keys = iter(jax.random.split(key, 64))
    f32 = jnp.float32

    def nrm(shape, scale=1.0):
        return scale * jax.random.normal(next(keys), shape, f32)

    def gain(shape):
        return 1.0 + nrm(shape, 0.05)

    n_pages = PAST_LEN // PAGE_SIZE
    n_used = DEC_BATCH * n_pages
    n_pool = n_used + n_used // 4
    w_buf = min(WINDOW, PAST_LEN)
    out_scale = (3 * DEPTH) ** -0.5
    paged_nsa = (N_ODD, n_pool, PAGE_SIZE, NSA_KV_HEADS, NSA_DH)
    inputs = {
        "x_prompt": nrm((BATCH, SEQ, D_MODEL)),
        "x_sample": nrm((DEC_BATCH, DEC_SEQ, D_MODEL)),
        "state_gdn_s": nrm((N_EVEN, DEC_BATCH, GDN_HEADS, GDN_DK, GDN_DV), 0.1),
        "state_gdn_conv": nrm((N_EVEN, DEC_BATCH, GDN_CONV - 1, A_CONV_CH)),
        "cache_fox_k": nrm((N_EVEN, n_pool, PAGE_SIZE, FOX_HEADS, FOX_DH)),
        "cache_fox_v": nrm((N_EVEN, n_pool, PAGE_SIZE, FOX_HEADS, FOX_DH)),
        "cache_fox_logf": jax.nn.log_sigmoid(3.0 + nrm((N_EVEN, n_pool, PAGE_SIZE, FOX_HEADS))),
        "cache_nsa_cmp_k": nrm(paged_nsa),
        "cache_nsa_cmp_v": nrm(paged_nsa),
        "cache_nsa_sel_k": nrm(paged_nsa),
        "cache_nsa_sel_v": nrm(paged_nsa),
        "cache_nsa_win_k": nrm((N_ODD, DEC_BATCH, w_buf, NSA_KV_HEADS, NSA_DH)),
        "cache_nsa_win_v": nrm((N_ODD, DEC_BATCH, w_buf, NSA_KV_HEADS, NSA_DH)),
        "cache_mem_k": nrm((DEPTH, DEC_BATCH, MEM_TOKENS, MEM_HEADS, MEM_DH)),
        "cache_mem_v": nrm((DEPTH, DEC_BATCH, MEM_TOKENS, MEM_HEADS, MEM_DH)),
        "page_table": jax.random.permutation(next(keys), n_pool)[:n_used].reshape(DEC_BATCH, n_pages).astype(jnp.int32),
        "mem_prompt": nrm((BATCH, MEM_TOKENS, D_MODEL)),
        "norm_mix": gain((DEPTH, D_MODEL)),
        "norm_xattn": gain((DEPTH, D_MODEL)),
        "norm_mem": gain((DEPTH, D_MODEL)),
        "norm_ffn": gain((DEPTH, D_MODEL)),
        "norm_final": gain((D_MODEL,)),
        "w_in_even": nrm((N_EVEN, D_MODEL, W_IN_EVEN), D_MODEL ** -0.5),
        "b_forget": 3.0 + nrm((N_EVEN, FOX_HEADS), 0.5),
        "gdn_conv_w": nrm((N_EVEN, GDN_CONV, A_CONV_CH), GDN_CONV ** -0.5),
        "gdn_a_log": jnp.log(jax.random.uniform(next(keys), (N_EVEN, GDN_HEADS), f32, 1.0, 16.0)),
    }
    dt = jnp.exp(jax.random.uniform(next(keys), (N_EVEN, GDN_HEADS), f32, math.log(1e-3), math.log(1e-1)))
    inputs["gdn_dt_bias"] = dt + jnp.log(-jnp.expm1(-dt))
    inputs["gdn_norm"] = gain((N_EVEN, GDN_DV))
    inputs["w_out_even"] = nrm((N_EVEN, A_WIDTH + B_WIDTH, D_MODEL), (A_WIDTH + B_WIDTH) ** -0.5 * out_scale)
    inputs["w_in_odd"] = nrm((N_ODD, D_MODEL, W_IN_ODD), D_MODEL ** -0.5)
    inputs["nsa_gate_bias"] = nrm((N_ODD, 3 * NSA_HEADS), 0.1)
    inputs["nsa_cmp_pos"] = nrm((N_ODD, 2, CMP_LEN, NSA_DH), 0.1)
    inputs["nsa_cmp_w1"] = nrm((N_ODD, 2, CMP_LEN * NSA_DH, CMP_HID), (CMP_LEN * NSA_DH) ** -0.5)
    inputs["nsa_cmp_w2"] = nrm((N_ODD, 2, CMP_HID, NSA_DH), CMP_HID ** -0.5)
    inputs["w_out_odd"] = nrm((N_ODD, C_WIDTH, D_MODEL), C_WIDTH ** -0.5 * out_scale)
    inputs["w_mem_q"] = nrm((DEPTH, D_MODEL, MEM_WIDTH), D_MODEL ** -0.5)
    inputs["w_mem_kv"] = nrm((DEPTH, D_MODEL, 2 * MEM_WIDTH), D_MODEL ** -0.5)
    inputs["w_mem_o"] = nrm((DEPTH, MEM_WIDTH, D_MODEL), MEM_WIDTH ** -0.5 * out_scale)
    inputs["w_ffn_in"] = nrm((DEPTH, D_MODEL, 2 * FFN_HIDDEN), D_MODEL ** -0.5)
    inputs["w_ffn_out"] = nrm((DEPTH, FFN_HIDDEN, D_MODEL), FFN_HIDDEN ** -0.5 * out_scale)
    return inputs


def reference(x_prompt, x_sample, state_gdn_s, state_gdn_conv, cache_fox_k, cache_fox_v, cache_fox_logf,
              cache_nsa_cmp_k, cache_nsa_cmp_v, cache_nsa_sel_k, cache_nsa_sel_v, cache_nsa_win_k, cache_nsa_win_v,
              cache_mem_k, cache_mem_v, page_table, mem_prompt, norm_mix, norm_xattn, norm_mem, norm_ffn,
              norm_final, w_in_even, b_forget, gdn_conv_w, gdn_a_log, gdn_dt_bias, gdn_norm, w_out_even,
              w_in_odd, nsa_gate_bias, nsa_cmp_pos, nsa_cmp_w1, nsa_cmp_w2, w_out_odd, w_mem_q, w_mem_kv,
              w_mem_o, w_ffn_in, w_ffn_out):
    B, L, _ = x_prompt.shape
    past_len = page_table.shape[1] * cache_fox_k.shape[2]
    pos_p = jnp.arange(L, dtype=jnp.int32)
    pos_s = past_len + jnp.arange(x_sample.shape[1], dtype=jnp.int32)
    xp, xs = x_prompt, x_sample
    even_p, even_s, odd_p, odd_s, mem_p = [], [], [], [], []
    for layer in range(DEPTH):
        hp = _rmsnorm(xp, norm_mix[layer])
        hs = _rmsnorm(xs, norm_mix[layer])
        if layer % 2 == 0:
            e = layer // 2
            wts = (w_in_even[e], b_forget[e], gdn_conv_w[e], gdn_a_log[e], gdn_dt_bias[e], gdn_norm[e], w_out_even[e])
            conv0 = jnp.zeros((B, GDN_CONV - 1, A_CONV_CH), xp.dtype)
            s0 = jnp.zeros((B, GDN_HEADS, GDN_DK, GDN_DV), jnp.float32)
            mp, st_p = _even_mixer(hp, pos_p, conv0, s0, None, *wts)
            fox_past = (_gather_pages(cache_fox_k[e], page_table), _gather_pages(cache_fox_v[e], page_table),
                        _gather_pages(cache_fox_logf[e], page_table))
            ms, st_s = _even_mixer(hs, pos_s, state_gdn_conv[e], state_gdn_s[e], fox_past, *wts)
            even_p.append(st_p)
            even_s.append(st_s)
        else:
            o = layer // 2
            wts = (w_in_odd[o], nsa_gate_bias[o], nsa_cmp_pos[o], nsa_cmp_w1[o], nsa_cmp_w2[o], w_out_odd[o])
            mp, st_p = _odd_mixer(hp, pos_p, None, *wts)
            nsa_past = (_gather_pages(cache_nsa_cmp_k[o], page_table), _gather_pages(cache_nsa_cmp_v[o], page_table),
                        _gather_pages(cache_nsa_sel_k[o], page_table), _gather_pages(cache_nsa_sel_v[o], page_table),
                        cache_nsa_win_k[o], cache_nsa_win_v[o])
            ms, st_s = _odd_mixer(hs, pos_s, nsa_past, *wts)
            odd_p.append(st_p)
            odd_s.append(st_s)
        xp = xp + mp
        xs = xs + ms
        mk, mv = _mem_kv(mem_prompt, norm_mem[layer], w_mem_kv[layer])
        mem_p.append((mk, mv))
        xp = xp + _mem_attend(_rmsnorm(xp, norm_xattn[layer]), mk, mv, w_mem_q[layer], w_mem_o[layer])
        xs = xs + _mem_attend(_rmsnorm(xs, norm_xattn[layer]), cache_mem_k[layer], cache_mem_v[layer],
                              w_mem_q[layer], w_mem_o[layer])
        xp = xp + _swiglu(_rmsnorm(xp, norm_ffn[layer]), w_ffn_in[layer], w_ffn_out[layer])
        xs = xs + _swiglu(_rmsnorm(xs, norm_ffn[layer]), w_ffn_in[layer], w_ffn_out[layer])
    y_prompt = _rmsnorm(xp, norm_final)
    y_sample = _rmsnorm(xs, norm_final)
    p_gdn_s, p_gdn_conv, p_fox_k, p_fox_v, p_fox_logf = _stack_layers(even_p)
    s_gdn_s, s_gdn_conv, s_fox_k, s_fox_v, s_fox_logf = _stack_layers(even_s)
    p_cmp_k, p_cmp_v, p_sel_k, p_sel_v, p_win_k, p_win_v = _stack_layers(odd_p)
    s_cmp_k, s_cmp_v, s_sel_k, s_sel_v, s_win_k, s_win_v = _stack_layers(odd_s)
    p_mem_k, p_mem_v = _stack_layers(mem_p)
    return (y_prompt, y_sample,
            p_gdn_s, p_gdn_conv, p_fox_k, p_fox_v, p_fox_logf,
            p_cmp_k, p_cmp_v, p_sel_k, p_sel_v, p_win_k, p_win_v, p_mem_k, p_mem_v,
            s_gdn_s, s_gdn_conv, s_fox_k, s_fox_v, s_fox_logf,
            s_cmp_k, s_cmp_v, s_sel_k, s_sel_v, s_win_k, s_win_v)
```

```python
import functools
import math

import jax
import jax.numpy as jnp
import numpy as np
from jax import lax
from jax.experimental import pallas as pl
from jax.experimental.pallas import tpu as pltpu

D_MODEL = 1024
RMS_EPS = 1e-6
NEG_BIG = -1e30

GDN_HEADS = 4
GDN_DK = 128
GDN_DV = 128
GDN_CONV = 4
GDN_CHUNK = 64
A_QK = GDN_HEADS * GDN_DK
A_WIDTH = GDN_HEADS * GDN_DV
A_CONV_CH = 2 * A_QK + A_WIDTH

FOX_HEADS = 8
FOX_DH = 64
B_WIDTH = FOX_HEADS * FOX_DH
FOX_Q_BLOCK = 128

NSA_HEADS = 16
NSA_KV_HEADS = 4
NSA_GROUP = NSA_HEADS // NSA_KV_HEADS
NSA_DH = 64
C_WIDTH = NSA_HEADS * NSA_DH
C_KV_WIDTH = NSA_KV_HEADS * NSA_DH
CMP_LEN = 32
CMP_STRIDE = CMP_LEN // 2
SEL_BLOCK = 64
SEL_TOP_N = 16
WINDOW = 512
WIN_Q_BLOCK = 128
SEL_SWEEP_ROWS = 256

ROPE_THETA = 500000.0
ROT_DIM = NSA_DH // 4

MEM_TOKENS = 256
MEM_HEADS = 4
MEM_DH = D_MODEL // MEM_HEADS

LANES = 128
VMEM_LIMIT_BYTES = 56 * 1024 * 1024
ROW_TILE = 512

_BF16 = jnp.bfloat16
_F32 = jnp.float32


def _params(n_grid_dims):
    return pltpu.CompilerParams(
        dimension_semantics=("arbitrary",) * n_grid_dims,
        vmem_limit_bytes=VMEM_LIMIT_BYTES)


def _rms_rows(x, g):
    return x * lax.rsqrt(jnp.mean(x * x, axis=-1, keepdims=True) + RMS_EPS) * g


def _pad_cols(w, mult=LANES):
    pad = (-w.shape[-1]) % mult
    return w if pad == 0 else jnp.pad(w, ((0, 0), (0, pad)))


def _rms_matmul_kernel(x_ref, g_ref, w_ref, o_ref, *, col_chunk):
    xn = _rms_rows(x_ref[...], g_ref[...]).astype(_BF16)
    width = w_ref.shape[1]
    for c0 in range(0, width, col_chunk):
        c1 = min(width, c0 + col_chunk)
        o_ref[:, c0:c1] = jnp.dot(xn, w_ref[:, c0:c1], preferred_element_type=_F32).astype(o_ref.dtype)


def _rms_matmul(x, g, w_bf16, out_dtype=_F32, row_tile=ROW_TILE):
    n, d = x.shape
    width = w_bf16.shape[1]
    assert n % row_tile == 0 and width % LANES == 0
    return pl.pallas_call(
        functools.partial(_rms_matmul_kernel, col_chunk=512),
        grid=(n // row_tile,),
        in_specs=[pl.BlockSpec((row_tile, d), lambda i: (i, 0)),
                  pl.BlockSpec((1, d), lambda i: (0, 0)),
                  pl.BlockSpec((d, width), lambda i: (0, 0))],
        out_specs=pl.BlockSpec((row_tile, width), lambda i: (i, 0)),
        out_shape=jax.ShapeDtypeStruct((n, width), out_dtype),
        compiler_params=_params(1),
        name="rms_matmul",
    )(x, g.reshape(1, d), w_bf16)


def _matmul_residual_kernel(a_ref, w_ref, r_ref, o_ref):
    o_ref[...] = r_ref[...] + jnp.dot(a_ref[...].astype(_BF16), w_ref[...], preferred_element_type=_F32)


def _matmul_residual(a, w_bf16, resid, row_tile=ROW_TILE):
    n, k = a.shape
    d = w_bf16.shape[1]
    assert n % row_tile == 0
    return pl.pallas_call(
        _matmul_residual_kernel,
        grid=(n // row_tile,),
        in_specs=[pl.BlockSpec((row_tile, k), lambda i: (i, 0)),
                  pl.BlockSpec((k, d), lambda i: (0, 0)),
                  pl.BlockSpec((row_tile, d), lambda i: (i, 0))],
        out_specs=pl.BlockSpec((row_tile, d), lambda i: (i, 0)),
        out_shape=jax.ShapeDtypeStruct((n, d), _F32),
        input_output_aliases={2: 0},
        compiler_params=_params(1),
        name="matmul_residual",
    )(a, w_bf16, resid)


def _ffn_kernel(x_ref, g_ref, wg_ref, wu_ref, wo_ref, o_ref, *, chunk):
    x = x_ref[...]
    xn = _rms_rows(x, g_ref[...]).astype(_BF16)
    o_ref[...] = x
    hidden = wg_ref.shape[1]
    for c0 in range(0, hidden, chunk):
        gate = jnp.dot(xn, wg_ref[:, c0:c0 + chunk], preferred_element_type=_F32)
        up = jnp.dot(xn, wu_ref[:, c0:c0 + chunk], preferred_element_type=_F32)
        h = (gate * jax.nn.sigmoid(gate) * up).astype(_BF16)
        o_ref[...] += jnp.dot(h, wo_ref[c0:c0 + chunk, :], preferred_element_type=_F32)


def _ffn(x, g, wg, wu, wo, row_tile=ROW_TILE):
    n, d = x.shape
    hidden = wg.shape[1]
    chunk = 256
    assert n % row_tile == 0 and hidden % chunk == 0
    const = lambda i: (0, 0)
    return pl.pallas_call(
        functools.partial(_ffn_kernel, chunk=chunk),
        grid=(n // row_tile,),
        in_specs=[pl.BlockSpec((row_tile, d), lambda i: (i, 0)),
                  pl.BlockSpec((1, d), const),
                  pl.BlockSpec((d, hidden), const, pipeline_mode=pl.Buffered(1)),
                  pl.BlockSpec((d, hidden), const, pipeline_mode=pl.Buffered(1)),
                  pl.BlockSpec((hidden, d), const, pipeline_mode=pl.Buffered(1))],
        out_specs=pl.BlockSpec((row_tile, d), lambda i: (i, 0)),
        out_shape=jax.ShapeDtypeStruct((n, d), _F32),
        input_output_aliases={0: 0},
        compiler_params=_params(1),
        name="ffn",
    )(x, g.reshape(1, d), wg, wu, wo)


def _mem_attn_kernel(q_ref, k_ref, v_ref, o_ref):
    scale = MEM_DH ** -0.5
    for h in range(MEM_HEADS):
        cols = slice(h * MEM_DH, (h + 1) * MEM_DH)
        q = q_ref[:, cols].astype(_BF16)
        k = k_ref[:, cols].astype(_BF16)
        v = v_ref[:, cols].astype(_BF16)
        s = lax.dot_general(q, k, (((1,), (1,)), ((), ())), preferred_element_type=_F32) * scale
        p = jnp.exp(s - jnp.max(s, axis=-1, keepdims=True))
        inv = 1.0 / jnp.sum(p, axis=-1, keepdims=True)
        o = jnp.dot(p.astype(_BF16), v, preferred_element_type=_F32) * inv
        o_ref[:, cols] = o.astype(o_ref.dtype)


def _mem_attn(q, k, v, *, n_batch, q_len, q_row0, q_tile, k_col_block, v_col_block):
    assert q_len % q_tile == 0 and q_row0 % q_tile == 0
    tiles = q_len // q_tile
    t0 = q_row0 // q_tile
    width = MEM_HEADS * MEM_DH
    return pl.pallas_call(
        _mem_attn_kernel,
        grid=(n_batch, tiles),
        in_specs=[pl.BlockSpec((q_tile, width), lambda b, i: (t0 + b * tiles + i, 0)),
                  pl.BlockSpec((MEM_TOKENS, width), lambda b, i: (b, k_col_block)),
                  pl.BlockSpec((MEM_TOKENS, width), lambda b, i: (b, v_col_block))],
        out_specs=pl.BlockSpec((q_tile, width), lambda b, i: (b * tiles + i, 0)),
        out_shape=jax.ShapeDtypeStruct((n_batch * q_len, width), _BF16 if q_tile % 16 == 0 else _F32),
        compiler_params=_params(2),
        name="mem_attn",
    )(q, k, v)


def _rms_kernel(x_ref, g_ref, o_ref):
    o_ref[...] = _rms_rows(x_ref[...], g_ref[...])


def _rms(x, g, row_tile=ROW_TILE):
    n, d = x.shape
    return pl.pallas_call(
        _rms_kernel,
        grid=(n // row_tile,),
        in_specs=[pl.BlockSpec((row_tile, d), lambda i: (i, 0)),
                  pl.BlockSpec((1, d), lambda i: (0, 0))],
        out_specs=pl.BlockSpec((row_tile, d), lambda i: (i, 0)),
        out_shape=jax.ShapeDtypeStruct((n, d), _F32),
        compiler_params=_params(1),
        name="final_rms",
    )(x, g.reshape(1, d))


def _split(x, sizes):
    return jnp.split(x, np.cumsum(sizes)[:-1].tolist(), axis=-1)


def _l2norm(x):
    return x * lax.rsqrt(jnp.sum(x * x, axis=-1, keepdims=True) + 1e-6)


def _rmsnorm(x, g):
    return x * lax.rsqrt(jnp.mean(x * x, axis=-1, keepdims=True) + RMS_EPS) * g


def _pad_rows(x, t_pad):
    return jnp.pad(x, [(0, 0), (0, t_pad - x.shape[1])] + [(0, 0)] * (x.ndim - 2))


def _masked_probs(logits, mask):
    lf = jnp.where(mask, logits, NEG_BIG)
    m = jnp.max(lf, axis=-1, keepdims=True)
    p = jnp.where(mask, jnp.exp(lf - m), 0.0)
    return p / jnp.maximum(jnp.sum(p, axis=-1, keepdims=True), 1e-30)


def _copy_kernel(pt_ref, src_ref, dst_ref):
    del pt_ref
    dst_ref[0] = src_ref[...]


def _gather_pages(pool, page_table):
    n_pool, page = pool.shape[:2]
    feat = int(np.prod(pool.shape[2:]))
    nb, n_pages = page_table.shape
    rows, cols = (page, feat) if feat % LANES == 0 else (1, page * feat)
    out = pl.pallas_call(
        _copy_kernel,
        grid_spec=pltpu.PrefetchScalarGridSpec(
            num_scalar_prefetch=1,
            grid=(nb, n_pages),
            in_specs=[pl.BlockSpec((1, rows, cols), lambda b, p, pt: (pt[b, p], 0, 0))],
            out_specs=pl.BlockSpec((1, 1, rows, cols), lambda b, p, pt: (b, p, 0, 0))),
        out_shape=jax.ShapeDtypeStruct((nb, n_pages, rows, cols), pool.dtype),
        compiler_params=_params(2),
        name="gather_pages",
    )(page_table, pool.reshape(n_pool, rows, cols))
    return out.reshape(nb, n_pages * page, *pool.shape[2:])


def _query_sweep(fn, block, q_pos, *q_args):
    L = q_pos.shape[0]
    if block >= L or L % block:
        return fn(q_pos, *q_args)
    n = L // block

    def split(a):
        return jnp.moveaxis(a.reshape(a.shape[0], n, block, *a.shape[2:]), 1, 0)

    out = lax.map(lambda xs: fn(*xs), (q_pos.reshape(n, block),) + tuple(split(a) for a in q_args))
    return jnp.moveaxis(out, 0, 1).reshape(out.shape[1], L, *out.shape[3:])


def _partial_rope(x, pos):
    half = ROT_DIM // 2
    inv_freq = ROPE_THETA ** (-jnp.arange(half, dtype=_F32) / half)
    ang = pos.astype(_F32)[:, None] * inv_freq[None, :]
    shape = (1, pos.shape[0]) + (1,) * (x.ndim - 3) + (half,)
    cos, sin = jnp.cos(ang).reshape(shape), jnp.sin(ang).reshape(shape)
    xr = x[..., :ROT_DIM]
    x1, x2 = xr[..., :half], xr[..., half:]
    rot = jnp.concatenate([x1 * cos - x2 * sin, x2 * cos + x1 * sin], axis=-1)
    return jnp.concatenate([rot, x[..., ROT_DIM:]], axis=-1)


def _short_conv(x, buf, w):
    L = x.shape[1]
    ext = jnp.concatenate([buf, x], axis=1)
    y = sum(ext[:, i:i + L] * w[i] for i in range(GDN_CONV))
    return jax.nn.silu(y), ext[:, L:]


def _gated_delta(q, k, v, g, beta, s0):
    B, L, H, DK = q.shape
    DV = v.shape[-1]
    C = min(GDN_CHUNK, L)
    n = -(-L // C)
    pad = n * C - L

    def blocks(x):
        x = jnp.pad(x, [(0, 0), (0, pad)] + [(0, 0)] * (x.ndim - 2))
        x = x.reshape(B, n, C, *x.shape[2:])
        return jnp.transpose(x, (1, 0, 3, 2) + tuple(range(4, x.ndim)))

    qc = blocks(q) * DK ** -0.5
    kc, vc, bc = blocks(k), blocks(v), blocks(beta)
    gc = jnp.cumsum(blocks(g), axis=-1)
    idx = jnp.arange(C)
    causal = idx[:, None] >= idx[None, :]
    strict = idx[:, None] > idx[None, :]
    decay = jnp.exp(jnp.where(causal, gc[..., :, None] - gc[..., None, :], -jnp.inf))
    kb = kc * bc[..., None]
    a_mat = jnp.where(strict, jnp.einsum('nbhid,nbhjd->nbhij', kb, kc) * decay, 0.0)
    t_mat = a_mat + jnp.eye(C, dtype=_F32)
    rhs = jnp.concatenate([vc * bc[..., None], kb * jnp.exp(gc)[..., None]], axis=-1)
    sol = lax.linalg.triangular_solve(t_mat, rhs, left_side=True, lower=True, unit_diagonal=True)
    u, w = sol[..., :DV], sol[..., DV:]
    qk = jnp.where(causal, jnp.einsum('nbhid,nbhjd->nbhij', qc, kc) * decay, 0.0)

    def step(s, xs):
        q_i, k_i, u_i, w_i, g_i, qk_i = xs
        v_new = u_i - jnp.einsum('bhck,bhkv->bhcv', w_i, s)
        o_i = (jnp.einsum('bhck,bhkv->bhcv', q_i * jnp.exp(g_i)[..., None], s)
               + jnp.einsum('bhij,bhjv->bhiv', qk_i, v_new))
        g_last = g_i[..., -1:]
        s = s * jnp.exp(g_last)[..., None] + jnp.einsum('bhck,bhcv->bhkv', k_i * jnp.exp(g_last - g_i)[..., None], v_new)
        return s, o_i

    s_fin, o = lax.scan(step, s0, (qc, kc, u, w, gc, qk))
    o = jnp.transpose(o, (1, 0, 3, 2, 4)).reshape(B, n * C, H, DV)[:, :L]
    return o, s_fin


def _fox_attend(q_pos, q, c_q, k, v, c_k, k_pos):
    logits = jnp.einsum('bqhd,bkhd->bhqk', q, k) * FOX_DH ** -0.5
    logits = logits + (jnp.transpose(c_q, (0, 2, 1))[..., :, None] - jnp.transpose(c_k, (0, 2, 1))[:, :, None, :])
    mask = (k_pos[None, :] <= q_pos[:, None])[None, None]
    p = _masked_probs(logits, mask)
    return jnp.einsum('bhqk,bkhd->bqhd', p, v)


def _band_attend(q_pos, q, k, v, k_pos):
    logits = jnp.einsum('bqhgd,bkhd->bhgqk', q, k) * NSA_DH ** -0.5
    dist = q_pos[:, None] - k_pos[None, :]
    mask = (dist >= 0) & (dist < WINDOW) & (k_pos[None, :] >= 0)
    p = _masked_probs(logits, mask[None, None, None])
    return jnp.einsum('bhgqk,bkhd->bqhgd', p, v)


def _compress(x, pos_emb, w1, w2):
    B, T, KV, dh = x.shape
    sub = x.reshape(B, T // CMP_STRIDE, CMP_STRIDE, KV, dh)
    blocks = jnp.concatenate([sub[:, :-1], sub[:, 1:]], axis=2) + pos_emb[None, None, :, None, :]
    flat = jnp.transpose(blocks, (0, 1, 3, 2, 4)).reshape(B, T // CMP_STRIDE - 1, KV, CMP_LEN * dh)
    return jax.nn.silu(flat @ w1) @ w2


def _sel_rows(L, B):
    r = max(1, min(L, SEL_SWEEP_ROWS // B))
    while L % r:
        r -= 1
    return r


def _even_mixer(proj, q_pos, conv_buf, s0, fox_past, b_forget, conv_w, a_log, dt_bias, gdn_norm):
    B, L, _ = proj.shape
    qkv_a, a_in, b_in, z, q_b, k_b, v_b, f_in = _split(
        proj, [A_CONV_CH, GDN_HEADS, GDN_HEADS, A_WIDTH, B_WIDTH, B_WIDTH, B_WIDTH, FOX_HEADS])
    qkv_a, new_buf = _short_conv(qkv_a, conv_buf, conv_w)
    q_a, k_a, v_a = _split(qkv_a, [A_QK, A_QK, A_WIDTH])
    q_a = _l2norm(q_a.reshape(B, L, GDN_HEADS, GDN_DK))
    k_a = _l2norm(k_a.reshape(B, L, GDN_HEADS, GDN_DK))
    v_a = v_a.reshape(B, L, GDN_HEADS, GDN_DV)
    g = -jnp.exp(a_log) * jax.nn.softplus(a_in + dt_bias)
    beta = jax.nn.sigmoid(b_in)
    o_a, s_new = _gated_delta(q_a, k_a, v_a, g, beta, s0)
    o_a = _rmsnorm(o_a, gdn_norm) * jax.nn.silu(z.reshape(B, L, GDN_HEADS, GDN_DV))
    q_b = q_b.reshape(B, L, FOX_HEADS, FOX_DH)
    k_b = k_b.reshape(B, L, FOX_HEADS, FOX_DH)
    v_b = v_b.reshape(B, L, FOX_HEADS, FOX_DH)
    logf = jax.nn.log_sigmoid(f_in + b_forget)
    if fox_past is None:
        k_all, v_all, logf_all = k_b, v_b, logf
    else:
        pk, pv, plogf = fox_past
        k_all = jnp.concatenate([pk, k_b], axis=1)
        v_all = jnp.concatenate([pv, v_b], axis=1)
        logf_all = jnp.concatenate([plogf, logf], axis=1)
    c_all = jnp.cumsum(logf_all, axis=1)
    k_pos = jnp.arange(k_all.shape[1], dtype=jnp.int32)
    o_b = _query_sweep(lambda qp, qb, cb: _fox_attend(qp, qb, cb, k_all, v_all, c_all, k_pos),
                       FOX_Q_BLOCK, q_pos, q_b, c_all[:, -L:])
    out = jnp.concatenate([o_a.reshape(B, L, A_WIDTH), o_b.reshape(B, L, B_WIDTH)], axis=-1)
    return out, (s_new, new_buf, k_b, v_b, logf)


def _odd_mixer(proj, q_pos, nsa_past, gate_bias, cmp_pos, cmp_w1, cmp_w2):
    B, L, _ = proj.shape
    scale = NSA_DH ** -0.5
    q, kc, vc, ks, vs, kw, vw, gl = _split(proj, [C_WIDTH] + [C_KV_WIDTH] * 6 + [3 * NSA_HEADS])
    q = q.reshape(B, L, NSA_KV_HEADS, NSA_GROUP, NSA_DH)
    kc, vc, ks, vs, kw, vw = [a.reshape(B, L, NSA_KV_HEADS, NSA_DH) for a in (kc, vc, ks, vs, kw, vw)]
    q_rot = _partial_rope(q, q_pos)
    ks = _partial_rope(ks, q_pos)
    kw = _partial_rope(kw, q_pos)
    gates = jax.nn.sigmoid(gl + gate_bias).reshape(B, L, NSA_KV_HEADS, NSA_GROUP, 3)
    if nsa_past is None:
        kc_all, vc_all, ks_all, vs_all = kc, vc, ks, vs
    else:
        pkc, pvc, pks, pvs, pwk, pwv = nsa_past
        kc_all = jnp.concatenate([pkc, kc], axis=1)
        vc_all = jnp.concatenate([pvc, vc], axis=1)
        ks_all = jnp.concatenate([pks, ks], axis=1)
        vs_all = jnp.concatenate([pvs, vs], axis=1)
    t_pad = -(-kc_all.shape[1] // SEL_BLOCK) * SEL_BLOCK
    kc_all, vc_all, ks_all, vs_all = [_pad_rows(a, t_pad) for a in (kc_all, vc_all, ks_all, vs_all)]
    n_sel = t_pad // SEL_BLOCK

    kcb = _compress(kc_all, cmp_pos[0], cmp_w1[0], cmp_w2[0])
    vcb = _compress(vc_all, cmp_pos[1], cmp_w1[1], cmp_w2[1])
    n_cmp = kcb.shape[1]
    c_start = jnp.arange(n_cmp, dtype=jnp.int32) * CMP_STRIDE
    cmask = (c_start + CMP_LEN - 1)[None, :] <= q_pos[:, None]
    p_cmp = _masked_probs(jnp.einsum('blhgd,bnhd->blhgn', q, kcb) * scale, cmask[None, :, None, None, :])
    o_cmp = jnp.einsum('blhgn,bnhd->blhgd', p_cmp, vcb)

    s_start = jnp.arange(n_sel, dtype=jnp.int32) * SEL_BLOCK
    shared_len = (jnp.minimum(c_start[:, None] + CMP_LEN, s_start[None, :] + SEL_BLOCK)
                  - jnp.maximum(c_start[:, None], s_start[None, :]))
    cmp_to_sel = jnp.clip(shared_len, 0).astype(_F32) / CMP_LEN
    imp = jnp.einsum('blhgn,nj->blhj', p_cmp, cmp_to_sel, precision=lax.Precision.HIGHEST)
    blk = jnp.arange(n_sel, dtype=jnp.int32)[None, :]
    cur = (q_pos // SEL_BLOCK)[:, None]
    causal_blk = (blk <= cur)[None, :, None, :]
    forced = ((blk == 0) | (blk == cur) | (blk == cur - 1))[None, :, None, :]
    imp = jnp.where(causal_blk, jnp.where(forced, -NEG_BIG, imp), NEG_BIG)
    ahead = (imp[..., None, :] > imp[..., :, None]) | (
        (imp[..., None, :] == imp[..., :, None]) & (blk[0][None, :] < blk[0][:, None]))
    rank = jnp.sum(ahead, axis=-1)
    sel = (rank < min(SEL_TOP_N, n_sel)) & (imp > 0.5 * NEG_BIG)

    k_pos_all = jnp.arange(t_pad, dtype=jnp.int32)
    sel_keys = jnp.repeat(sel, SEL_BLOCK, axis=-1)
    smask = sel_keys & (k_pos_all[None, None, None, :] <= q_pos[None, :, None, None])
    smask = jnp.transpose(smask, (0, 2, 1, 3))[:, :, None]
    logits = jnp.einsum('bqhgd,bkhd->bhgqk', q_rot, ks_all) * scale
    p_sel = _masked_probs(logits, smask)
    o_sel = jnp.einsum('bhgqk,bkhd->bqhgd', p_sel, vs_all)

    if nsa_past is None:
        kw_pad = jnp.pad(kw, ((0, 0), (WINDOW, 0), (0, 0), (0, 0)))
        vw_pad = jnp.pad(vw, ((0, 0), (WINDOW, 0), (0, 0), (0, 0)))

        def win_block(qp, qb):
            nq = qb.shape[1]
            start = qp[0]
            kb = lax.dynamic_slice_in_dim(kw_pad, start, WINDOW + nq, axis=1)
            vb = lax.dynamic_slice_in_dim(vw_pad, start, WINDOW + nq, axis=1)
            return _band_attend(qp, qb, kb, vb, start - WINDOW + jnp.arange(WINDOW + nq, dtype=jnp.int32))

        o_win = _query_sweep(win_block, WIN_Q_BLOCK, q_pos, q_rot)
        keep = min(WINDOW, L)
        new_wk, new_wv = kw[:, L - keep:], vw[:, L - keep:]
    else:
        kw_all = jnp.concatenate([pwk, kw], axis=1)
        vw_all = jnp.concatenate([pwv, vw], axis=1)
        w_pos = q_pos[0] - pwk.shape[1] + jnp.arange(pwk.shape[1] + L, dtype=jnp.int32)
        o_win = _band_attend(q_pos, q_rot, kw_all, vw_all, w_pos)
        new_wk, new_wv = kw_all[:, L:], vw_all[:, L:]

    o = gates[..., 0:1] * o_cmp + gates[..., 1:2] * o_sel + gates[..., 2:3] * o_win
    return o.reshape(B, L, C_WIDTH), (kc, vc, ks, vs, new_wk, new_wv)


def _stack_layers(states):
    return [jnp.stack(a) for a in zip(*states)]


def kernel(x_prompt, x_sample, state_gdn_s, state_gdn_conv, cache_fox_k, cache_fox_v, cache_fox_logf, cache_nsa_cmp_k, cache_nsa_cmp_v, cache_nsa_sel_k, cache_nsa_sel_v, cache_nsa_win_k, cache_nsa_win_v, cache_mem_k, cache_mem_v, page_table, mem_prompt, norm_mix, norm_xattn, norm_mem, norm_ffn, norm_final, w_in_even, b_forget, gdn_conv_w, gdn_a_log, gdn_dt_bias, gdn_norm, w_out_even, w_in_odd, nsa_gate_bias, nsa_cmp_pos, nsa_cmp_w1, nsa_cmp_w2, w_out_odd, w_mem_q, w_mem_kv, w_mem_o, w_ffn_in, w_ffn_out):
    B, L, D = x_prompt.shape
    SB, SL, _ = x_sample.shape
    depth = norm_mix.shape[0]
    n_p = B * L
    n_s = SB * SL
    past_len = page_table.shape[1] * cache_fox_k.shape[2]
    pos_p = jnp.arange(L, dtype=jnp.int32)
    pos_s = past_len + jnp.arange(SL, dtype=jnp.int32)
    ffn_hidden = w_ffn_out.shape[1]

    x = jnp.concatenate([x_prompt.reshape(n_p, D), x_sample.reshape(n_s, D)], axis=0)
    mem_flat = mem_prompt.reshape(B * MEM_TOKENS, D)
    even_p, even_s, odd_p, odd_s, mem_p = [], [], [], [], []
    for layer in range(depth):
        if layer % 2 == 0:
            e = layer // 2
            proj = _rms_matmul(x, norm_mix[layer], _pad_cols(w_in_even[e]).astype(_BF16))
            width = w_in_even.shape[2]
            proj_p = proj[:n_p, :width].reshape(B, L, width)
            proj_s = proj[n_p:, :width].reshape(SB, SL, width)
            wts = (b_forget[e], gdn_conv_w[e], gdn_a_log[e], gdn_dt_bias[e], gdn_norm[e])
            conv0 = jnp.zeros((B, GDN_CONV - 1, A_CONV_CH), _F32)
            s0 = jnp.zeros((B, GDN_HEADS, GDN_DK, GDN_DV), _F32)
            mp, st_p = _even_mixer(proj_p, pos_p, conv0, s0, None, *wts)
            fox_past = (_gather_pages(cache_fox_k[e], page_table), _gather_pages(cache_fox_v[e], page_table),
                        _gather_pages(cache_fox_logf[e], page_table))
            ms, st_s = _even_mixer(proj_s, pos_s, state_gdn_conv[e], state_gdn_s[e], fox_past, *wts)
            even_p.append(st_p)
            even_s.append(st_s)
            w_out = w_out_even[e]
        else:
            o = layer // 2
            proj = _rms_matmul(x, norm_mix[layer], _pad_cols(w_in_odd[o]).astype(_BF16))
            width = w_in_odd.shape[2]
            proj_p = proj[:n_p, :width].reshape(B, L, width)
            proj_s = proj[n_p:, :width].reshape(SB, SL, width)
            wts = (nsa_gate_bias[o], nsa_cmp_pos[o], nsa_cmp_w1[o], nsa_cmp_w2[o])
            mp, st_p = _odd_mixer(proj_p, pos_p, None, *wts)
            nsa_past = (_gather_pages(cache_nsa_cmp_k[o], page_table), _gather_pages(cache_nsa_cmp_v[o], page_table),
                        _gather_pages(cache_nsa_sel_k[o], page_table), _gather_pages(cache_nsa_sel_v[o], page_table),
                        cache_nsa_win_k[o], cache_nsa_win_v[o])
            ms, st_s = _odd_mixer(proj_s, pos_s, nsa_past, *wts)
            odd_p.append(st_p)
            odd_s.append(st_s)
            w_out = w_out_odd[o]
        mixed = jnp.concatenate([mp.reshape(n_p, -1), ms.reshape(n_s, -1)], axis=0)
        x = _matmul_residual(mixed, w_out.astype(_BF16), x)

        mkv = _rms_matmul(mem_flat, norm_mem[layer], w_mem_kv[layer].astype(_BF16))
        width = MEM_HEADS * MEM_DH
        mem_p.append((mkv[:, :width].reshape(B, MEM_TOKENS, MEM_HEADS, MEM_DH),
                      mkv[:, width:].reshape(B, MEM_TOKENS, MEM_HEADS, MEM_DH)))
        q = _rms_matmul(x, norm_xattn[layer], w_mem_q[layer].astype(_BF16))
        o_p = _mem_attn(q, mkv, mkv, n_batch=B, q_len=L, q_row0=0, q_tile=512, k_col_block=0, v_col_block=1)
        o_s = _mem_attn(q, cache_mem_k[layer].reshape(SB * MEM_TOKENS, width),
                        cache_mem_v[layer].reshape(SB * MEM_TOKENS, width),
                        n_batch=SB, q_len=SL, q_row0=n_p, q_tile=SL, k_col_block=0, v_col_block=0)
        o_mem = jnp.concatenate([o_p.astype(_F32), o_s], axis=0)
        x = _matmul_residual(o_mem, w_mem_o[layer].astype(_BF16), x)

        x = _ffn(x, norm_ffn[layer], w_ffn_in[layer][:, :ffn_hidden].astype(_BF16),
                 w_ffn_in[layer][:, ffn_hidden:].astype(_BF16), w_ffn_out[layer].astype(_BF16))

    y = _rms(x, norm_final)
    y_prompt = y[:n_p].reshape(B, L, D)
    y_sample = y[n_p:].reshape(SB, SL, D)
    p_gdn_s, p_gdn_conv, p_fox_k, p_fox_v, p_fox_logf = _stack_layers(even_p)
    s_gdn_s, s_gdn_conv, s_fox_k, s_fox_v, s_fox_logf = _stack_layers(even_s)
    p_cmp_k, p_cmp_v, p_sel_k, p_sel_v, p_win_k, p_win_v = _stack_layers(odd_p)
    s_cmp_k, s_cmp_v, s_sel_k, s_sel_v, s_win_k, s_win_v = _stack_layers(odd_s)
    p_mem_k, p_mem_v = _stack_layers(mem_p)
    return (y_prompt, y_sample,
            p_gdn_s, p_gdn_conv, p_fox_k, p_fox_v, p_fox_logf,
            p_cmp_k, p_cmp_v, p_sel_k, p_sel_v, p_win_k, p_win_v, p_mem_k, p_mem_v,
            s_gdn_s, s_gdn_conv, s_fox_k, s_fox_v, s_fox_logf,
            s_cmp_k, s_cmp_v, s_sel_k, s_sel_v, s_win_k, s_win_v)
```

```python
import functools

import jax
import jax.numpy as jnp
import numpy as np
from jax import lax
from jax.experimental import pallas as pl
from jax.experimental.pallas import tpu as pltpu

D_MODEL = 1024
RMS_EPS = 1e-6
NEG_BIG = -1e30

GDN_HEADS = 4
GDN_DK = 128
GDN_DV = 128
GDN_CONV = 4
GDN_CHUNK = 64
A_QK = GDN_HEADS * GDN_DK
A_WIDTH = GDN_HEADS * GDN_DV
A_CONV_CH = 2 * A_QK + A_WIDTH

FOX_HEADS = 8
FOX_DH = 64
B_WIDTH = FOX_HEADS * FOX_DH

NSA_HEADS = 16
NSA_KV_HEADS = 4
NSA_GROUP = NSA_HEADS // NSA_KV_HEADS
NSA_DH = 64
C_WIDTH = NSA_HEADS * NSA_DH
C_KV_WIDTH = NSA_KV_HEADS * NSA_DH
CMP_LEN = 32
CMP_STRIDE = CMP_LEN // 2
SEL_BLOCK = 64
SEL_TOP_N = 16
WINDOW = 512

ROPE_THETA = 500000.0
ROT_DIM = NSA_DH // 4

MEM_TOKENS = 256
MEM_HEADS = 4
MEM_DH = D_MODEL // MEM_HEADS

LANES = 128
VMEM_LIMIT_BYTES = 56 * 1024 * 1024
ROW_TILE = 512
ATTN_TILE = 256

_BF16 = jnp.bfloat16
_F32 = jnp.float32


def _params(n_grid_dims):
    return pltpu.CompilerParams(
        dimension_semantics=("arbitrary",) * n_grid_dims,
        vmem_limit_bytes=VMEM_LIMIT_BYTES)


def _rms_rows(x, g):
    return x * lax.rsqrt(jnp.mean(x * x, axis=-1, keepdims=True) + RMS_EPS) * g


def _rms_matmul_kernel(x_ref, g_ref, w_ref, o_ref, *, col_chunk):
    xn = _rms_rows(x_ref[...], g_ref[...]).astype(_BF16)
    width = w_ref.shape[1]
    for c0 in range(0, width, col_chunk):
        c1 = min(width, c0 + col_chunk)
        o_ref[:, c0:c1] = jnp.dot(xn, w_ref[:, c0:c1], preferred_element_type=_F32).astype(o_ref.dtype)


def _rms_matmul(x, g, w_bf16, out_dtype=_F32, row_tile=ROW_TILE):
    n, d = x.shape
    width = w_bf16.shape[1]
    assert n % row_tile == 0 and width % LANES == 0
    return pl.pallas_call(
        functools.partial(_rms_matmul_kernel, col_chunk=512),
        grid=(n // row_tile,),
        in_specs=[pl.BlockSpec((row_tile, d), lambda i: (i, 0)),
                  pl.BlockSpec((1, d), lambda i: (0, 0)),
                  pl.BlockSpec((d, width), lambda i: (0, 0))],
        out_specs=pl.BlockSpec((row_tile, width), lambda i: (i, 0)),
        out_shape=jax.ShapeDtypeStruct((n, width), out_dtype),
        compiler_params=_params(1),
        name="rms_matmul",
    )(x, g.reshape(1, d), w_bf16)


def _matmul_residual_kernel(*refs, n_in):
    r_ref, o_ref = refs[2 * n_in], refs[2 * n_in + 1]
    acc = r_ref[...]
    for a_ref, w_ref in zip(refs[:n_in], refs[n_in:2 * n_in]):
        acc = acc + jnp.dot(a_ref[...].astype(_BF16), w_ref[...], preferred_element_type=_F32)
    o_ref[...] = acc


def _matmul_residual(a_list, w_list, resid, row_tile=ROW_TILE):
    n, d = resid.shape
    n_in = len(a_list)
    assert n % row_tile == 0
    return pl.pallas_call(
        functools.partial(_matmul_residual_kernel, n_in=n_in),
        grid=(n // row_tile,),
        in_specs=[pl.BlockSpec((row_tile, a.shape[1]), lambda i: (i, 0)) for a in a_list]
                 + [pl.BlockSpec(w.shape, lambda i: (0, 0)) for w in w_list]
                 + [pl.BlockSpec((row_tile, d), lambda i: (i, 0))],
        out_specs=pl.BlockSpec((row_tile, d), lambda i: (i, 0)),
        out_shape=jax.ShapeDtypeStruct((n, d), _F32),
        input_output_aliases={2 * n_in: 0},
        compiler_params=_params(1),
        name="matmul_residual",
    )(*a_list, *w_list, resid)


def _ffn_kernel(x_ref, g_ref, wg_ref, wu_ref, wo_ref, o_ref, *, chunk):
    x = x_ref[...]
    xn = _rms_rows(x, g_ref[...]).astype(_BF16)
    o_ref[...] = x
    hidden = wg_ref.shape[1]
    for c0 in range(0, hidden, chunk):
        gate = jnp.dot(xn, wg_ref[:, c0:c0 + chunk], preferred_element_type=_F32)
        up = jnp.dot(xn, wu_ref[:, c0:c0 + chunk], preferred_element_type=_F32)
        h = (gate * jax.nn.sigmoid(gate) * up).astype(_BF16)
        o_ref[...] += jnp.dot(h, wo_ref[c0:c0 + chunk, :], preferred_element_type=_F32)


def _ffn(x, g, wg, wu, wo, row_tile=ROW_TILE):
    n, d = x.shape
    hidden = wg.shape[1]
    chunk = 256
    assert n % row_tile == 0 and hidden % chunk == 0
    const = lambda i: (0, 0)
    return pl.pallas_call(
        functools.partial(_ffn_kernel, chunk=chunk),
        grid=(n // row_tile,),
        in_specs=[pl.BlockSpec((row_tile, d), lambda i: (i, 0)),
                  pl.BlockSpec((1, d), const),
                  pl.BlockSpec((d, hidden), const, pipeline_mode=pl.Buffered(1)),
                  pl.BlockSpec((d, hidden), const, pipeline_mode=pl.Buffered(1)),
                  pl.BlockSpec((hidden, d), const, pipeline_mode=pl.Buffered(1))],
        out_specs=pl.BlockSpec((row_tile, d), lambda i: (i, 0)),
        out_shape=jax.ShapeDtypeStruct((n, d), _F32),
        input_output_aliases={0: 0},
        compiler_params=_params(1),
        name="ffn",
    )(x, g.reshape(1, d), wg, wu, wo)


def _mem_attn_kernel(q_ref, k_ref, v_ref, o_ref):
    scale = MEM_DH ** -0.5
    for h in range(MEM_HEADS):
        cols = slice(h * MEM_DH, (h + 1) * MEM_DH)
        q = q_ref[:, cols].astype(_BF16)
        k = k_ref[:, cols].astype(_BF16)
        v = v_ref[:, cols].astype(_BF16)
        s = lax.dot_general(q, k, (((1,), (1,)), ((), ())), preferred_element_type=_F32) * scale
        p = jnp.exp(s - jnp.max(s, axis=-1, keepdims=True))
        inv = 1.0 / jnp.sum(p, axis=-1, keepdims=True)
        o = jnp.dot(p.astype(_BF16), v, preferred_element_type=_F32) * inv
        o_ref[:, cols] = o.astype(o_ref.dtype)


def _mem_attn(q, k, v, *, n_batch, q_len, q_row0, q_tile, k_col_block, v_col_block):
    assert q_len % q_tile == 0 and q_row0 % q_tile == 0
    tiles = q_len // q_tile
    t0 = q_row0 // q_tile
    width = MEM_HEADS * MEM_DH
    return pl.pallas_call(
        _mem_attn_kernel,
        grid=(n_batch, tiles),
        in_specs=[pl.BlockSpec((q_tile, width), lambda b, i: (t0 + b * tiles + i, 0)),
                  pl.BlockSpec((MEM_TOKENS, width), lambda b, i: (b, k_col_block)),
                  pl.BlockSpec((MEM_TOKENS, width), lambda b, i: (b, v_col_block))],
        out_specs=pl.BlockSpec((q_tile, width), lambda b, i: (b * tiles + i, 0)),
        out_shape=jax.ShapeDtypeStruct((n_batch * q_len, width), _BF16 if q_tile % 16 == 0 else _F32),
        compiler_params=_params(2),
        name="mem_attn",
    )(q, k, v)


def _rms_kernel(x_ref, g_ref, o_ref):
    o_ref[...] = _rms_rows(x_ref[...], g_ref[...])


def _rms(x, g, row_tile=ROW_TILE):
    n, d = x.shape
    return pl.pallas_call(
        _rms_kernel,
        grid=(n // row_tile,),
        in_specs=[pl.BlockSpec((row_tile, d), lambda i: (i, 0)),
                  pl.BlockSpec((1, d), lambda i: (0, 0))],
        out_specs=pl.BlockSpec((row_tile, d), lambda i: (i, 0)),
        out_shape=jax.ShapeDtypeStruct((n, d), _F32),
        compiler_params=_params(1),
        name="final_rms",
    )(x, g.reshape(1, d))


def _rms_matmul_split_kernel(x_ref, g_ref, w_ref, *o_refs, col_chunk):
    xn = _rms_rows(x_ref[...], g_ref[...]).astype(_BF16)
    c0 = 0
    for o_ref in o_refs:
        width = o_ref.shape[1]
        for s0 in range(0, width, col_chunk):
            s1 = min(width, s0 + col_chunk)
            o_ref[:, s0:s1] = jnp.dot(xn, w_ref[:, c0 + s0:c0 + s1], preferred_element_type=_F32)
        c0 += width


def _rms_matmul_split(x, g, w_bf16, widths, row_tile=ROW_TILE):
    n, d = x.shape
    assert n % row_tile == 0 and sum(widths) == w_bf16.shape[1] and all(w % LANES == 0 for w in widths)
    return pl.pallas_call(
        functools.partial(_rms_matmul_split_kernel, col_chunk=512),
        grid=(n // row_tile,),
        in_specs=[pl.BlockSpec((row_tile, d), lambda i: (i, 0)),
                  pl.BlockSpec((1, d), lambda i: (0, 0)),
                  pl.BlockSpec((d, w_bf16.shape[1]), lambda i: (0, 0))],
        out_specs=[pl.BlockSpec((row_tile, w), lambda i: (i, 0)) for w in widths],
        out_shape=[jax.ShapeDtypeStruct((n, w), _F32) for w in widths],
        compiler_params=_params(1),
        name="rms_matmul_split",
    )(x, g.reshape(1, d), w_bf16)


def _rope_apply(x, cos, sin_up, sin_dn):
    width = x.shape[1]
    reps = width // cos.shape[1]
    if reps > 1:
        cos, sin_up, sin_dn = [jnp.concatenate([t] * reps, axis=1) for t in (cos, sin_up, sin_dn)]
    return x * cos + pltpu.roll(x, width - ROT_DIM // 2, 1) * sin_up + pltpu.roll(x, ROT_DIM // 2, 1) * sin_dn


def _odd_proj_kernel(x_ref, g_ref, w_ref, gb_ref, cos_ref, sup_ref, sdn_ref,
                     q_ref, qr_ref, kc_ref, vc_ref, ks_ref, vs_ref, kw_ref, vw_ref, gate_ref):
    xn = _rms_rows(x_ref[...], g_ref[...]).astype(_BF16)
    cos, sup, sdn = cos_ref[...], sup_ref[...], sdn_ref[...]
    kvw = C_KV_WIDTH
    for h0 in range(0, C_WIDTH, 512):
        q = jnp.dot(xn, w_ref[:, h0:h0 + 512], preferred_element_type=_F32)
        q_ref[:, h0:h0 + 512] = q
        qr_ref[:, h0:h0 + 512] = _rope_apply(q, cos, sup, sdn)
    c0 = C_WIDTH
    for o_ref, rot in ((kc_ref, False), (vc_ref, False), (ks_ref, True), (vs_ref, False), (kw_ref, True), (vw_ref, False)):
        y = jnp.dot(xn, w_ref[:, c0:c0 + kvw], preferred_element_type=_F32)
        o_ref[...] = _rope_apply(y, cos, sup, sdn) if rot else y
        c0 += kvw
    gl = jnp.dot(xn, w_ref[:, c0:c0 + LANES], preferred_element_type=_F32)
    gate_ref[...] = jax.nn.sigmoid(gl + gb_ref[...])


def _rope_tables(pos):
    half = ROT_DIM // 2
    inv_freq = ROPE_THETA ** (-jnp.arange(half, dtype=_F32) / half)
    ang = pos.astype(_F32)[:, None] * inv_freq[None, :]
    cos, sin = jnp.cos(ang), jnp.sin(ang)
    rest = NSA_DH - ROT_DIM
    ones = jnp.ones((pos.shape[0], rest), _F32)
    zeros = jnp.zeros((pos.shape[0], rest), _F32)
    z8 = jnp.zeros_like(sin)
    head_cos = jnp.concatenate([cos, cos, ones], axis=1)
    head_up = jnp.concatenate([-sin, z8, zeros], axis=1)
    head_dn = jnp.concatenate([z8, sin, zeros], axis=1)
    return [jnp.tile(t, (1, NSA_KV_HEADS)) for t in (head_cos, head_up, head_dn)]


def _odd_proj(x, g, w_in, gate_bias, pos_p, pos_s, n_p, row_tile=ROW_TILE):
    n, d = x.shape
    L, SL = pos_p.shape[0], pos_s.shape[0]
    assert L % row_tile == 0 and n_p % row_tile == 0 and (n - n_p) % row_tile == 0 and row_tile % SL == 0
    head = np.arange(NSA_HEADS)
    gl_cols = np.concatenate([C_WIDTH + 6 * C_KV_WIDTH + head * 3 + br for br in range(3)])
    w = jnp.concatenate([w_in[:, :C_WIDTH + 6 * C_KV_WIDTH], w_in[:, gl_cols],
                         jnp.zeros((d, LANES - 3 * NSA_HEADS), w_in.dtype)], axis=1).astype(_BF16)
    gb = jnp.concatenate([gate_bias[gl_cols - (C_WIDTH + 6 * C_KV_WIDTH)], jnp.zeros((LANES - 3 * NSA_HEADS,), _F32)]).reshape(1, LANES)
    tables = _rope_tables(jnp.concatenate([pos_p, jnp.tile(pos_s, row_tile // SL)]))
    tiles_per_seq = L // row_tile
    n_p_tiles = n_p // row_tile
    tab_map = lambda i: (jnp.where(i < n_p_tiles, i % tiles_per_seq, tiles_per_seq), 0)
    row_map = lambda i: (i, 0)
    const = lambda i: (0, 0)
    widths = [C_WIDTH, C_WIDTH] + [C_KV_WIDTH] * 6 + [LANES]
    return pl.pallas_call(
        _odd_proj_kernel,
        grid=(n // row_tile,),
        in_specs=[pl.BlockSpec((row_tile, d), row_map),
                  pl.BlockSpec((1, d), const),
                  pl.BlockSpec((d, w.shape[1]), const),
                  pl.BlockSpec((1, LANES), const)]
                 + [pl.BlockSpec((row_tile, C_KV_WIDTH), tab_map)] * 3,
        out_specs=[pl.BlockSpec((row_tile, wd), row_map) for wd in widths],
        out_shape=[jax.ShapeDtypeStruct((n, wd), _F32) for wd in widths],
        compiler_params=_params(1),
        name="odd_proj",
    )(x, g.reshape(1, d), w, gb, *tables)


def _online_softmax_step(s, mask, v_bf16, m, l, acc):
    s = jnp.where(mask, s, NEG_BIG)
    m_new = jnp.maximum(m, jnp.max(s, axis=-1, keepdims=True))
    alpha = jnp.exp(m - m_new)
    p = jnp.where(mask, jnp.exp(s - m_new), 0.0)
    l = alpha * l + jnp.sum(p, axis=-1, keepdims=True)
    acc = alpha * acc + jnp.dot(p.astype(_BF16), v_bf16, preferred_element_type=_F32)
    return m_new, l, acc


def _fox_prompt_kernel(q_ref, k_ref, v_ref, c_ref, o_ref, *, tile):
    i = pl.program_id(1)
    scale = FOX_DH ** -0.5
    row = lax.broadcasted_iota(jnp.int32, (tile, tile), 0)
    col = lax.broadcasted_iota(jnp.int32, (tile, tile), 1)
    for h in range(FOX_HEADS):
        cols = slice(h * FOX_DH, (h + 1) * FOX_DH)
        q = (q_ref[:, cols] * scale).astype(_BF16)

        def body(j, carry, cols=cols, q=q, h=h):
            k0 = pl.multiple_of(j * tile, tile)
            k = k_ref[pl.ds(k0, tile), cols].astype(_BF16)
            v = v_ref[pl.ds(k0, tile), cols].astype(_BF16)
            s = lax.dot_general(q, k, (((1,), (1,)), ((), ())), preferred_element_type=_F32)
            s = s - c_ref[0, h:h + 1, pl.ds(k0, tile)]
            mask = (col + k0) <= (row + i * tile)
            return _online_softmax_step(s, mask, v, *carry)

        init = (jnp.full((tile, 1), NEG_BIG, _F32), jnp.zeros((tile, 1), _F32), jnp.zeros((tile, FOX_DH), _F32))
        m, l, acc = lax.fori_loop(0, i + 1, body, init)
        o_ref[:, cols] = acc / jnp.maximum(l, 1e-30)


def _fox_prompt(q, k, v, c_rows, *, n_batch, seq, tile):
    assert seq % tile == 0
    tiles = seq // tile
    return pl.pallas_call(
        functools.partial(_fox_prompt_kernel, tile=tile),
        grid=(n_batch, tiles),
        in_specs=[pl.BlockSpec((tile, B_WIDTH), lambda b, i: (b * tiles + i, 0)),
                  pl.BlockSpec((seq, B_WIDTH), lambda b, i: (b, 0)),
                  pl.BlockSpec((seq, B_WIDTH), lambda b, i: (b, 0)),
                  pl.BlockSpec((1, FOX_HEADS, seq), lambda b, i: (b, 0, 0))],
        out_specs=pl.BlockSpec((tile, B_WIDTH), lambda b, i: (b * tiles + i, 0)),
        out_shape=jax.ShapeDtypeStruct((n_batch * seq, B_WIDTH), _F32),
        compiler_params=_params(2),
        name="fox_prompt",
    )(q, k, v, c_rows)


MASKED = -2e30


def _softmax_step(s, v_bf16, m, l, acc):
    m_new = jnp.maximum(m, jnp.max(s, axis=-1, keepdims=True))
    alpha = jnp.exp(m - m_new)
    p = jnp.exp(s - m_new)
    l = alpha * l + jnp.sum(p, axis=-1, keepdims=True)
    acc = alpha * acc + jnp.dot(p.astype(_BF16), v_bf16, preferred_element_type=_F32)
    return m_new, l, acc


def _split3_dot(x, w_bf16):
    hi = x.astype(_BF16)
    r1 = x - hi.astype(_F32)
    mid = r1.astype(_BF16)
    lo = (r1 - mid.astype(_F32)).astype(_BF16)
    dot = lambda a: jnp.dot(a, w_bf16, preferred_element_type=_F32)
    return dot(hi) + dot(mid) + dot(lo)


def _compress_into(x2_ref, n_sub, wpair_ref, w1_ref, w2_ref, pos_ref, out_ref):
    pair_rows = []
    for p in range(NSA_KV_HEADS // 2):
        pieces = [x2_ref[pl.ds(2 * r + p, n_sub, stride=2 * CMP_STRIDE), :] for r in range(CMP_STRIDE)]
        pair_rows.append(jnp.concatenate(pieces, axis=1))
    x = jnp.concatenate(pair_rows, axis=0).astype(_BF16)
    both = jnp.dot(x, wpair_ref[...], preferred_element_type=_F32)
    bias = jnp.dot(pos_ref[...].astype(_BF16), w1_ref[...], preferred_element_type=_F32)
    hid_w = 2 * NSA_DH
    for kv in range(NSA_KV_HEADS):
        p, s = divmod(kv, 2)
        rows = slice(p * n_sub, (p + 1) * n_sub)
        first = both[rows, s * hid_w:(s + 1) * hid_w]
        second = both[rows, (2 + s) * hid_w:(3 + s) * hid_w]
        pre = first + pltpu.roll(second, n_sub - 1, 0) + bias
        hid = (pre * jax.nn.sigmoid(pre)).astype(_BF16)
        out_ref[0:n_sub, kv * NSA_DH:(kv + 1) * NSA_DH] = jnp.dot(hid, w2_ref[...], preferred_element_type=_F32)


def _cmp_branch(q_ref, gate_ref, kcb_ref, vcb_ref, c2s_ref, o_ref, sel_ref, q_pos, n_sel):
    rows = q_ref.shape[0]
    nb = kcb_ref.shape[0]
    scale = NSA_DH ** -0.5
    blk_n = lax.broadcasted_iota(jnp.int32, (rows, nb), 1)
    cbias = jnp.where(blk_n * CMP_STRIDE + (CMP_LEN - 1) <= q_pos, 0.0, MASKED)
    lane = lax.broadcasted_iota(jnp.int32, (rows, LANES), 1)
    cur = q_pos // SEL_BLOCK
    causal = (lane <= cur) & (lane < n_sel)
    forced = (lane == 0) | (lane == cur) | (lane == cur - 1)
    for kv in range(NSA_KV_HEADS):
        kcols = slice(kv * NSA_DH, (kv + 1) * NSA_DH)
        kcb = kcb_ref[:, kcols].astype(_BF16)
        vcb = vcb_ref[:, kcols].astype(_BF16)
        p_sum = jnp.zeros((rows, nb), _F32)
        for g in range(NSA_GROUP):
            h = kv * NSA_GROUP + g
            hcols = slice(h * NSA_DH, (h + 1) * NSA_DH)
            q = (q_ref[:, hcols] * scale).astype(_BF16)
            s = lax.dot_general(q, kcb, (((1,), (1,)), ((), ())), preferred_element_type=_F32) + cbias
            m = jnp.maximum(jnp.max(s, axis=-1, keepdims=True), NEG_BIG)
            p = jnp.exp(s - m)
            p = p / jnp.maximum(jnp.sum(p, axis=-1, keepdims=True), 1e-30)
            p_sum = p_sum + p
            o = jnp.dot(p.astype(_BF16), vcb, preferred_element_type=_F32)
            o_ref[:, hcols] = o * gate_ref[:, h:h + 1]
        imp = _split3_dot(p_sum, c2s_ref[...])
        imp = jnp.where(causal, jnp.where(forced, -NEG_BIG, imp), NEG_BIG)
        ahead = jnp.zeros((rows, LANES), _F32)
        for j in range(n_sel):
            cj = imp[:, j:j + 1]
            ahead = ahead + jnp.where((cj > imp) | ((cj == imp) & (lane > j)), 1.0, 0.0)
        sel = jnp.where((ahead < float(min(SEL_TOP_N, n_sel))) & (imp > 0.5 * NEG_BIG), 1.0, 0.0)
        sel_ref[:, kv * LANES:(kv + 1) * LANES] = sel


def _cmp_to_sel(nb):
    c_start = np.arange(nb)[:, None] * CMP_STRIDE
    s_start = np.arange(LANES)[None, :] * SEL_BLOCK
    shared = np.minimum(c_start + CMP_LEN, s_start + SEL_BLOCK) - np.maximum(c_start, s_start)
    return jnp.asarray(np.clip(shared, 0, None).astype(np.float32) / CMP_LEN, _BF16)


def _sel_expand(n_keys):
    return jnp.asarray((np.arange(n_keys)[None, :] // SEL_BLOCK == np.arange(LANES)[:, None]).astype(np.float32), _BF16)


def _nsa_cmp_prompt_kernel(q_ref, gate_ref, kc_ref, vc_ref, wpk_ref, w1k_ref, w2k_ref, posk_ref,
                           wpv_ref, w1v_ref, w2v_ref, posv_ref, c2s_ref, o_ref, sel_ref, kcb_ref, vcb_ref,
                           *, tile, n_sub, n_sel):
    i = pl.program_id(1)

    @pl.when(i == 0)
    def _():
        kcb_ref[...] = jnp.zeros_like(kcb_ref)
        vcb_ref[...] = jnp.zeros_like(vcb_ref)
        _compress_into(kc_ref, n_sub, wpk_ref, w1k_ref, w2k_ref, posk_ref, kcb_ref)
        _compress_into(vc_ref, n_sub, wpv_ref, w1v_ref, w2v_ref, posv_ref, vcb_ref)

    q_pos = i * tile + lax.broadcasted_iota(jnp.int32, (tile, 1), 0)
    _cmp_branch(q_ref, gate_ref, kcb_ref, vcb_ref, c2s_ref, o_ref, sel_ref, q_pos, n_sel)


def _cmp_weights(cmp_pos, cmp_w1, cmp_w2):
    out = []
    eye2 = jnp.eye(2, dtype=_F32)
    for idx in range(2):
        w1r = cmp_w1[idx].reshape(CMP_LEN, NSA_DH, 2 * NSA_DH)
        halves = [jnp.einsum('rdj,st->rsdtj', w1r[h * CMP_STRIDE:(h + 1) * CMP_STRIDE], eye2).reshape(
            CMP_STRIDE * 2 * NSA_DH, 4 * NSA_DH) for h in range(2)]
        out += [jnp.concatenate(halves, axis=1).astype(_BF16), cmp_w1[idx].astype(_BF16), cmp_w2[idx].astype(_BF16),
                cmp_pos[idx].reshape(1, CMP_LEN * NSA_DH)]
    return out


def _cmp_weight_specs(const):
    return [pl.BlockSpec((CMP_STRIDE * 2 * NSA_DH, 8 * NSA_DH), const),
            pl.BlockSpec((CMP_LEN * NSA_DH, 2 * NSA_DH), const),
            pl.BlockSpec((2 * NSA_DH, NSA_DH), const),
            pl.BlockSpec((1, CMP_LEN * NSA_DH), const)] * 2


def _nsa_cmp_prompt(q, gates, kc, vc, cmp_pos, cmp_w1, cmp_w2, *, n_batch, seq, tile):
    assert seq % tile == 0 and seq % SEL_BLOCK == 0
    tiles = seq // tile
    n_sub = seq // CMP_STRIDE
    nb = -(-n_sub // LANES) * LANES
    n_sel = seq // SEL_BLOCK
    row_map = lambda b, i: (b * tiles + i, 0)
    seq_map = lambda b, i: (b, 0)
    const = lambda b, i: (0, 0)
    return pl.pallas_call(
        functools.partial(_nsa_cmp_prompt_kernel, tile=tile, n_sub=n_sub, n_sel=n_sel),
        grid=(n_batch, tiles),
        in_specs=[pl.BlockSpec((tile, C_WIDTH), row_map),
                  pl.BlockSpec((tile, LANES), row_map),
                  pl.BlockSpec((2 * seq, LANES), seq_map),
                  pl.BlockSpec((2 * seq, LANES), seq_map)] + _cmp_weight_specs(const)
                 + [pl.BlockSpec((nb, LANES), const)],
        out_specs=[pl.BlockSpec((tile, C_WIDTH), row_map),
                   pl.BlockSpec((tile, NSA_KV_HEADS * LANES), row_map)],
        out_shape=[jax.ShapeDtypeStruct((n_batch * seq, C_WIDTH), _F32),
                   jax.ShapeDtypeStruct((n_batch * seq, NSA_KV_HEADS * LANES), _F32)],
        scratch_shapes=[pltpu.VMEM((nb, C_KV_WIDTH), _F32), pltpu.VMEM((nb, C_KV_WIDTH), _F32)],
        compiler_params=_params(2),
        name="nsa_cmp_prompt",
    )(q, gates, kc.reshape(-1, LANES), vc.reshape(-1, LANES), *_cmp_weights(cmp_pos, cmp_w1, cmp_w2), _cmp_to_sel(nb))


def _nsa_selwin_prompt_kernel(qr_ref, gate_ref, sel_ref, ocmp_ref, ks_ref, vs_ref, kw_ref, vw_ref, exp_ref,
                              o_ref, selbias_ref, *, tile):
    i = pl.program_id(1)
    scale = NSA_DH ** -0.5
    q0 = i * tile
    row = lax.broadcasted_iota(jnp.int32, (tile, tile), 0) + q0
    col = lax.broadcasted_iota(jnp.int32, (tile, tile), 1)
    rows4 = NSA_GROUP * tile
    init = (jnp.full((rows4, 1), NEG_BIG, _F32), jnp.zeros((rows4, 1), _F32), jnp.zeros((rows4, NSA_DH), _F32))
    first_win = jnp.maximum(i - (WINDOW + tile - 1) // tile, 0)
    for kv in range(NSA_KV_HEADS):
        kcols = slice(kv * NSA_DH, (kv + 1) * NSA_DH)
        picked = jnp.dot(sel_ref[:, kv * LANES:(kv + 1) * LANES].astype(_BF16), exp_ref[...], preferred_element_type=_F32)
        selbias_ref[...] = (1.0 - picked) * MASKED
        q = jnp.concatenate(
            [(qr_ref[:, (kv * NSA_GROUP + g) * NSA_DH:(kv * NSA_GROUP + g + 1) * NSA_DH] * scale).astype(_BF16)
             for g in range(NSA_GROUP)], axis=0)

        def sel_body(j, carry, q=q, kcols=kcols):
            k0 = pl.multiple_of(j * tile, tile)
            k = ks_ref[pl.ds(k0, tile), kcols].astype(_BF16)
            v = vs_ref[pl.ds(k0, tile), kcols].astype(_BF16)
            bias = selbias_ref[:, pl.ds(k0, tile)] + jnp.where(col + k0 <= row, 0.0, MASKED)
            s = lax.dot_general(q, k, (((1,), (1,)), ((), ())), preferred_element_type=_F32)
            return _softmax_step(s + jnp.concatenate([bias] * NSA_GROUP, axis=0), v, *carry)

        def win_body(j, carry, q=q, kcols=kcols):
            k0 = pl.multiple_of(j * tile, tile)
            k = kw_ref[pl.ds(k0, tile), kcols].astype(_BF16)
            v = vw_ref[pl.ds(k0, tile), kcols].astype(_BF16)
            dist = row - (col + k0)
            bias = jnp.where((dist >= 0) & (dist < WINDOW), 0.0, MASKED)
            s = lax.dot_general(q, k, (((1,), (1,)), ((), ())), preferred_element_type=_F32)
            return _softmax_step(s + jnp.concatenate([bias] * NSA_GROUP, axis=0), v, *carry)

        _, l_s, acc_s = lax.fori_loop(0, i + 1, sel_body, init)
        _, l_w, acc_w = lax.fori_loop(first_win, i + 1, win_body, init)
        o_s = acc_s / jnp.maximum(l_s, 1e-30)
        o_w = acc_w / jnp.maximum(l_w, 1e-30)
        for g in range(NSA_GROUP):
            h = kv * NSA_GROUP + g
            hcols = slice(h * NSA_DH, (h + 1) * NSA_DH)
            rws = slice(g * tile, (g + 1) * tile)
            o_ref[:, hcols] = (ocmp_ref[:, hcols]
                               + o_s[rws] * gate_ref[:, NSA_HEADS + h:NSA_HEADS + h + 1]
                               + o_w[rws] * gate_ref[:, 2 * NSA_HEADS + h:2 * NSA_HEADS + h + 1])


def _nsa_selwin_prompt(qr, gates, sel, o_cmp, ks, vs, kw, vw, *, n_batch, seq, tile):
    assert seq % tile == 0
    tiles = seq // tile
    row_map = lambda b, i: (b * tiles + i, 0)
    seq_map = lambda b, i: (b, 0)
    return pl.pallas_call(
        functools.partial(_nsa_selwin_prompt_kernel, tile=tile),
        grid=(n_batch, tiles),
        in_specs=[pl.BlockSpec((tile, C_WIDTH), row_map),
                  pl.BlockSpec((tile, LANES), row_map),
                  pl.BlockSpec((tile, NSA_KV_HEADS * LANES), row_map),
                  pl.BlockSpec((tile, C_WIDTH), row_map)]
                 + [pl.BlockSpec((seq, C_KV_WIDTH), seq_map)] * 4
                 + [pl.BlockSpec((LANES, seq), lambda b, i: (0, 0))],
        out_specs=pl.BlockSpec((tile, C_WIDTH), row_map),
        out_shape=jax.ShapeDtypeStruct((n_batch * seq, C_WIDTH), _F32),
        scratch_shapes=[pltpu.VMEM((tile, seq), _F32)],
        compiler_params=_params(2),
        name="nsa_selwin_prompt",
    )(qr, gates, sel, o_cmp, ks, vs, kw, vw, _sel_expand(seq))


def _page_specs(block, n_pages):
    return [pl.BlockSpec(block, functools.partial(lambda b, pt, p: (pt[b, p],) + (0,) * (len(block) - 1), p=p))
            for p in range(n_pages)]


def _pad_rows_to(x, rows):
    return jnp.concatenate([x, jnp.zeros((rows - x.shape[0], x.shape[1]), x.dtype)], axis=0)


def _fox_sample_kernel(pt_ref, q_ref, kn_ref, vn_ref, cn_ref, *refs, n_pages, page):
    del pt_ref
    k_refs, v_refs, c_refs = refs[:n_pages], refs[n_pages:2 * n_pages], refs[2 * n_pages:3 * n_pages]
    o_ref, kall_ref, vall_ref, call_ref = refs[3 * n_pages:]
    sl = q_ref.shape[0]
    rows = sl * FOX_HEADS
    carry = jnp.zeros((FOX_HEADS, 1), _F32)
    for p in range(n_pages):
        kall_ref[p * page:(p + 1) * page, :] = k_refs[p][0].astype(_BF16)
        vall_ref[p * page:(p + 1) * page, :] = v_refs[p][0].astype(_BF16)
        c_page = c_refs[p][0] + carry
        call_ref[:, p * page:(p + 1) * page] = c_page
        carry = c_page[:, page - 1:page]
    head_of_lane = lax.broadcasted_iota(jnp.int32, (FOX_HEADS, B_WIDTH), 1) // FOX_DH
    head_mask = jnp.where(head_of_lane == lax.broadcasted_iota(jnp.int32, (FOX_HEADS, B_WIDTH), 0), 1.0, 0.0)
    q = q_ref[...] * (FOX_DH ** -0.5)
    q_bd = jnp.concatenate([jnp.broadcast_to(q[i:i + 1, :], (FOX_HEADS, B_WIDTH)) * head_mask for i in range(sl)],
                           axis=0).astype(_BF16)
    s_past = lax.dot_general(q_bd, kall_ref[...], (((1,), (1,)), ((), ())), preferred_element_type=_F32)
    s_past = s_past - jnp.concatenate([call_ref[...]] * sl, axis=0)
    k_new = _pad_rows_to(kn_ref[...], LANES).astype(_BF16)
    v_new = _pad_rows_to(vn_ref[...], LANES).astype(_BF16)
    s_new = lax.dot_general(q_bd, k_new, (((1,), (1,)), ((), ())), preferred_element_type=_F32)
    lane = lax.broadcasted_iota(jnp.int32, (rows, LANES), 1)
    q_idx = lax.broadcasted_iota(jnp.int32, (rows, LANES), 0) // FOX_HEADS
    s_new = s_new - jnp.concatenate([cn_ref[0] + carry] * sl, axis=0) + jnp.where(lane <= q_idx, 0.0, MASKED)
    m = jnp.maximum(jnp.max(s_past, axis=-1, keepdims=True), jnp.max(s_new, axis=-1, keepdims=True))
    p_past = jnp.exp(s_past - m)
    p_new = jnp.exp(s_new - m)
    l = jnp.sum(p_past, axis=-1, keepdims=True) + jnp.sum(p_new, axis=-1, keepdims=True)
    acc = (jnp.dot(p_past.astype(_BF16), vall_ref[...], preferred_element_type=_F32)
           + jnp.dot(p_new.astype(_BF16), v_new, preferred_element_type=_F32)) / l
    o_ref[...] = jnp.concatenate(
        [jnp.sum(acc[i * FOX_HEADS:(i + 1) * FOX_HEADS] * head_mask, axis=0, keepdims=True) for i in range(sl)], axis=0)


def _fox_sample(q, k, v, c_new, k_pool, v_pool, c_pool, page_table, *, row0, sl):
    nb, n_pages = page_table.shape
    page = k_pool.shape[1]
    assert row0 % sl == 0 and sl <= LANES
    r0 = row0 // sl
    row_spec = pl.BlockSpec((sl, B_WIDTH), lambda b, pt: (r0 + b, 0))
    past = n_pages * page
    return pl.pallas_call(
        functools.partial(_fox_sample_kernel, n_pages=n_pages, page=page),
        grid_spec=pltpu.PrefetchScalarGridSpec(
            num_scalar_prefetch=1,
            grid=(nb,),
            in_specs=[row_spec, row_spec, row_spec, pl.BlockSpec((1, FOX_HEADS, LANES), lambda b, pt: (b, 0, 0))]
                     + _page_specs((1, page, B_WIDTH), n_pages) * 2 + _page_specs((1, FOX_HEADS, page), n_pages),
            out_specs=pl.BlockSpec((sl, B_WIDTH), lambda b, pt: (b, 0)),
            scratch_shapes=[pltpu.VMEM((past, B_WIDTH), _BF16), pltpu.VMEM((past, B_WIDTH), _BF16),
                            pltpu.VMEM((FOX_HEADS, past), _F32)]),
        out_shape=jax.ShapeDtypeStruct((nb * sl, B_WIDTH), _F32),
        compiler_params=_params(1),
        name="fox_sample",
    )(page_table, q, k, v, c_new, *([k_pool.reshape(-1, page, B_WIDTH)] * n_pages),
      *([v_pool.reshape(-1, page, B_WIDTH)] * n_pages), *([c_pool] * n_pages))


def _nsa_sample_kernel(pt_ref, q_ref, qr_ref, gate_ref, ksn_ref, vsn_ref, kwn_ref, vwn_ref, kwc_ref, vwc_ref,
                       wpk_ref, w1k_ref, w2k_ref, posk_ref, wpv_ref, w1v_ref, w2v_ref, posv_ref, c2s_ref, exp_ref,
                       *refs, n_pages, page, past_len, n_sel):
    del pt_ref
    ck_refs, cv_refs = refs[:n_pages], refs[n_pages:2 * n_pages]
    sk_refs, sv_refs = refs[2 * n_pages:3 * n_pages], refs[3 * n_pages:4 * n_pages]
    o_ref, xk_ref, xv_ref, kcb_ref, vcb_ref, ksel_ref, vsel_ref, ocmp_ref, sel_ref = refs[4 * n_pages:]
    sl = q_ref.shape[0]
    scale = NSA_DH ** -0.5
    for p in range(n_pages):
        xk_ref[2 * p * page:2 * (p + 1) * page, :] = ck_refs[p][0]
        xv_ref[2 * p * page:2 * (p + 1) * page, :] = cv_refs[p][0]
        ksel_ref[p * page:(p + 1) * page, :] = sk_refs[p][0].astype(_BF16)
        vsel_ref[p * page:(p + 1) * page, :] = sv_refs[p][0].astype(_BF16)
    n_sub = past_len // CMP_STRIDE
    _compress_into(xk_ref, n_sub, wpk_ref, w1k_ref, w2k_ref, posk_ref, kcb_ref)
    _compress_into(xv_ref, n_sub, wpv_ref, w1v_ref, w2v_ref, posv_ref, vcb_ref)
    q_pos = past_len + lax.broadcasted_iota(jnp.int32, (sl, 1), 0)
    _cmp_branch(q_ref, gate_ref, kcb_ref, vcb_ref, c2s_ref, ocmp_ref, sel_ref, q_pos, n_sel)

    lane = lax.broadcasted_iota(jnp.int32, (sl, LANES), 1)
    q_idx = lax.broadcasted_iota(jnp.int32, (sl, LANES), 0)
    new_ok = lane <= q_idx
    cache_rows = kwc_ref.shape[0]
    c_lane = lax.broadcasted_iota(jnp.int32, (sl, cache_rows), 1)
    c_qidx = lax.broadcasted_iota(jnp.int32, (sl, cache_rows), 0)
    c_dist = (past_len + c_qidx) - (past_len - cache_rows + c_lane)
    win_cache_bias = jnp.where((c_dist >= 0) & (c_dist < WINDOW), 0.0, MASKED)
    win_new_bias = jnp.where(new_ok, 0.0, MASKED)
    tile4 = lambda b: jnp.concatenate([b] * NSA_GROUP, axis=0)
    nt = (((1,), (1,)), ((), ()))
    for kv in range(NSA_KV_HEADS):
        kcols = slice(kv * NSA_DH, (kv + 1) * NSA_DH)
        q = jnp.concatenate(
            [(qr_ref[:, (kv * NSA_GROUP + g) * NSA_DH:(kv * NSA_GROUP + g + 1) * NSA_DH] * scale).astype(_BF16)
             for g in range(NSA_GROUP)], axis=0)
        sel = sel_ref[:, kv * LANES:(kv + 1) * LANES]
        picked = jnp.dot(sel.astype(_BF16), exp_ref[...], preferred_element_type=_F32)
        new_blk = past_len // SEL_BLOCK
        sel_new_bias = jnp.where(new_ok & (sel[:, new_blk:new_blk + 1] > 0.5), 0.0, MASKED)
        s_a = lax.dot_general(q, ksel_ref[:, kcols], nt, preferred_element_type=_F32) + tile4((1.0 - picked) * MASKED)
        k_new = _pad_rows_to(ksn_ref[:, kcols], LANES).astype(_BF16)
        s_b = lax.dot_general(q, k_new, nt, preferred_element_type=_F32) + tile4(sel_new_bias)
        m = jnp.maximum(jnp.maximum(jnp.max(s_a, axis=-1, keepdims=True), jnp.max(s_b, axis=-1, keepdims=True)), NEG_BIG)
        p_a, p_b = jnp.exp(s_a - m), jnp.exp(s_b - m)
        l = jnp.sum(p_a, axis=-1, keepdims=True) + jnp.sum(p_b, axis=-1, keepdims=True)
        o_sel = (jnp.dot(p_a.astype(_BF16), vsel_ref[:, kcols], preferred_element_type=_F32)
                 + jnp.dot(p_b.astype(_BF16), _pad_rows_to(vsn_ref[:, kcols], LANES).astype(_BF16),
                           preferred_element_type=_F32)) / jnp.maximum(l, 1e-30)
        s_a = lax.dot_general(q, kwc_ref[:, kcols].astype(_BF16), nt, preferred_element_type=_F32) + tile4(win_cache_bias)
        k_new = _pad_rows_to(kwn_ref[:, kcols], LANES).astype(_BF16)
        s_b = lax.dot_general(q, k_new, nt, preferred_element_type=_F32) + tile4(win_new_bias)
        m = jnp.maximum(jnp.maximum(jnp.max(s_a, axis=-1, keepdims=True), jnp.max(s_b, axis=-1, keepdims=True)), NEG_BIG)
        p_a, p_b = jnp.exp(s_a - m), jnp.exp(s_b - m)
        l = jnp.sum(p_a, axis=-1, keepdims=True) + jnp.sum(p_b, axis=-1, keepdims=True)
        o_win = (jnp.dot(p_a.astype(_BF16), vwc_ref[:, kcols].astype(_BF16), preferred_element_type=_F32)
                 + jnp.dot(p_b.astype(_BF16), _pad_rows_to(vwn_ref[:, kcols], LANES).astype(_BF16),
                           preferred_element_type=_F32)) / jnp.maximum(l, 1e-30)
        for g in range(NSA_GROUP):
            h = kv * NSA_GROUP + g
            hcols = slice(h * NSA_DH, (h + 1) * NSA_DH)
            rws = slice(g * sl, (g + 1) * sl)
            o_ref[:, hcols] = (ocmp_ref[:, hcols]
                               + o_sel[rws] * gate_ref[:, NSA_HEADS + h:NSA_HEADS + h + 1]
                               + o_win[rws] * gate_ref[:, 2 * NSA_HEADS + h:2 * NSA_HEADS + h + 1])


def _nsa_sample(q, qr, gates, ks, vs, kw, vw, win_k, win_v, cmp_k_pool, cmp_v_pool, sel_k_pool, sel_v_pool,
                page_table, cmp_pos, cmp_w1, cmp_w2, *, row0, sl):
    nb, n_pages = page_table.shape
    page = cmp_k_pool.shape[1]
    past_len = n_pages * page
    cache_rows = win_k.shape[0] // nb
    assert row0 % sl == 0 and sl <= SEL_BLOCK and past_len % SEL_BLOCK == 0 and past_len // CMP_STRIDE == LANES
    assert past_len >= cache_rows
    n_sel = past_len // SEL_BLOCK + 1
    r0 = row0 // sl
    row = lambda w: pl.BlockSpec((sl, w), lambda b, pt: (r0 + b, 0))
    const = lambda b, pt: (0, 0)
    return pl.pallas_call(
        functools.partial(_nsa_sample_kernel, n_pages=n_pages, page=page, past_len=past_len, n_sel=n_sel),
        grid_spec=pltpu.PrefetchScalarGridSpec(
            num_scalar_prefetch=1,
            grid=(nb,),
            in_specs=[row(C_WIDTH), row(C_WIDTH), row(LANES)] + [row(C_KV_WIDTH)] * 4
                     + [pl.BlockSpec((cache_rows, C_KV_WIDTH), lambda b, pt: (b, 0))] * 2
                     + _cmp_weight_specs(const)
                     + [pl.BlockSpec((LANES, LANES), const), pl.BlockSpec((LANES, past_len), const)]
                     + _page_specs((1, 2 * page, LANES), n_pages) * 2
                     + _page_specs((1, page, C_KV_WIDTH), n_pages) * 2,
            out_specs=pl.BlockSpec((sl, C_WIDTH), lambda b, pt: (b, 0)),
            scratch_shapes=[pltpu.VMEM((2 * past_len, LANES), _F32), pltpu.VMEM((2 * past_len, LANES), _F32),
                            pltpu.VMEM((LANES, C_KV_WIDTH), _F32), pltpu.VMEM((LANES, C_KV_WIDTH), _F32),
                            pltpu.VMEM((past_len, C_KV_WIDTH), _BF16), pltpu.VMEM((past_len, C_KV_WIDTH), _BF16),
                            pltpu.VMEM((sl, C_WIDTH), _F32), pltpu.VMEM((sl, NSA_KV_HEADS * LANES), _F32)]),
        out_shape=jax.ShapeDtypeStruct((nb * sl, C_WIDTH), _F32),
        compiler_params=_params(1),
        name="nsa_sample",
    )(page_table, q, qr, gates, ks, vs, kw, vw, win_k, win_v, *_cmp_weights(cmp_pos, cmp_w1, cmp_w2),
      _cmp_to_sel(LANES), _sel_expand(past_len),
      *([cmp_k_pool.reshape(-1, 2 * page, LANES)] * n_pages), *([cmp_v_pool.reshape(-1, 2 * page, LANES)] * n_pages),
      *([sel_k_pool.reshape(-1, page, C_KV_WIDTH)] * n_pages), *([sel_v_pool.reshape(-1, page, C_KV_WIDTH)] * n_pages))


def _l2norm(x):
    return x * lax.rsqrt(jnp.sum(x * x, axis=-1, keepdims=True) + 1e-6)


def _rmsnorm(x, g):
    return x * lax.rsqrt(jnp.mean(x * x, axis=-1, keepdims=True) + RMS_EPS) * g


def _short_conv(x, buf, w):
    L = x.shape[1]
    ext = jnp.concatenate([buf, x], axis=1)
    y = sum(ext[:, i:i + L] * w[i] for i in range(GDN_CONV))
    return jax.nn.silu(y), ext[:, L:]


def _gated_delta(q, k, v, g, beta, s0):
    B, L, H, DK = q.shape
    DV = v.shape[-1]
    C = min(GDN_CHUNK, L)
    n = -(-L // C)
    pad = n * C - L

    def blocks(x):
        x = jnp.pad(x, [(0, 0), (0, pad)] + [(0, 0)] * (x.ndim - 2))
        x = x.reshape(B, n, C, *x.shape[2:])
        return jnp.transpose(x, (1, 0, 3, 2) + tuple(range(4, x.ndim)))

    qc = blocks(q) * DK ** -0.5
    kc, vc, bc = blocks(k), blocks(v), blocks(beta)
    gc = jnp.cumsum(blocks(g), axis=-1)
    idx = jnp.arange(C)
    causal = idx[:, None] >= idx[None, :]
    strict = idx[:, None] > idx[None, :]
    decay = jnp.exp(jnp.where(causal, gc[..., :, None] - gc[..., None, :], -jnp.inf))
    kb = kc * bc[..., None]
    a_mat = jnp.where(strict, jnp.einsum('nbhid,nbhjd->nbhij', kb, kc) * decay, 0.0)
    t_mat = a_mat + jnp.eye(C, dtype=_F32)
    rhs = jnp.concatenate([vc * bc[..., None], kb * jnp.exp(gc)[..., None]], axis=-1)
    sol = lax.linalg.triangular_solve(t_mat, rhs, left_side=True, lower=True, unit_diagonal=True)
    u, w = sol[..., :DV], sol[..., DV:]
    qk = jnp.where(causal, jnp.einsum('nbhid,nbhjd->nbhij', qc, kc) * decay, 0.0)

    def step(s, xs):
        q_i, k_i, u_i, w_i, g_i, qk_i = xs
        v_new = u_i - jnp.einsum('bhck,bhkv->bhcv', w_i, s)
        o_i = (jnp.einsum('bhck,bhkv->bhcv', q_i * jnp.exp(g_i)[..., None], s)
               + jnp.einsum('bhij,bhjv->bhiv', qk_i, v_new))
        g_last = g_i[..., -1:]
        s = s * jnp.exp(g_last)[..., None] + jnp.einsum('bhck,bhcv->bhkv', k_i * jnp.exp(g_last - g_i)[..., None], v_new)
        return s, o_i

    s_fin, o = lax.scan(step, s0, (qc, kc, u, w, gc, qk))
    o = jnp.transpose(o, (1, 0, 3, 2, 4)).reshape(B, n * C, H, DV)[:, :L]
    return o, s_fin


def _gdn_jax(qkv_a, a_in, b_in, z, conv_buf, s0, conv_w, a_log, dt_bias, gdn_norm):
    B, L, _ = qkv_a.shape
    qkv_a, new_buf = _short_conv(qkv_a, conv_buf, conv_w)
    q_a, k_a, v_a = qkv_a[..., :A_QK], qkv_a[..., A_QK:2 * A_QK], qkv_a[..., 2 * A_QK:]
    q_a = _l2norm(q_a.reshape(B, L, GDN_HEADS, GDN_DK))
    k_a = _l2norm(k_a.reshape(B, L, GDN_HEADS, GDN_DK))
    v_a = v_a.reshape(B, L, GDN_HEADS, GDN_DV)
    g = -jnp.exp(a_log) * jax.nn.softplus(a_in + dt_bias)
    beta = jax.nn.sigmoid(b_in)
    o_a, s_new = _gated_delta(q_a, k_a, v_a, g, beta, s0)
    o_a = _rmsnorm(o_a, gdn_norm) * jax.nn.silu(z.reshape(B, L, GDN_HEADS, GDN_DV))
    return o_a.reshape(B, L, A_WIDTH), s_new, new_buf


def _stack_layers(states):
    return [jnp.stack(a) for a in zip(*states)]


def kernel(x_prompt, x_sample, state_gdn_s, state_gdn_conv, cache_fox_k, cache_fox_v, cache_fox_logf, cache_nsa_cmp_k, cache_nsa_cmp_v, cache_nsa_sel_k, cache_nsa_sel_v, cache_nsa_win_k, cache_nsa_win_v, cache_mem_k, cache_mem_v, page_table, mem_prompt, norm_mix, norm_xattn, norm_mem, norm_ffn, norm_final, w_in_even, b_forget, gdn_conv_w, gdn_a_log, gdn_dt_bias, gdn_norm, w_out_even, w_in_odd, nsa_gate_bias, nsa_cmp_pos, nsa_cmp_w1, nsa_cmp_w2, w_out_odd, w_mem_q, w_mem_kv, w_mem_o, w_ffn_in, w_ffn_out):
    B, L, D = x_prompt.shape
    SB, SL, _ = x_sample.shape
    depth = norm_mix.shape[0]
    n_p = B * L
    n_s = SB * SL
    past_len = page_table.shape[1] * cache_fox_k.shape[2]
    pos_p = jnp.arange(L, dtype=jnp.int32)
    pos_s = past_len + jnp.arange(SL, dtype=jnp.int32)
    ffn_hidden = w_ffn_out.shape[1]
    row_tile = min(ROW_TILE, L)
    attn_tile = min(ATTN_TILE, L)
    mem_width = MEM_HEADS * MEM_DH
    keep = min(WINDOW, L)

    def rows_p(a, *shape):
        return a[:n_p].reshape(B, L, *shape)

    def rows_s(a, *shape):
        return a[n_p:].reshape(SB, SL, *shape)

    x = jnp.concatenate([x_prompt.reshape(n_p, D), x_sample.reshape(n_s, D)], axis=0)
    mem_flat = mem_prompt.reshape(B * MEM_TOKENS, D)
    even_p, even_s, odd_p, odd_s, mem_p = [], [], [], [], []
    for layer in range(depth):
        if layer % 2 == 0:
            e = layer // 2
            w = w_in_even[e]
            gate_cols = A_CONV_CH + 2 * GDN_HEADS
            w_perm = jnp.concatenate(
                [w[:, :A_CONV_CH], w[:, gate_cols:gate_cols + A_WIDTH + 3 * B_WIDTH], w[:, A_CONV_CH:gate_cols],
                 w[:, gate_cols + A_WIDTH + 3 * B_WIDTH:],
                 jnp.zeros((D, LANES - 2 * GDN_HEADS - FOX_HEADS), w.dtype)], axis=1).astype(_BF16)
            qkv_a, z, q_b, k_b, v_b, small = _rms_matmul_split(
                x, norm_mix[layer], w_perm, [A_CONV_CH, A_WIDTH, B_WIDTH, B_WIDTH, B_WIDTH, LANES], row_tile)
            a_in, b_in = small[:, :GDN_HEADS], small[:, GDN_HEADS:2 * GDN_HEADS]
            logf = jax.nn.log_sigmoid(small[:, 2 * GDN_HEADS:2 * GDN_HEADS + FOX_HEADS] + b_forget[e])
            gdn_w = (gdn_conv_w[e], gdn_a_log[e], gdn_dt_bias[e], gdn_norm[e])
            c_rows = jnp.transpose(jnp.cumsum(rows_p(logf, FOX_HEADS), axis=1), (0, 2, 1))
            o_b_p = _fox_prompt(q_b, k_b, v_b, c_rows, n_batch=B, seq=L, tile=attn_tile)
            o_a_p, s_p, buf_p = _gdn_jax(rows_p(qkv_a, A_CONV_CH), rows_p(a_in, GDN_HEADS), rows_p(b_in, GDN_HEADS),
                                         rows_p(z, A_WIDTH), jnp.zeros((B, GDN_CONV - 1, A_CONV_CH), _F32),
                                         jnp.zeros((B, GDN_HEADS, GDN_DK, GDN_DV), _F32), *gdn_w)
            even_p.append((s_p, buf_p, rows_p(k_b, FOX_HEADS, FOX_DH), rows_p(v_b, FOX_HEADS, FOX_DH),
                           rows_p(logf, FOX_HEADS)))
            c_pool = jnp.transpose(jnp.cumsum(cache_fox_logf[e], axis=1), (0, 2, 1))
            c_new = jnp.transpose(jnp.cumsum(rows_s(logf, FOX_HEADS), axis=1), (0, 2, 1))
            c_new = jnp.pad(c_new, ((0, 0), (0, 0), (0, LANES - SL)))
            o_b_s = _fox_sample(q_b, k_b, v_b, c_new, cache_fox_k[e], cache_fox_v[e], c_pool, page_table, row0=n_p, sl=SL)
            o_a_s, s_s, buf_s = _gdn_jax(rows_s(qkv_a, A_CONV_CH), rows_s(a_in, GDN_HEADS), rows_s(b_in, GDN_HEADS),
                                         rows_s(z, A_WIDTH), state_gdn_conv[e], state_gdn_s[e], *gdn_w)
            even_s.append((s_s, buf_s, rows_s(k_b, FOX_HEADS, FOX_DH), rows_s(v_b, FOX_HEADS, FOX_DH),
                           rows_s(logf, FOX_HEADS)))
            o_a = jnp.concatenate([o_a_p.reshape(n_p, A_WIDTH), o_a_s.reshape(n_s, A_WIDTH)], axis=0)
            o_b = jnp.concatenate([o_b_p, o_b_s], axis=0)
            w_out = w_out_even[e].astype(_BF16)
            x = _matmul_residual([o_a, o_b], [w_out[:A_WIDTH], w_out[A_WIDTH:]], x, row_tile)
        else:
            o = layer // 2
            q, qr, kc, vc, ks, vs, kw, vw, gates = _odd_proj(
                x, norm_mix[layer], w_in_odd[o], nsa_gate_bias[o], pos_p, pos_s, n_p, row_tile)
            cmp_w = (nsa_cmp_pos[o], nsa_cmp_w1[o], nsa_cmp_w2[o])
            o_cmp, sel = _nsa_cmp_prompt(q, gates, kc, vc, *cmp_w, n_batch=B, seq=L, tile=attn_tile)
            o_p = _nsa_selwin_prompt(qr, gates, sel, o_cmp, ks, vs, kw, vw, n_batch=B, seq=L, tile=attn_tile)
            kv4 = (NSA_KV_HEADS, NSA_DH)
            odd_p.append((rows_p(kc, *kv4), rows_p(vc, *kv4), rows_p(ks, *kv4), rows_p(vs, *kv4),
                          rows_p(kw, *kv4)[:, L - keep:], rows_p(vw, *kv4)[:, L - keep:]))
            w_buf = cache_nsa_win_k.shape[2]
            o_s = _nsa_sample(q, qr, gates, ks, vs, kw, vw,
                              cache_nsa_win_k[o].reshape(SB * w_buf, C_KV_WIDTH), cache_nsa_win_v[o].reshape(SB * w_buf, C_KV_WIDTH),
                              cache_nsa_cmp_k[o], cache_nsa_cmp_v[o], cache_nsa_sel_k[o], cache_nsa_sel_v[o],
                              page_table, *cmp_w, row0=n_p, sl=SL)
            new_wk = jnp.concatenate([cache_nsa_win_k[o], rows_s(kw, *kv4)], axis=1)[:, SL:]
            new_wv = jnp.concatenate([cache_nsa_win_v[o], rows_s(vw, *kv4)], axis=1)[:, SL:]
            odd_s.append((rows_s(kc, *kv4), rows_s(vc, *kv4), rows_s(ks, *kv4), rows_s(vs, *kv4), new_wk, new_wv))
            o_all = jnp.concatenate([o_p, o_s], axis=0)
            x = _matmul_residual([o_all], [w_out_odd[o].astype(_BF16)], x, row_tile)

        mkv = _rms_matmul(mem_flat, norm_mem[layer], w_mem_kv[layer].astype(_BF16), row_tile=min(ROW_TILE, B * MEM_TOKENS))
        mem_p.append((mkv[:, :mem_width].reshape(B, MEM_TOKENS, MEM_HEADS, MEM_DH),
                      mkv[:, mem_width:].reshape(B, MEM_TOKENS, MEM_HEADS, MEM_DH)))
        q = _rms_matmul(x, norm_xattn[layer], w_mem_q[layer].astype(_BF16), row_tile=row_tile)
        o_p = _mem_attn(q, mkv, mkv, n_batch=B, q_len=L, q_row0=0, q_tile=row_tile, k_col_block=0, v_col_block=1)
        o_s = _mem_attn(q, cache_mem_k[layer].reshape(SB * MEM_TOKENS, mem_width),
                        cache_mem_v[layer].reshape(SB * MEM_TOKENS, mem_width),
                        n_batch=SB, q_len=SL, q_row0=n_p, q_tile=SL, k_col_block=0, v_col_block=0)
        o_mem = jnp.concatenate([o_p.astype(_F32), o_s], axis=0)
        x = _matmul_residual([o_mem], [w_mem_o[layer].astype(_BF16)], x, row_tile)

        x = _ffn(x, norm_ffn[layer], w_ffn_in[layer][:, :ffn_hidden].astype(_BF16),
                 w_ffn_in[layer][:, ffn_hidden:].astype(_BF16), w_ffn_out[layer].astype(_BF16), row_tile)

    y = _rms(x, norm_final, row_tile)
    y_prompt = y[:n_p].reshape(B, L, D)
    y_sample = y[n_p:].reshape(SB, SL, D)
    p_gdn_s, p_gdn_conv, p_fox_k, p_fox_v, p_fox_logf = _stack_layers(even_p)
    s_gdn_s, s_gdn_conv, s_fox_k, s_fox_v, s_fox_logf = _stack_layers(even_s)
    p_cmp_k, p_cmp_v, p_sel_k, p_sel_v, p_win_k, p_win_v = _stack_layers(odd_p)
    s_cmp_k, s_cmp_v, s_sel_k, s_sel_v, s_win_k, s_win_v = _stack_layers(odd_s)
    p_mem_k, p_mem_v = _stack_layers(mem_p)
    return (y_prompt, y_sample,
            p_gdn_s, p_gdn_conv, p_fox_k, p_fox_v, p_fox_logf,
            p_cmp_k, p_cmp_v, p_sel_k, p_sel_v, p_win_k, p_win_v, p_mem_k, p_mem_v,
            s_gdn_s, s_gdn_conv, s_fox_k, s_fox_v, s_fox_logf,
            s_cmp_k, s_cmp_v, s_sel_k, s_sel_v, s_win_k, s_win_v)
```

```python
import functools

import jax
import jax.numpy as jnp
import numpy as np
from jax import lax
from jax.experimental import pallas as pl
from jax.experimental.pallas import tpu as pltpu

D_MODEL = 1024
RMS_EPS = 1e-6
NEG_BIG = -1e30

GDN_HEADS = 4
GDN_DK = 128
GDN_DV = 128
GDN_CONV = 4
GDN_CHUNK = 64
A_QK = GDN_HEADS * GDN_DK
A_WIDTH = GDN_HEADS * GDN_DV
A_CONV_CH = 2 * A_QK + A_WIDTH

FOX_HEADS = 8
FOX_DH = 64
B_WIDTH = FOX_HEADS * FOX_DH

NSA_HEADS = 16
NSA_KV_HEADS = 4
NSA_GROUP = NSA_HEADS // NSA_KV_HEADS
NSA_DH = 64
C_WIDTH = NSA_HEADS * NSA_DH
C_KV_WIDTH = NSA_KV_HEADS * NSA_DH
CMP_LEN = 32
CMP_STRIDE = CMP_LEN // 2
SEL_BLOCK = 64
SEL_TOP_N = 16
WINDOW = 512

ROPE_THETA = 500000.0
ROT_DIM = NSA_DH // 4

MEM_TOKENS = 256
MEM_HEADS = 4
MEM_DH = D_MODEL // MEM_HEADS

LANES = 128
VMEM_LIMIT_BYTES = 56 * 1024 * 1024
ROW_TILE = 512
ATTN_TILE = 256

_BF16 = jnp.bfloat16
_F32 = jnp.float32


def _params(n_grid_dims):
    return pltpu.CompilerParams(
        dimension_semantics=("arbitrary",) * n_grid_dims,
        vmem_limit_bytes=VMEM_LIMIT_BYTES)


def _rms_rows(x, g):
    return x * lax.rsqrt(jnp.mean(x * x, axis=-1, keepdims=True) + RMS_EPS) * g


def _rms_matmul_kernel(x_ref, g_ref, w_ref, o_ref, *, col_chunk):
    xn = _rms_rows(x_ref[...], g_ref[...]).astype(_BF16)
    width = w_ref.shape[1]
    for c0 in range(0, width, col_chunk):
        c1 = min(width, c0 + col_chunk)
        o_ref[:, c0:c1] = jnp.dot(xn, w_ref[:, c0:c1], preferred_element_type=_F32).astype(o_ref.dtype)


def _rms_matmul(x, g, w_bf16, out_dtype=_F32, row_tile=ROW_TILE):
    n, d = x.shape
    width = w_bf16.shape[1]
    assert n % row_tile == 0 and width % LANES == 0
    return pl.pallas_call(
        functools.partial(_rms_matmul_kernel, col_chunk=512),
        grid=(n // row_tile,),
        in_specs=[pl.BlockSpec((row_tile, d), lambda i: (i, 0)),
                  pl.BlockSpec((1, d), lambda i: (0, 0)),
                  pl.BlockSpec((d, width), lambda i: (0, 0))],
        out_specs=pl.BlockSpec((row_tile, width), lambda i: (i, 0)),
        out_shape=jax.ShapeDtypeStruct((n, width), out_dtype),
        compiler_params=_params(1),
        name="rms_matmul",
    )(x, g.reshape(1, d), w_bf16)


def _matmul_residual_kernel(*refs, n_in):
    r_ref, o_ref = refs[2 * n_in], refs[2 * n_in + 1]
    acc = r_ref[...]
    for a_ref, w_ref in zip(refs[:n_in], refs[n_in:2 * n_in]):
        acc = acc + jnp.dot(a_ref[...].astype(_BF16), w_ref[...], preferred_element_type=_F32)
    o_ref[...] = acc


def _matmul_residual(a_list, w_list, resid, row_tile=ROW_TILE):
    n, d = resid.shape
    n_in = len(a_list)
    assert n % row_tile == 0
    return pl.pallas_call(
        functools.partial(_matmul_residual_kernel, n_in=n_in),
        grid=(n // row_tile,),
        in_specs=[pl.BlockSpec((row_tile, a.shape[1]), lambda i: (i, 0)) for a in a_list]
                 + [pl.BlockSpec(w.shape, lambda i: (0, 0)) for w in w_list]
                 + [pl.BlockSpec((row_tile, d), lambda i: (i, 0))],
        out_specs=pl.BlockSpec((row_tile, d), lambda i: (i, 0)),
        out_shape=jax.ShapeDtypeStruct((n, d), _F32),
        input_output_aliases={2 * n_in: 0},
        compiler_params=_params(1),
        name="matmul_residual",
    )(*a_list, *w_list, resid)


def _ffn_kernel(x_ref, g_ref, wg_ref, wu_ref, wo_ref, o_ref, *, chunk):
    x = x_ref[...]
    xn = _rms_rows(x, g_ref[...]).astype(_BF16)
    o_ref[...] = x
    hidden = wg_ref.shape[1]
    for c0 in range(0, hidden, chunk):
        gate = jnp.dot(xn, wg_ref[:, c0:c0 + chunk], preferred_element_type=_F32)
        up = jnp.dot(xn, wu_ref[:, c0:c0 + chunk], preferred_element_type=_F32)
        h = (gate * jax.nn.sigmoid(gate) * up).astype(_BF16)
        o_ref[...] += jnp.dot(h, wo_ref[c0:c0 + chunk, :], preferred_element_type=_F32)


def _ffn(x, g, wg, wu, wo, row_tile=ROW_TILE):
    n, d = x.shape
    hidden = wg.shape[1]
    chunk = 256
    assert n % row_tile == 0 and hidden % chunk == 0
    const = lambda i: (0, 0)
    return pl.pallas_call(
        functools.partial(_ffn_kernel, chunk=chunk),
        grid=(n // row_tile,),
        in_specs=[pl.BlockSpec((row_tile, d), lambda i: (i, 0)),
                  pl.BlockSpec((1, d), const),
                  pl.BlockSpec((d, hidden), const, pipeline_mode=pl.Buffered(1)),
                  pl.BlockSpec((d, hidden), const, pipeline_mode=pl.Buffered(1)),
                  pl.BlockSpec((hidden, d), const, pipeline_mode=pl.Buffered(1))],
        out_specs=pl.BlockSpec((row_tile, d), lambda i: (i, 0)),
        out_shape=jax.ShapeDtypeStruct((n, d), _F32),
        input_output_aliases={0: 0},
        compiler_params=_params(1),
        name="ffn",
    )(x, g.reshape(1, d), wg, wu, wo)


def _mem_attn_kernel(q_ref, k_ref, v_ref, o_ref):
    scale = MEM_DH ** -0.5
    for h in range(MEM_HEADS):
        cols = slice(h * MEM_DH, (h + 1) * MEM_DH)
        q = q_ref[:, cols].astype(_BF16)
        k = k_ref[:, cols].astype(_BF16)
        v = v_ref[:, cols].astype(_BF16)
        s = lax.dot_general(q, k, (((1,), (1,)), ((), ())), preferred_element_type=_F32) * scale
        p = jnp.exp(s - jnp.max(s, axis=-1, keepdims=True))
        inv = 1.0 / jnp.sum(p, axis=-1, keepdims=True)
        o = jnp.dot(p.astype(_BF16), v, preferred_element_type=_F32) * inv
        o_ref[:, cols] = o.astype(o_ref.dtype)


def _mem_attn(q, k, v, *, n_batch, q_len, q_row0, q_tile, k_col_block, v_col_block):
    assert q_len % q_tile == 0 and q_row0 % q_tile == 0
    tiles = q_len // q_tile
    t0 = q_row0 // q_tile
    width = MEM_HEADS * MEM_DH
    return pl.pallas_call(
        _mem_attn_kernel,
        grid=(n_batch, tiles),
        in_specs=[pl.BlockSpec((q_tile, width), lambda b, i: (t0 + b * tiles + i, 0)),
                  pl.BlockSpec((MEM_TOKENS, width), lambda b, i: (b, k_col_block)),
                  pl.BlockSpec((MEM_TOKENS, width), lambda b, i: (b, v_col_block))],
        out_specs=pl.BlockSpec((q_tile, width), lambda b, i: (b * tiles + i, 0)),
        out_shape=jax.ShapeDtypeStruct((n_batch * q_len, width), _BF16 if q_tile % 16 == 0 else _F32),
        compiler_params=_params(2),
        name="mem_attn",
    )(q, k, v)


def _rms_kernel(x_ref, g_ref, o_ref):
    o_ref[...] = _rms_rows(x_ref[...], g_ref[...])


def _rms(x, g, row_tile=ROW_TILE):
    n, d = x.shape
    return pl.pallas_call(
        _rms_kernel,
        grid=(n // row_tile,),
        in_specs=[pl.BlockSpec((row_tile, d), lambda i: (i, 0)),
                  pl.BlockSpec((1, d), lambda i: (0, 0))],
        out_specs=pl.BlockSpec((row_tile, d), lambda i: (i, 0)),
        out_shape=jax.ShapeDtypeStruct((n, d), _F32),
        compiler_params=_params(1),
        name="final_rms",
    )(x, g.reshape(1, d))


def _rms_matmul_split_kernel(x_ref, g_ref, w_ref, *o_refs, col_chunk):
    xn = _rms_rows(x_ref[...], g_ref[...]).astype(_BF16)
    c0 = 0
    for o_ref in o_refs:
        width = o_ref.shape[1]
        for s0 in range(0, width, col_chunk):
            s1 = min(width, s0 + col_chunk)
            o_ref[:, s0:s1] = jnp.dot(xn, w_ref[:, c0 + s0:c0 + s1], preferred_element_type=_F32)
        c0 += width


def _rms_matmul_split(x, g, w_bf16, widths, row_tile=ROW_TILE):
    n, d = x.shape
    assert n % row_tile == 0 and sum(widths) == w_bf16.shape[1] and all(w % LANES == 0 for w in widths)
    return pl.pallas_call(
        functools.partial(_rms_matmul_split_kernel, col_chunk=512),
        grid=(n // row_tile,),
        in_specs=[pl.BlockSpec((row_tile, d), lambda i: (i, 0)),
                  pl.BlockSpec((1, d), lambda i: (0, 0)),
                  pl.BlockSpec((d, w_bf16.shape[1]), lambda i: (0, 0))],
        out_specs=[pl.BlockSpec((row_tile, w), lambda i: (i, 0)) for w in widths],
        out_shape=[jax.ShapeDtypeStruct((n, w), _F32) for w in widths],
        compiler_params=_params(1),
        name="rms_matmul_split",
    )(x, g.reshape(1, d), w_bf16)


def _rope_apply(x, cos, sin_up, sin_dn):
    width = x.shape[1]
    reps = width // cos.shape[1]
    if reps > 1:
        cos, sin_up, sin_dn = [jnp.concatenate([t] * reps, axis=1) for t in (cos, sin_up, sin_dn)]
    return x * cos + pltpu.roll(x, width - ROT_DIM // 2, 1) * sin_up + pltpu.roll(x, ROT_DIM // 2, 1) * sin_dn


def _odd_proj_kernel(x_ref, g_ref, w_ref, gb_ref, cos_ref, sup_ref, sdn_ref,
                     q_ref, qr_ref, kc_ref, vc_ref, ks_ref, vs_ref, kw_ref, vw_ref, gate_ref):
    xn = _rms_rows(x_ref[...], g_ref[...]).astype(_BF16)
    cos, sup, sdn = cos_ref[...], sup_ref[...], sdn_ref[...]
    kvw = C_KV_WIDTH
    for h0 in range(0, C_WIDTH, 512):
        q = jnp.dot(xn, w_ref[:, h0:h0 + 512], preferred_element_type=_F32)
        q_ref[:, h0:h0 + 512] = q
        qr_ref[:, h0:h0 + 512] = _rope_apply(q, cos, sup, sdn)
    c0 = C_WIDTH
    for o_ref, rot in ((kc_ref, False), (vc_ref, False), (ks_ref, True), (vs_ref, False), (kw_ref, True), (vw_ref, False)):
        y = jnp.dot(xn, w_ref[:, c0:c0 + kvw], preferred_element_type=_F32)
        o_ref[...] = _rope_apply(y, cos, sup, sdn) if rot else y
        c0 += kvw
    gl = jnp.dot(xn, w_ref[:, c0:c0 + LANES], preferred_element_type=_F32)
    gate_ref[...] = jax.nn.sigmoid(gl + gb_ref[...])


def _rope_tables(pos):
    half = ROT_DIM // 2
    inv_freq = ROPE_THETA ** (-jnp.arange(half, dtype=_F32) / half)
    ang = pos.astype(_F32)[:, None] * inv_freq[None, :]
    cos, sin = jnp.cos(ang), jnp.sin(ang)
    rest = NSA_DH - ROT_DIM
    ones = jnp.ones((pos.shape[0], rest), _F32)
    zeros = jnp.zeros((pos.shape[0], rest), _F32)
    z8 = jnp.zeros_like(sin)
    head_cos = jnp.concatenate([cos, cos, ones], axis=1)
    head_up = jnp.concatenate([-sin, z8, zeros], axis=1)
    head_dn = jnp.concatenate([z8, sin, zeros], axis=1)
    return [jnp.tile(t, (1, NSA_KV_HEADS)) for t in (head_cos, head_up, head_dn)]


def _odd_proj(x, g, w_in, gate_bias, pos_p, pos_s, n_p, row_tile=ROW_TILE):
    n, d = x.shape
    L, SL = pos_p.shape[0], pos_s.shape[0]
    assert L % row_tile == 0 and n_p % row_tile == 0 and (n - n_p) % row_tile == 0 and row_tile % SL == 0
    head = np.arange(NSA_HEADS)
    gl_cols = np.concatenate([C_WIDTH + 6 * C_KV_WIDTH + head * 3 + br for br in range(3)])
    w = jnp.concatenate([w_in[:, :C_WIDTH + 6 * C_KV_WIDTH], w_in[:, gl_cols],
                         jnp.zeros((d, LANES - 3 * NSA_HEADS), w_in.dtype)], axis=1).astype(_BF16)
    gb = jnp.concatenate([gate_bias[gl_cols - (C_WIDTH + 6 * C_KV_WIDTH)], jnp.zeros((LANES - 3 * NSA_HEADS,), _F32)]).reshape(1, LANES)
    tables = _rope_tables(jnp.concatenate([pos_p, jnp.tile(pos_s, row_tile // SL)]))
    tiles_per_seq = L // row_tile
    n_p_tiles = n_p // row_tile
    tab_map = lambda i: (jnp.where(i < n_p_tiles, i % tiles_per_seq, tiles_per_seq), 0)
    row_map = lambda i: (i, 0)
    const = lambda i: (0, 0)
    widths = [C_WIDTH, C_WIDTH] + [C_KV_WIDTH] * 6 + [LANES]
    return pl.pallas_call(
        _odd_proj_kernel,
        grid=(n // row_tile,),
        in_specs=[pl.BlockSpec((row_tile, d), row_map),
                  pl.BlockSpec((1, d), const),
                  pl.BlockSpec((d, w.shape[1]), const),
                  pl.BlockSpec((1, LANES), const)]
                 + [pl.BlockSpec((row_tile, C_KV_WIDTH), tab_map)] * 3,
        out_specs=[pl.BlockSpec((row_tile, wd), row_map) for wd in widths],
        out_shape=[jax.ShapeDtypeStruct((n, wd), _F32) for wd in widths],
        compiler_params=_params(1),
        name="odd_proj",
    )(x, g.reshape(1, d), w, gb, *tables)


def _online_softmax_step(s, mask, v_bf16, m, l, acc):
    s = jnp.where(mask, s, NEG_BIG)
    m_new = jnp.maximum(m, jnp.max(s, axis=-1, keepdims=True))
    alpha = jnp.exp(m - m_new)
    p = jnp.where(mask, jnp.exp(s - m_new), 0.0)
    l = alpha * l + jnp.sum(p, axis=-1, keepdims=True)
    acc = alpha * acc + jnp.dot(p.astype(_BF16), v_bf16, preferred_element_type=_F32)
    return m_new, l, acc


def _fox_prompt_kernel(q_ref, k_ref, v_ref, c_ref, o_ref, *, tile):
    i = pl.program_id(1)
    scale = FOX_DH ** -0.5
    row = lax.broadcasted_iota(jnp.int32, (tile, tile), 0)
    col = lax.broadcasted_iota(jnp.int32, (tile, tile), 1)
    for h in range(FOX_HEADS):
        cols = slice(h * FOX_DH, (h + 1) * FOX_DH)
        q = (q_ref[:, cols] * scale).astype(_BF16)

        def body(j, carry, cols=cols, q=q, h=h):
            k0 = pl.multiple_of(j * tile, tile)
            k = k_ref[pl.ds(k0, tile), cols].astype(_BF16)
            v = v_ref[pl.ds(k0, tile), cols].astype(_BF16)
            s = lax.dot_general(q, k, (((1,), (1,)), ((), ())), preferred_element_type=_F32)
            s = s - c_ref[0, h:h + 1, pl.ds(k0, tile)]
            mask = (col + k0) <= (row + i * tile)
            return _online_softmax_step(s, mask, v, *carry)

        init = (jnp.full((tile, 1), NEG_BIG, _F32), jnp.zeros((tile, 1), _F32), jnp.zeros((tile, FOX_DH), _F32))
        m, l, acc = lax.fori_loop(0, i + 1, body, init)
        o_ref[:, cols] = acc / jnp.maximum(l, 1e-30)


def _fox_prompt(q, k, v, c_rows, *, n_batch, seq, tile):
    assert seq % tile == 0
    tiles = seq // tile
    return pl.pallas_call(
        functools.partial(_fox_prompt_kernel, tile=tile),
        grid=(n_batch, tiles),
        in_specs=[pl.BlockSpec((tile, B_WIDTH), lambda b, i: (b * tiles + i, 0)),
                  pl.BlockSpec((seq, B_WIDTH), lambda b, i: (b, 0)),
                  pl.BlockSpec((seq, B_WIDTH), lambda b, i: (b, 0)),
                  pl.BlockSpec((1, FOX_HEADS, seq), lambda b, i: (b, 0, 0))],
        out_specs=pl.BlockSpec((tile, B_WIDTH), lambda b, i: (b * tiles + i, 0)),
        out_shape=jax.ShapeDtypeStruct((n_batch * seq, B_WIDTH), _F32),
        compiler_params=_params(2),
        name="fox_prompt",
    )(q, k, v, c_rows)


MASKED = -2e30


def _softmax_step(s, v_bf16, m, l, acc):
    m_new = jnp.maximum(m, jnp.max(s, axis=-1, keepdims=True))
    alpha = jnp.exp(m - m_new)
    p = jnp.exp(s - m_new)
    l = alpha * l + jnp.sum(p, axis=-1, keepdims=True)
    acc = alpha * acc + jnp.dot(p.astype(_BF16), v_bf16, preferred_element_type=_F32)
    return m_new, l, acc


def _split3_dot(x, w_bf16):
    hi = x.astype(_BF16)
    r1 = x - hi.astype(_F32)
    mid = r1.astype(_BF16)
    lo = (r1 - mid.astype(_F32)).astype(_BF16)
    dot = lambda a: jnp.dot(a, w_bf16, preferred_element_type=_F32)
    return dot(hi) + dot(mid) + dot(lo)


def _compress_into(x2_ref, n_sub, wpair_ref, w1_ref, w2_ref, pos_ref, out_ref):
    pair_rows = []
    for p in range(NSA_KV_HEADS // 2):
        pieces = [x2_ref[pl.ds(2 * r + p, n_sub, stride=2 * CMP_STRIDE), :] for r in range(CMP_STRIDE)]
        pair_rows.append(jnp.concatenate(pieces, axis=1))
    x = jnp.concatenate(pair_rows, axis=0).astype(_BF16)
    both = jnp.dot(x, wpair_ref[...], preferred_element_type=_F32)
    bias = jnp.dot(pos_ref[...].astype(_BF16), w1_ref[...], preferred_element_type=_F32)
    hid_w = 2 * NSA_DH
    for kv in range(NSA_KV_HEADS):
        p, s = divmod(kv, 2)
        rows = slice(p * n_sub, (p + 1) * n_sub)
        first = both[rows, s * hid_w:(s + 1) * hid_w]
        second = both[rows, (2 + s) * hid_w:(3 + s) * hid_w]
        pre = first + pltpu.roll(second, n_sub - 1, 0) + bias
        hid = (pre * jax.nn.sigmoid(pre)).astype(_BF16)
        out_ref[0:n_sub, kv * NSA_DH:(kv + 1) * NSA_DH] = jnp.dot(hid, w2_ref[...], preferred_element_type=_F32)


def _cmp_branch(q_ref, gate_ref, kcb_ref, vcb_ref, c2s_ref, o_ref, sel_ref, q_pos, n_sel):
    rows = q_ref.shape[0]
    nb = kcb_ref.shape[0]
    scale = NSA_DH ** -0.5
    blk_n = lax.broadcasted_iota(jnp.int32, (rows, nb), 1)
    cbias = jnp.where(blk_n * CMP_STRIDE + (CMP_LEN - 1) <= q_pos, 0.0, MASKED)
    lane = lax.broadcasted_iota(jnp.int32, (rows, LANES), 1)
    cur = q_pos // SEL_BLOCK
    causal = (lane <= cur) & (lane < n_sel)
    forced = (lane == 0) | (lane == cur) | (lane == cur - 1)
    for kv in range(NSA_KV_HEADS):
        kcols = slice(kv * NSA_DH, (kv + 1) * NSA_DH)
        kcb = kcb_ref[:, kcols].astype(_BF16)
        vcb = vcb_ref[:, kcols].astype(_BF16)
        p_sum = jnp.zeros((rows, nb), _F32)
        for g in range(NSA_GROUP):
            h = kv * NSA_GROUP + g
            hcols = slice(h * NSA_DH, (h + 1) * NSA_DH)
            q = (q_ref[:, hcols] * scale).astype(_BF16)
            s = lax.dot_general(q, kcb, (((1,), (1,)), ((), ())), preferred_element_type=_F32) + cbias
            m = jnp.maximum(jnp.max(s, axis=-1, keepdims=True), NEG_BIG)
            p = jnp.exp(s - m)
            p = p / jnp.maximum(jnp.sum(p, axis=-1, keepdims=True), 1e-30)
            p_sum = p_sum + p
            o = jnp.dot(p.astype(_BF16), vcb, preferred_element_type=_F32)
            o_ref[:, hcols] = o * gate_ref[:, h:h + 1]
        imp = _split3_dot(p_sum, c2s_ref[...])
        imp = jnp.where(causal, jnp.where(forced, -NEG_BIG, imp), NEG_BIG)
        ahead = jnp.zeros((rows, LANES), _F32)
        for j in range(n_sel):
            cj = imp[:, j:j + 1]
            ahead = ahead + jnp.where((cj > imp) | ((cj == imp) & (lane > j)), 1.0, 0.0)
        sel = jnp.where((ahead < float(min(SEL_TOP_N, n_sel))) & (imp > 0.5 * NEG_BIG), 1.0, 0.0)
        sel_ref[:, kv * LANES:(kv + 1) * LANES] = sel


def _cmp_to_sel(nb):
    c_start = np.arange(nb)[:, None] * CMP_STRIDE
    s_start = np.arange(LANES)[None, :] * SEL_BLOCK
    shared = np.minimum(c_start + CMP_LEN, s_start + SEL_BLOCK) - np.maximum(c_start, s_start)
    return jnp.asarray(np.clip(shared, 0, None).astype(np.float32) / CMP_LEN, _BF16)


def _sel_expand(n_keys):
    return jnp.asarray((np.arange(n_keys)[None, :] // SEL_BLOCK == np.arange(LANES)[:, None]).astype(np.float32), _BF16)


def _nsa_cmp_prompt_kernel(q_ref, gate_ref, kc_ref, vc_ref, wpk_ref, w1k_ref, w2k_ref, posk_ref,
                           wpv_ref, w1v_ref, w2v_ref, posv_ref, c2s_ref, o_ref, sel_ref, kcb_ref, vcb_ref,
                           *, tile, n_sub, n_sel):
    i = pl.program_id(1)

    @pl.when(i == 0)
    def _():
        kcb_ref[...] = jnp.zeros_like(kcb_ref)
        vcb_ref[...] = jnp.zeros_like(vcb_ref)
        _compress_into(kc_ref, n_sub, wpk_ref, w1k_ref, w2k_ref, posk_ref, kcb_ref)
        _compress_into(vc_ref, n_sub, wpv_ref, w1v_ref, w2v_ref, posv_ref, vcb_ref)

    q_pos = i * tile + lax.broadcasted_iota(jnp.int32, (tile, 1), 0)
    _cmp_branch(q_ref, gate_ref, kcb_ref, vcb_ref, c2s_ref, o_ref, sel_ref, q_pos, n_sel)


def _cmp_weights(cmp_pos, cmp_w1, cmp_w2):
    out = []
    eye2 = jnp.eye(2, dtype=_F32)
    for idx in range(2):
        w1r = cmp_w1[idx].reshape(CMP_LEN, NSA_DH, 2 * NSA_DH)
        halves = [jnp.einsum('rdj,st->rsdtj', w1r[h * CMP_STRIDE:(h + 1) * CMP_STRIDE], eye2).reshape(
            CMP_STRIDE * 2 * NSA_DH, 4 * NSA_DH) for h in range(2)]
        out += [jnp.concatenate(halves, axis=1).astype(_BF16), cmp_w1[idx].astype(_BF16), cmp_w2[idx].astype(_BF16),
                cmp_pos[idx].reshape(1, CMP_LEN * NSA_DH)]
    return out


def _cmp_weight_specs(const):
    return [pl.BlockSpec((CMP_STRIDE * 2 * NSA_DH, 8 * NSA_DH), const),
            pl.BlockSpec((CMP_LEN * NSA_DH, 2 * NSA_DH), const),
            pl.BlockSpec((2 * NSA_DH, NSA_DH), const),
            pl.BlockSpec((1, CMP_LEN * NSA_DH), const)] * 2


def _nsa_cmp_prompt(q, gates, kc, vc, cmp_pos, cmp_w1, cmp_w2, *, n_batch, seq, tile):
    assert seq % tile == 0 and seq % SEL_BLOCK == 0
    tiles = seq // tile
    n_sub = seq // CMP_STRIDE
    nb = -(-n_sub // LANES) * LANES
    n_sel = seq // SEL_BLOCK
    row_map = lambda b, i: (b * tiles + i, 0)
    seq_map = lambda b, i: (b, 0)
    const = lambda b, i: (0, 0)
    return pl.pallas_call(
        functools.partial(_nsa_cmp_prompt_kernel, tile=tile, n_sub=n_sub, n_sel=n_sel),
        grid=(n_batch, tiles),
        in_specs=[pl.BlockSpec((tile, C_WIDTH), row_map),
                  pl.BlockSpec((tile, LANES), row_map),
                  pl.BlockSpec((2 * seq, LANES), seq_map),
                  pl.BlockSpec((2 * seq, LANES), seq_map)] + _cmp_weight_specs(const)
                 + [pl.BlockSpec((nb, LANES), const)],
        out_specs=[pl.BlockSpec((tile, C_WIDTH), row_map),
                   pl.BlockSpec((tile, NSA_KV_HEADS * LANES), row_map)],
        out_shape=[jax.ShapeDtypeStruct((n_batch * seq, C_WIDTH), _F32),
                   jax.ShapeDtypeStruct((n_batch * seq, NSA_KV_HEADS * LANES), _F32)],
        scratch_shapes=[pltpu.VMEM((nb, C_KV_WIDTH), _F32), pltpu.VMEM((nb, C_KV_WIDTH), _F32)],
        compiler_params=_params(2),
        name="nsa_cmp_prompt",
    )(q, gates, kc.reshape(-1, LANES), vc.reshape(-1, LANES), *_cmp_weights(cmp_pos, cmp_w1, cmp_w2), _cmp_to_sel(nb))


def _nsa_selwin_prompt_kernel(qr_ref, gate_ref, sel_ref, ocmp_ref, ks_ref, vs_ref, kw_ref, vw_ref, exp_ref,
                              o_ref, selbias_ref, *, tile):
    i = pl.program_id(1)
    scale = NSA_DH ** -0.5
    q0 = i * tile
    row = lax.broadcasted_iota(jnp.int32, (tile, tile), 0) + q0
    col = lax.broadcasted_iota(jnp.int32, (tile, tile), 1)
    rows4 = NSA_GROUP * tile
    init = (jnp.full((rows4, 1), NEG_BIG, _F32), jnp.zeros((rows4, 1), _F32), jnp.zeros((rows4, NSA_DH), _F32))
    first_win = jnp.maximum(i - (WINDOW + tile - 1) // tile, 0)
    for kv in range(NSA_KV_HEADS):
        kcols = slice(kv * NSA_DH, (kv + 1) * NSA_DH)
        picked = jnp.dot(sel_ref[:, kv * LANES:(kv + 1) * LANES].astype(_BF16), exp_ref[...], preferred_element_type=_F32)
        selbias_ref[...] = (1.0 - picked) * MASKED
        q = jnp.concatenate(
            [(qr_ref[:, (kv * NSA_GROUP + g) * NSA_DH:(kv * NSA_GROUP + g + 1) * NSA_DH] * scale).astype(_BF16)
             for g in range(NSA_GROUP)], axis=0)

        def sel_body(j, carry, q=q, kcols=kcols):
            k0 = pl.multiple_of(j * tile, tile)
            k = ks_ref[pl.ds(k0, tile), kcols].astype(_BF16)
            v = vs_ref[pl.ds(k0, tile), kcols].astype(_BF16)
            bias = selbias_ref[:, pl.ds(k0, tile)] + jnp.where(col + k0 <= row, 0.0, MASKED)
            s = lax.dot_general(q, k, (((1,), (1,)), ((), ())), preferred_element_type=_F32)
            return _softmax_step(s + jnp.concatenate([bias] * NSA_GROUP, axis=0), v, *carry)

        def win_body(j, carry, q=q, kcols=kcols):
            k0 = pl.multiple_of(j * tile, tile)
            k = kw_ref[pl.ds(k0, tile), kcols].astype(_BF16)
            v = vw_ref[pl.ds(k0, tile), kcols].astype(_BF16)
            dist = row - (col + k0)
            bias = jnp.where((dist >= 0) & (dist < WINDOW), 0.0, MASKED)
            s = lax.dot_general(q, k, (((1,), (1,)), ((), ())), preferred_element_type=_F32)
            return _softmax_step(s + jnp.concatenate([bias] * NSA_GROUP, axis=0), v, *carry)

        _, l_s, acc_s = lax.fori_loop(0, i + 1, sel_body, init)
        _, l_w, acc_w = lax.fori_loop(first_win, i + 1, win_body, init)
        o_s = acc_s / jnp.maximum(l_s, 1e-30)
        o_w = acc_w / jnp.maximum(l_w, 1e-30)
        for g in range(NSA_GROUP):
            h = kv * NSA_GROUP + g
            hcols = slice(h * NSA_DH, (h + 1) * NSA_DH)
            rws = slice(g * tile, (g + 1) * tile)
            o_ref[:, hcols] = (ocmp_ref[:, hcols]
                               + o_s[rws] * gate_ref[:, NSA_HEADS + h:NSA_HEADS + h + 1]
                               + o_w[rws] * gate_ref[:, 2 * NSA_HEADS + h:2 * NSA_HEADS + h + 1])


def _nsa_selwin_prompt(qr, gates, sel, o_cmp, ks, vs, kw, vw, *, n_batch, seq, tile):
    assert seq % tile == 0
    tiles = seq // tile
    row_map = lambda b, i: (b * tiles + i, 0)
    seq_map = lambda b, i: (b, 0)
    return pl.pallas_call(
        functools.partial(_nsa_selwin_prompt_kernel, tile=tile),
        grid=(n_batch, tiles),
        in_specs=[pl.BlockSpec((tile, C_WIDTH), row_map),
                  pl.BlockSpec((tile, LANES), row_map),
                  pl.BlockSpec((tile, NSA_KV_HEADS * LANES), row_map),
                  pl.BlockSpec((tile, C_WIDTH), row_map)]
                 + [pl.BlockSpec((seq, C_KV_WIDTH), seq_map)] * 4
                 + [pl.BlockSpec((LANES, seq), lambda b, i: (0, 0))],
        out_specs=pl.BlockSpec((tile, C_WIDTH), row_map),
        out_shape=jax.ShapeDtypeStruct((n_batch * seq, C_WIDTH), _F32),
        scratch_shapes=[pltpu.VMEM((tile, seq), _F32)],
        compiler_params=_params(2),
        name="nsa_selwin_prompt",
    )(qr, gates, sel, o_cmp, ks, vs, kw, vw, _sel_expand(seq))


def _page_specs(block, n_pages):
    return [pl.BlockSpec(block, functools.partial(lambda b, pt, p: (pt[b, p],) + (0,) * (len(block) - 1), p=p))
            for p in range(n_pages)]


def _pad_rows_to(x, rows):
    return jnp.concatenate([x, jnp.zeros((rows - x.shape[0], x.shape[1]), x.dtype)], axis=0)


def _fox_sample_kernel(pt_ref, q_ref, kn_ref, vn_ref, cn_ref, *refs, n_pages, page):
    del pt_ref
    k_refs, v_refs, c_refs = refs[:n_pages], refs[n_pages:2 * n_pages], refs[2 * n_pages:3 * n_pages]
    o_ref, kall_ref, vall_ref, call_ref = refs[3 * n_pages:]
    sl = q_ref.shape[0]
    rows = sl * FOX_HEADS
    carry = jnp.zeros((FOX_HEADS, 1), _F32)
    for p in range(n_pages):
        kall_ref[p * page:(p + 1) * page, :] = k_refs[p][0].astype(_BF16)
        vall_ref[p * page:(p + 1) * page, :] = v_refs[p][0].astype(_BF16)
        c_page = c_refs[p][0] + carry
        call_ref[:, p * page:(p + 1) * page] = c_page
        carry = c_page[:, page - 1:page]
    head_of_lane = lax.broadcasted_iota(jnp.int32, (FOX_HEADS, B_WIDTH), 1) // FOX_DH
    head_mask = jnp.where(head_of_lane == lax.broadcasted_iota(jnp.int32, (FOX_HEADS, B_WIDTH), 0), 1.0, 0.0)
    q = q_ref[...] * (FOX_DH ** -0.5)
    q_bd = jnp.concatenate([jnp.broadcast_to(q[i:i + 1, :], (FOX_HEADS, B_WIDTH)) * head_mask for i in range(sl)],
                           axis=0).astype(_BF16)
    s_past = lax.dot_general(q_bd, kall_ref[...], (((1,), (1,)), ((), ())), preferred_element_type=_F32)
    s_past = s_past - jnp.concatenate([call_ref[...]] * sl, axis=0)
    k_new = _pad_rows_to(kn_ref[...], LANES).astype(_BF16)
    v_new = _pad_rows_to(vn_ref[...], LANES).astype(_BF16)
    s_new = lax.dot_general(q_bd, k_new, (((1,), (1,)), ((), ())), preferred_element_type=_F32)
    lane = lax.broadcasted_iota(jnp.int32, (rows, LANES), 1)
    q_idx = lax.broadcasted_iota(jnp.int32, (rows, LANES), 0) // FOX_HEADS
    s_new = s_new - jnp.concatenate([cn_ref[0] + carry] * sl, axis=0) + jnp.where(lane <= q_idx, 0.0, MASKED)
    m = jnp.maximum(jnp.max(s_past, axis=-1, keepdims=True), jnp.max(s_new, axis=-1, keepdims=True))
    p_past = jnp.exp(s_past - m)
    p_new = jnp.exp(s_new - m)
    l = jnp.sum(p_past, axis=-1, keepdims=True) + jnp.sum(p_new, axis=-1, keepdims=True)
    acc = (jnp.dot(p_past.astype(_BF16), vall_ref[...], preferred_element_type=_F32)
           + jnp.dot(p_new.astype(_BF16), v_new, preferred_element_type=_F32)) / l
    o_ref[...] = jnp.concatenate(
        [jnp.sum(acc[i * FOX_HEADS:(i + 1) * FOX_HEADS] * head_mask, axis=0, keepdims=True) for i in range(sl)], axis=0)


def _fox_sample(q, k, v, c_new, k_pool, v_pool, c_pool, page_table, *, row0, sl):
    nb, n_pages = page_table.shape
    page = k_pool.shape[1]
    assert row0 % sl == 0 and sl <= LANES
    r0 = row0 // sl
    row_spec = pl.BlockSpec((sl, B_WIDTH), lambda b, pt: (r0 + b, 0))
    past = n_pages * page
    return pl.pallas_call(
        functools.partial(_fox_sample_kernel, n_pages=n_pages, page=page),
        grid_spec=pltpu.PrefetchScalarGridSpec(
            num_scalar_prefetch=1,
            grid=(nb,),
            in_specs=[row_spec, row_spec, row_spec, pl.BlockSpec((1, FOX_HEADS, LANES), lambda b, pt: (b, 0, 0))]
                     + _page_specs((1, page, B_WIDTH), n_pages) * 2 + _page_specs((1, FOX_HEADS, page), n_pages),
            out_specs=pl.BlockSpec((sl, B_WIDTH), lambda b, pt: (b, 0)),
            scratch_shapes=[pltpu.VMEM((past, B_WIDTH), _BF16), pltpu.VMEM((past, B_WIDTH), _BF16),
                            pltpu.VMEM((FOX_HEADS, past), _F32)]),
        out_shape=jax.ShapeDtypeStruct((nb * sl, B_WIDTH), _F32),
        compiler_params=_params(1),
        name="fox_sample",
    )(page_table, q, k, v, c_new, *([k_pool.reshape(-1, page, B_WIDTH)] * n_pages),
      *([v_pool.reshape(-1, page, B_WIDTH)] * n_pages), *([c_pool] * n_pages))


def _nsa_sample_kernel(pt_ref, q_ref, qr_ref, gate_ref, ksn_ref, vsn_ref, kwn_ref, vwn_ref, kwc_ref, vwc_ref,
                       wpk_ref, w1k_ref, w2k_ref, posk_ref, wpv_ref, w1v_ref, w2v_ref, posv_ref, c2s_ref, exp_ref,
                       *refs, n_pages, page, past_len, n_sel):
    del pt_ref
    ck_refs, cv_refs = refs[:n_pages], refs[n_pages:2 * n_pages]
    sk_refs, sv_refs = refs[2 * n_pages:3 * n_pages], refs[3 * n_pages:4 * n_pages]
    o_ref, xk_ref, xv_ref, kcb_ref, vcb_ref, ksel_ref, vsel_ref, ocmp_ref, sel_ref = refs[4 * n_pages:]
    sl = q_ref.shape[0]
    scale = NSA_DH ** -0.5
    for p in range(n_pages):
        xk_ref[2 * p * page:2 * (p + 1) * page, :] = ck_refs[p][0]
        xv_ref[2 * p * page:2 * (p + 1) * page, :] = cv_refs[p][0]
        ksel_ref[p * page:(p + 1) * page, :] = sk_refs[p][0].astype(_BF16)
        vsel_ref[p * page:(p + 1) * page, :] = sv_refs[p][0].astype(_BF16)
    n_sub = past_len // CMP_STRIDE
    _compress_into(xk_ref, n_sub, wpk_ref, w1k_ref, w2k_ref, posk_ref, kcb_ref)
    _compress_into(xv_ref, n_sub, wpv_ref, w1v_ref, w2v_ref, posv_ref, vcb_ref)
    q_pos = past_len + lax.broadcasted_iota(jnp.int32, (sl, 1), 0)
    _cmp_branch(q_ref, gate_ref, kcb_ref, vcb_ref, c2s_ref, ocmp_ref, sel_ref, q_pos, n_sel)

    lane = lax.broadcasted_iota(jnp.int32, (sl, LANES), 1)
    q_idx = lax.broadcasted_iota(jnp.int32, (sl, LANES), 0)
    new_ok = lane <= q_idx
    cache_rows = kwc_ref.shape[0]
    c_lane = lax.broadcasted_iota(jnp.int32, (sl, cache_rows), 1)
    c_qidx = lax.broadcasted_iota(jnp.int32, (sl, cache_rows), 0)
    c_dist = (past_len + c_qidx) - (past_len - cache_rows + c_lane)
    win_cache_bias = jnp.where((c_dist >= 0) & (c_dist < WINDOW), 0.0, MASKED)
    win_new_bias = jnp.where(new_ok, 0.0, MASKED)
    tile4 = lambda b: jnp.concatenate([b] * NSA_GROUP, axis=0)
    nt = (((1,), (1,)), ((), ()))
    for kv in range(NSA_KV_HEADS):
        kcols = slice(kv * NSA_DH, (kv + 1) * NSA_DH)
        q = jnp.concatenate(
            [(qr_ref[:, (kv * NSA_GROUP + g) * NSA_DH:(kv * NSA_GROUP + g + 1) * NSA_DH] * scale).astype(_BF16)
             for g in range(NSA_GROUP)], axis=0)
        sel = sel_ref[:, kv * LANES:(kv + 1) * LANES]
        picked = jnp.dot(sel.astype(_BF16), exp_ref[...], preferred_element_type=_F32)
        new_blk = past_len // SEL_BLOCK
        sel_new_bias = jnp.where(new_ok & (sel[:, new_blk:new_blk + 1] > 0.5), 0.0, MASKED)
        s_a = lax.dot_general(q, ksel_ref[:, kcols], nt, preferred_element_type=_F32) + tile4((1.0 - picked) * MASKED)
        k_new = _pad_rows_to(ksn_ref[:, kcols], LANES).astype(_BF16)
        s_b = lax.dot_general(q, k_new, nt, preferred_element_type=_F32) + tile4(sel_new_bias)
        m = jnp.maximum(jnp.maximum(jnp.max(s_a, axis=-1, keepdims=True), jnp.max(s_b, axis=-1, keepdims=True)), NEG_BIG)
        p_a, p_b = jnp.exp(s_a - m), jnp.exp(s_b - m)
        l = jnp.sum(p_a, axis=-1, keepdims=True) + jnp.sum(p_b, axis=-1, keepdims=True)
        o_sel = (jnp.dot(p_a.astype(_BF16), vsel_ref[:, kcols], preferred_element_type=_F32)
                 + jnp.dot(p_b.astype(_BF16), _pad_rows_to(vsn_ref[:, kcols], LANES).astype(_BF16),
                           preferred_element_type=_F32)) / jnp.maximum(l, 1e-30)
        s_a = lax.dot_general(q, kwc_ref[:, kcols].astype(_BF16), nt, preferred_element_type=_F32) + tile4(win_cache_bias)
        k_new = _pad_rows_to(kwn_ref[:, kcols], LANES).astype(_BF16)
        s_b = lax.dot_general(q, k_new, nt, preferred_element_type=_F32) + tile4(win_new_bias)
        m = jnp.maximum(jnp.maximum(jnp.max(s_a, axis=-1, keepdims=True), jnp.max(s_b, axis=-1, keepdims=True)), NEG_BIG)
        p_a, p_b = jnp.exp(s_a - m), jnp.exp(s_b - m)
        l = jnp.sum(p_a, axis=-1, keepdims=True) + jnp.sum(p_b, axis=-1, keepdims=True)
        o_win = (jnp.dot(p_a.astype(_BF16), vwc_ref[:, kcols].astype(_BF16), preferred_element_type=_F32)
                 + jnp.dot(p_b.astype(_BF16), _pad_rows_to(vwn_ref[:, kcols], LANES).astype(_BF16),
                           preferred_element_type=_F32)) / jnp.maximum(l, 1e-30)
        for g in range(NSA_GROUP):
            h = kv * NSA_GROUP + g
            hcols = slice(h * NSA_DH, (h + 1) * NSA_DH)
            rws = slice(g * sl, (g + 1) * sl)
            o_ref[:, hcols] = (ocmp_ref[:, hcols]
                               + o_sel[rws] * gate_ref[:, NSA_HEADS + h:NSA_HEADS + h + 1]
                               + o_win[rws] * gate_ref[:, 2 * NSA_HEADS + h:2 * NSA_HEADS + h + 1])


def _nsa_sample(q, qr, gates, ks, vs, kw, vw, win_k, win_v, cmp_k_pool, cmp_v_pool, sel_k_pool, sel_v_pool,
                page_table, cmp_pos, cmp_w1, cmp_w2, *, row0, sl):
    nb, n_pages = page_table.shape
    page = cmp_k_pool.shape[1]
    past_len = n_pages * page
    cache_rows = win_k.shape[0] // nb
    assert row0 % sl == 0 and sl <= SEL_BLOCK and past_len % SEL_BLOCK == 0 and past_len // CMP_STRIDE == LANES
    assert past_len >= cache_rows
    n_sel = past_len // SEL_BLOCK + 1
    r0 = row0 // sl
    row = lambda w: pl.BlockSpec((sl, w), lambda b, pt: (r0 + b, 0))
    const = lambda b, pt: (0, 0)
    return pl.pallas_call(
        functools.partial(_nsa_sample_kernel, n_pages=n_pages, page=page, past_len=past_len, n_sel=n_sel),
        grid_spec=pltpu.PrefetchScalarGridSpec(
            num_scalar_prefetch=1,
            grid=(nb,),
            in_specs=[row(C_WIDTH), row(C_WIDTH), row(LANES)] + [row(C_KV_WIDTH)] * 4
                     + [pl.BlockSpec((cache_rows, C_KV_WIDTH), lambda b, pt: (b, 0))] * 2
                     + _cmp_weight_specs(const)
                     + [pl.BlockSpec((LANES, LANES), const), pl.BlockSpec((LANES, past_len), const)]
                     + _page_specs((1, 2 * page, LANES), n_pages) * 2
                     + _page_specs((1, page, C_KV_WIDTH), n_pages) * 2,
            out_specs=pl.BlockSpec((sl, C_WIDTH), lambda b, pt: (b, 0)),
            scratch_shapes=[pltpu.VMEM((2 * past_len, LANES), _F32), pltpu.VMEM((2 * past_len, LANES), _F32),
                            pltpu.VMEM((LANES, C_KV_WIDTH), _F32), pltpu.VMEM((LANES, C_KV_WIDTH), _F32),
                            pltpu.VMEM((past_len, C_KV_WIDTH), _BF16), pltpu.VMEM((past_len, C_KV_WIDTH), _BF16),
                            pltpu.VMEM((sl, C_WIDTH), _F32), pltpu.VMEM((sl, NSA_KV_HEADS * LANES), _F32)]),
        out_shape=jax.ShapeDtypeStruct((nb * sl, C_WIDTH), _F32),
        compiler_params=_params(1),
        name="nsa_sample",
    )(page_table, q, qr, gates, ks, vs, kw, vw, win_k, win_v, *_cmp_weights(cmp_pos, cmp_w1, cmp_w2),
      _cmp_to_sel(LANES), _sel_expand(past_len),
      *([cmp_k_pool.reshape(-1, 2 * page, LANES)] * n_pages), *([cmp_v_pool.reshape(-1, 2 * page, LANES)] * n_pages),
      *([sel_k_pool.reshape(-1, page, C_KV_WIDTH)] * n_pages), *([sel_v_pool.reshape(-1, page, C_KV_WIDTH)] * n_pages))


SUBLANES = 8


def _split_hi_lo(x):
    hi = x.astype(_BF16)
    return hi, (x - hi.astype(_F32)).astype(_BF16)


def _split_hi_mid_lo(x):
    hi = x.astype(_BF16)
    r1 = x - hi.astype(_F32)
    mid = r1.astype(_BF16)
    return hi, mid, (r1 - mid.astype(_F32)).astype(_BF16)


def _dot_two_term(a, b):
    a_hi, a_lo = _split_hi_lo(a)
    b_hi, b_lo = _split_hi_lo(b)
    dot = lambda x, y: jnp.dot(x, y, preferred_element_type=_F32)
    return dot(a_hi, b_hi) + dot(a_hi, b_lo) + dot(a_lo, b_hi)


def _unit_lower_inverse(at, n_rows):
    c = at.shape[0]
    sub = lax.broadcasted_iota(jnp.int32, (SUBLANES, c), 0)
    lane = lax.broadcasted_iota(jnp.int32, (SUBLANES, c), 1)
    slabs = [jnp.where(lane == sub + SUBLANES * r, 1.0, 0.0) for r in range(c // SUBLANES)]
    for i in range(1, n_rows):
        n_slab = (i + SUBLANES - 1) // SUBLANES
        acc = at[0:SUBLANES, i:i + 1] * slabs[0]
        for r in range(1, n_slab):
            acc = acc + at[r * SUBLANES:(r + 1) * SUBLANES, i:i + 1] * slabs[r]
        row = jnp.where(lane[0:1] == i, 1.0, 0.0) - jnp.sum(acc, axis=0, keepdims=True)
        r_i = i // SUBLANES
        slabs[r_i] = jnp.where(sub == i % SUBLANES, row, slabs[r_i])
    return slabs


def _gdn_kernel(x_ref, z_ref, gb_ref, buf_ref, s0_ref, convw_ref, alog_ref, dtb_ref, norm_ref,
                o_ref, sfin_ref, bufout_ref, state_ref, tail_ref, *, rows, n_chunks):
    c_idx = pl.program_id(1)
    C = GDN_CHUNK
    valid = rows

    @pl.when(c_idx == 0)
    def _():
        state_ref[...] = s0_ref[0]
        tail_ref[...] = jnp.zeros_like(tail_ref)
        tail_ref[SUBLANES - (GDN_CONV - 1):SUBLANES, :] = buf_ref[0]

    pad = lambda a: a if rows == C else jnp.concatenate([a, jnp.zeros((C - rows, a.shape[1]), a.dtype)], axis=0)
    x = pad(x_ref[...])
    ext = jnp.concatenate([tail_ref[...], x], axis=0)
    w = convw_ref[...]
    y = ext[SUBLANES:SUBLANES + C] * w[GDN_CONV - 1:GDN_CONV]
    for i in range(1, GDN_CONV):
        y = y + ext[SUBLANES - i:SUBLANES - i + C] * w[GDN_CONV - 1 - i:GDN_CONV - i]
    y = y * jax.nn.sigmoid(y)
    tail_ref[...] = x[C - SUBLANES:C]

    @pl.when(c_idx == n_chunks - 1)
    def _():
        bufout_ref[0] = ext[SUBLANES + valid - (GDN_CONV - 1):SUBLANES + valid]

    row_i = lax.broadcasted_iota(jnp.int32, (C, 1), 0)
    live = jnp.where(row_i < valid, 1.0, 0.0)
    gb_in = pad(gb_ref[...])
    lane = lax.broadcasted_iota(jnp.int32, (C, LANES), 1)
    t = gb_in + dtb_ref[...]
    softplus = jnp.maximum(t, 0.0) + jnp.log(1.0 + jnp.exp(-jnp.abs(t)))
    g_all = jnp.where(lane < GDN_HEADS, -jnp.exp(alog_ref[...]) * softplus, 0.0) * live
    beta_all = jnp.where((lane >= GDN_HEADS) & (lane < 2 * GDN_HEADS), jax.nn.sigmoid(gb_in), 0.0) * live
    ii = lax.broadcasted_iota(jnp.int32, (C, C), 0)
    jj = lax.broadcasted_iota(jnp.int32, (C, C), 1)
    tri = jnp.where(ii >= jj, 1.0, 0.0).astype(_BF16)
    gc_all = sum(jnp.dot(tri, part, preferred_element_type=_F32)
                 for part in _split_hi_mid_lo(g_all))
    mix = gc_all + beta_all
    eye = jnp.where(lax.broadcasted_iota(jnp.int32, (SUBLANES, LANES), 0)
                    == lax.broadcasted_iota(jnp.int32, (SUBLANES, LANES), 1), 1.0, 0.0).astype(_BF16)
    nt = (((1,), (1,)), ((), ()))
    mix_rows = sum(lax.dot_general(eye, part, nt, preferred_element_type=_F32)
                   for part in _split_hi_mid_lo(mix))
    scale = GDN_DK ** -0.5
    for h in range(GDN_HEADS):
        q = y[:, h * GDN_DK:(h + 1) * GDN_DK]
        k = y[:, A_QK + h * GDN_DK:A_QK + (h + 1) * GDN_DK]
        v = y[:, 2 * A_QK + h * GDN_DV:2 * A_QK + (h + 1) * GDN_DV] * live
        q = q * lax.rsqrt(jnp.sum(q * q, axis=-1, keepdims=True) + 1e-6) * (live * scale)
        k = k * lax.rsqrt(jnp.sum(k * k, axis=-1, keepdims=True) + 1e-6) * live
        g_col, b_col = gc_all[:, h:h + 1], beta_all[:, GDN_HEADS + h:GDN_HEADS + h + 1]
        g_row, b_row = mix_rows[h:h + 1, :], mix_rows[GDN_HEADS + h:GDN_HEADS + h + 1, :]
        k_bf = k.astype(_BF16)
        kk = lax.dot_general(k_bf, k_bf, nt, preferred_element_type=_F32)
        qk = lax.dot_general(q.astype(_BF16), k_bf, nt, preferred_element_type=_F32)
        qk = qk * jnp.exp(jnp.where(ii >= jj, g_col - g_row, NEG_BIG))
        at = b_row * kk * jnp.exp(jnp.where(jj > ii, g_row - g_col, NEG_BIG))
        t_inv = jnp.concatenate(_unit_lower_inverse(at, valid), axis=0)
        rhs = jnp.concatenate([v * b_col, k * (b_col * jnp.exp(g_col))], axis=1)
        sol = _dot_two_term(t_inv, rhs)
        u, w_ = sol[:, :GDN_DV], sol[:, GDN_DV:]
        s = state_ref[h]
        s_bf = s.astype(_BF16)
        v_new = u - jnp.dot(w_.astype(_BF16), s_bf, preferred_element_type=_F32)
        v_new_bf = v_new.astype(_BF16)
        o = (jnp.dot((q * jnp.exp(g_col)).astype(_BF16), s_bf, preferred_element_type=_F32)
             + jnp.dot(qk.astype(_BF16), v_new_bf, preferred_element_type=_F32))
        g_last = g_col[C - 1:C, :]
        k_dec = (k * jnp.exp(g_last - g_col)).astype(_BF16)
        state_ref[h] = s * jnp.exp(g_last) + lax.dot_general(k_dec, v_new_bf, (((0,), (0,)), ((), ())),
                                                             preferred_element_type=_F32)
        o = o * lax.rsqrt(jnp.mean(o * o, axis=-1, keepdims=True) + RMS_EPS) * norm_ref[...]
        zh = pad(z_ref[:, h * GDN_DV:(h + 1) * GDN_DV])
        o = o * (zh * jax.nn.sigmoid(zh))
        o_ref[:, h * GDN_DV:(h + 1) * GDN_DV] = o[:rows]

    @pl.when(c_idx == n_chunks - 1)
    def _():
        sfin_ref[0] = state_ref[...]


def _gdn(qkv, z, gates, conv_buf, s0, conv_w, a_log, dt_bias, gdn_norm, *, n_batch, seq, row0):
    rows = min(GDN_CHUNK, seq)
    assert seq % rows == 0 and row0 % rows == 0 and (rows == GDN_CHUNK or seq == rows) and rows % SUBLANES == 0
    n_chunks = seq // rows
    r0 = row0 // rows
    row_map = lambda b, c: (r0 + b * n_chunks + c, 0)
    const = lambda b, c: (0, 0)
    lane_pad = lambda vec: jnp.zeros((1, LANES), _F32).at[0, :vec.shape[0]].set(vec)
    return pl.pallas_call(
        functools.partial(_gdn_kernel, rows=rows, n_chunks=n_chunks),
        grid=(n_batch, n_chunks),
        in_specs=[pl.BlockSpec((rows, A_CONV_CH), row_map),
                  pl.BlockSpec((rows, A_WIDTH), row_map),
                  pl.BlockSpec((rows, LANES), row_map),
                  pl.BlockSpec((1, GDN_CONV - 1, A_CONV_CH), lambda b, c: (b, 0, 0)),
                  pl.BlockSpec((1, GDN_HEADS, GDN_DK, GDN_DV), lambda b, c: (b, 0, 0, 0)),
                  pl.BlockSpec((GDN_CONV, A_CONV_CH), const),
                  pl.BlockSpec((1, LANES), const),
                  pl.BlockSpec((1, LANES), const),
                  pl.BlockSpec((1, GDN_DV), const)],
        out_specs=[pl.BlockSpec((rows, A_WIDTH), lambda b, c: (b * n_chunks + c, 0)),
                   pl.BlockSpec((1, GDN_HEADS, GDN_DK, GDN_DV), lambda b, c: (b, 0, 0, 0)),
                   pl.BlockSpec((1, GDN_CONV - 1, A_CONV_CH), lambda b, c: (b, 0, 0))],
        out_shape=[jax.ShapeDtypeStruct((n_batch * seq, A_WIDTH), _F32),
                   jax.ShapeDtypeStruct((n_batch, GDN_HEADS, GDN_DK, GDN_DV), _F32),
                   jax.ShapeDtypeStruct((n_batch, GDN_CONV - 1, A_CONV_CH), _F32)],
        scratch_shapes=[pltpu.VMEM((GDN_HEADS, GDN_DK, GDN_DV), _F32), pltpu.VMEM((SUBLANES, A_CONV_CH), _F32)],
        compiler_params=_params(2),
        name="gdn",
    )(qkv, z, gates, conv_buf, s0, conv_w, lane_pad(a_log), lane_pad(dt_bias), gdn_norm.reshape(1, GDN_DV))


def _stack_layers(states):
    return [jnp.stack(a) for a in zip(*states)]


def kernel(x_prompt, x_sample, state_gdn_s, state_gdn_conv, cache_fox_k, cache_fox_v, cache_fox_logf, cache_nsa_cmp_k, cache_nsa_cmp_v, cache_nsa_sel_k, cache_nsa_sel_v, cache_nsa_win_k, cache_nsa_win_v, cache_mem_k, cache_mem_v, page_table, mem_prompt, norm_mix, norm_xattn, norm_mem, norm_ffn, norm_final, w_in_even, b_forget, gdn_conv_w, gdn_a_log, gdn_dt_bias, gdn_norm, w_out_even, w_in_odd, nsa_gate_bias, nsa_cmp_pos, nsa_cmp_w1, nsa_cmp_w2, w_out_odd, w_mem_q, w_mem_kv, w_mem_o, w_ffn_in, w_ffn_out):
    B, L, D = x_prompt.shape
    SB, SL, _ = x_sample.shape
    depth = norm_mix.shape[0]
    n_p = B * L
    n_s = SB * SL
    past_len = page_table.shape[1] * cache_fox_k.shape[2]
    pos_p = jnp.arange(L, dtype=jnp.int32)
    pos_s = past_len + jnp.arange(SL, dtype=jnp.int32)
    ffn_hidden = w_ffn_out.shape[1]
    row_tile = min(ROW_TILE, L)
    attn_tile = min(ATTN_TILE, L)
    mem_width = MEM_HEADS * MEM_DH
    keep = min(WINDOW, L)

    def rows_p(a, *shape):
        return a[:n_p].reshape(B, L, *shape)

    def rows_s(a, *shape):
        return a[n_p:].reshape(SB, SL, *shape)

    x = jnp.concatenate([x_prompt.reshape(n_p, D), x_sample.reshape(n_s, D)], axis=0)
    mem_flat = mem_prompt.reshape(B * MEM_TOKENS, D)
    even_p, even_s, odd_p, odd_s, mem_p = [], [], [], [], []
    for layer in range(depth):
        if layer % 2 == 0:
            e = layer // 2
            w = w_in_even[e]
            gate_cols = A_CONV_CH + 2 * GDN_HEADS
            w_perm = jnp.concatenate(
                [w[:, :A_CONV_CH], w[:, gate_cols:gate_cols + A_WIDTH + 3 * B_WIDTH], w[:, A_CONV_CH:gate_cols],
                 w[:, gate_cols + A_WIDTH + 3 * B_WIDTH:],
                 jnp.zeros((D, LANES - 2 * GDN_HEADS - FOX_HEADS), w.dtype)], axis=1).astype(_BF16)
            qkv_a, z, q_b, k_b, v_b, small = _rms_matmul_split(
                x, norm_mix[layer], w_perm, [A_CONV_CH, A_WIDTH, B_WIDTH, B_WIDTH, B_WIDTH, LANES], row_tile)
            logf = jax.nn.log_sigmoid(small[:, 2 * GDN_HEADS:2 * GDN_HEADS + FOX_HEADS] + b_forget[e])
            gdn_w = (gdn_conv_w[e], gdn_a_log[e], gdn_dt_bias[e], gdn_norm[e])
            c_rows = jnp.transpose(jnp.cumsum(rows_p(logf, FOX_HEADS), axis=1), (0, 2, 1))
            o_b_p = _fox_prompt(q_b, k_b, v_b, c_rows, n_batch=B, seq=L, tile=attn_tile)
            o_a_p, s_p, buf_p = _gdn(qkv_a, z, small, jnp.zeros((B, GDN_CONV - 1, A_CONV_CH), _F32),
                                     jnp.zeros((B, GDN_HEADS, GDN_DK, GDN_DV), _F32), *gdn_w,
                                     n_batch=B, seq=L, row0=0)
            even_p.append((s_p, buf_p, rows_p(k_b, FOX_HEADS, FOX_DH), rows_p(v_b, FOX_HEADS, FOX_DH),
                           rows_p(logf, FOX_HEADS)))
            c_pool = jnp.transpose(jnp.cumsum(cache_fox_logf[e], axis=1), (0, 2, 1))
            c_new = jnp.transpose(jnp.cumsum(rows_s(logf, FOX_HEADS), axis=1), (0, 2, 1))
            c_new = jnp.pad(c_new, ((0, 0), (0, 0), (0, LANES - SL)))
            o_b_s = _fox_sample(q_b, k_b, v_b, c_new, cache_fox_k[e], cache_fox_v[e], c_pool, page_table, row0=n_p, sl=SL)
            o_a_s, s_s, buf_s = _gdn(qkv_a, z, small, state_gdn_conv[e], state_gdn_s[e], *gdn_w,
                                     n_batch=SB, seq=SL, row0=n_p)
            even_s.append((s_s, buf_s, rows_s(k_b, FOX_HEADS, FOX_DH), rows_s(v_b, FOX_HEADS, FOX_DH),
                           rows_s(logf, FOX_HEADS)))
            o_a = jnp.concatenate([o_a_p, o_a_s], axis=0)
            o_b = jnp.concatenate([o_b_p, o_b_s], axis=0)
            w_out = w_out_even[e].astype(_BF16)
            x = _matmul_residual([o_a, o_b], [w_out[:A_WIDTH], w_out[A_WIDTH:]], x, row_tile)
        else:
            o = layer // 2
            q, qr, kc, vc, ks, vs, kw, vw, gates = _odd_proj(
                x, norm_mix[layer], w_in_odd[o], nsa_gate_bias[o], pos_p, pos_s, n_p, row_tile)
            cmp_w = (nsa_cmp_pos[o], nsa_cmp_w1[o], nsa_cmp_w2[o])
            o_cmp, sel = _nsa_cmp_prompt(q, gates, kc, vc, *cmp_w, n_batch=B, seq=L, tile=attn_tile)
            o_p = _nsa_selwin_prompt(qr, gates, sel, o_cmp, ks, vs, kw, vw, n_batch=B, seq=L, tile=attn_tile)
            kv4 = (NSA_KV_HEADS, NSA_DH)
            odd_p.append((rows_p(kc, *kv4), rows_p(vc, *kv4), rows_p(ks, *kv4), rows_p(vs, *kv4),
                          rows_p(kw, *kv4)[:, L - keep:], rows_p(vw, *kv4)[:, L - keep:]))
            w_buf = cache_nsa_win_k.shape[2]
            o_s = _nsa_sample(q, qr, gates, ks, vs, kw, vw,
                              cache_nsa_win_k[o].reshape(SB * w_buf, C_KV_WIDTH), cache_nsa_win_v[o].reshape(SB * w_buf, C_KV_WIDTH),
                              cache_nsa_cmp_k[o], cache_nsa_cmp_v[o], cache_nsa_sel_k[o], cache_nsa_sel_v[o],
                              page_table, *cmp_w, row0=n_p, sl=SL)
            new_wk = jnp.concatenate([cache_nsa_win_k[o], rows_s(kw, *kv4)], axis=1)[:, SL:]
            new_wv = jnp.concatenate([cache_nsa_win_v[o], rows_s(vw, *kv4)], axis=1)[:, SL:]
            odd_s.append((rows_s(kc, *kv4), rows_s(vc, *kv4), rows_s(ks, *kv4), rows_s(vs, *kv4), new_wk, new_wv))
            o_all = jnp.concatenate([o_p, o_s], axis=0)
            x = _matmul_residual([o_all], [w_out_odd[o].astype(_BF16)], x, row_tile)

        mkv = _rms_matmul(mem_flat, norm_mem[layer], w_mem_kv[layer].astype(_BF16), row_tile=min(ROW_TILE, B * MEM_TOKENS))
        mem_p.append((mkv[:, :mem_width].reshape(B, MEM_TOKENS, MEM_HEADS, MEM_DH),
                      mkv[:, mem_width:].reshape(B, MEM_TOKENS, MEM_HEADS, MEM_DH)))
        q = _rms_matmul(x, norm_xattn[layer], w_mem_q[layer].astype(_BF16), row_tile=row_tile)
        o_p = _mem_attn(q, mkv, mkv, n_batch=B, q_len=L, q_row0=0, q_tile=row_tile, k_col_block=0, v_col_block=1)
        o_s = _mem_attn(q, cache_mem_k[layer].reshape(SB * MEM_TOKENS, mem_width),
                        cache_mem_v[layer].reshape(SB * MEM_TOKENS, mem_width),
                        n_batch=SB, q_len=SL, q_row0=n_p, q_tile=SL, k_col_block=0, v_col_block=0)
        o_mem = jnp.concatenate([o_p.astype(_F32), o_s], axis=0)
        x = _matmul_residual([o_mem], [w_mem_o[layer].astype(_BF16)], x, row_tile)

        x = _ffn(x, norm_ffn[layer], w_ffn_in[layer][:, :ffn_hidden].astype(_BF16),
                 w_ffn_in[layer][:, ffn_hidden:].astype(_BF16), w_ffn_out[layer].astype(_BF16), row_tile)

    y = _rms(x, norm_final, row_tile)
    y_prompt = y[:n_p].reshape(B, L, D)
    y_sample = y[n_p:].reshape(SB, SL, D)
    p_gdn_s, p_gdn_conv, p_fox_k, p_fox_v, p_fox_logf = _stack_layers(even_p)
    s_gdn_s, s_gdn_conv, s_fox_k, s_fox_v, s_fox_logf = _stack_layers(even_s)
    p_cmp_k, p_cmp_v, p_sel_k, p_sel_v, p_win_k, p_win_v = _stack_layers(odd_p)
    s_cmp_k, s_cmp_v, s_sel_k, s_sel_v, s_win_k, s_win_v = _stack_layers(odd_s)
    p_mem_k, p_mem_v = _stack_layers(mem_p)
    return (y_prompt, y_sample,
            p_gdn_s, p_gdn_conv, p_fox_k, p_fox_v, p_fox_logf,
            p_cmp_k, p_cmp_v, p_sel_k, p_sel_v, p_win_k, p_win_v, p_mem_k, p_mem_v,
            s_gdn_s, s_gdn_conv, s_fox_k, s_fox_v, s_fox_logf,
            s_cmp_k, s_cmp_v, s_sel_k, s_sel_v, s_win_k, s_win_v)
```

```python
import functools

import jax
import jax.numpy as jnp
import numpy as np
from jax import lax
from jax.experimental import pallas as pl
from jax.experimental.pallas import tpu as pltpu

D_MODEL = 1024
RMS_EPS = 1e-6
NEG_BIG = -1e30

GDN_HEADS = 4
GDN_DK = 128
GDN_DV = 128
GDN_CONV = 4
GDN_CHUNK = 64
A_QK = GDN_HEADS * GDN_DK
A_WIDTH = GDN_HEADS * GDN_DV
A_CONV_CH = 2 * A_QK + A_WIDTH

FOX_HEADS = 8
FOX_DH = 64
B_WIDTH = FOX_HEADS * FOX_DH

NSA_HEADS = 16
NSA_KV_HEADS = 4
NSA_GROUP = NSA_HEADS // NSA_KV_HEADS
NSA_DH = 64
C_WIDTH = NSA_HEADS * NSA_DH
C_KV_WIDTH = NSA_KV_HEADS * NSA_DH
CMP_LEN = 32
CMP_STRIDE = CMP_LEN // 2
SEL_BLOCK = 64
SEL_TOP_N = 16
WINDOW = 512

ROPE_THETA = 500000.0
ROT_DIM = NSA_DH // 4

MEM_TOKENS = 256
MEM_HEADS = 4
MEM_DH = D_MODEL // MEM_HEADS

LANES = 128
VMEM_LIMIT_BYTES = 56 * 1024 * 1024
ROW_TILE = 512
ATTN_TILE = 256

_BF16 = jnp.bfloat16
_F32 = jnp.float32


def _params(n_grid_dims):
    return pltpu.CompilerParams(
        dimension_semantics=("arbitrary",) * n_grid_dims,
        vmem_limit_bytes=VMEM_LIMIT_BYTES)


def _rms_rows(x, g):
    return x * lax.rsqrt(jnp.mean(x * x, axis=-1, keepdims=True) + RMS_EPS) * g


def _rms_matmul_kernel(x_ref, g_ref, w_ref, o_ref, *, col_chunk):
    xn = _rms_rows(x_ref[...], g_ref[...]).astype(_BF16)
    width = w_ref.shape[1]
    for c0 in range(0, width, col_chunk):
        c1 = min(width, c0 + col_chunk)
        o_ref[:, c0:c1] = jnp.dot(xn, w_ref[:, c0:c1], preferred_element_type=_F32).astype(o_ref.dtype)


def _rms_matmul(x, g, w_bf16, out_dtype=_F32, row_tile=ROW_TILE):
    n, d = x.shape
    width = w_bf16.shape[1]
    assert n % row_tile == 0 and width % LANES == 0
    return pl.pallas_call(
        functools.partial(_rms_matmul_kernel, col_chunk=512),
        grid=(n // row_tile,),
        in_specs=[pl.BlockSpec((row_tile, d), lambda i: (i, 0)),
                  pl.BlockSpec((1, d), lambda i: (0, 0)),
                  pl.BlockSpec((d, width), lambda i: (0, 0))],
        out_specs=pl.BlockSpec((row_tile, width), lambda i: (i, 0)),
        out_shape=jax.ShapeDtypeStruct((n, width), out_dtype),
        compiler_params=_params(1),
        name="rms_matmul",
    )(x, g.reshape(1, d), w_bf16)


def _matmul_residual_kernel(*refs, n_in):
    r_ref, o_ref = refs[2 * n_in], refs[2 * n_in + 1]
    acc = r_ref[...]
    for a_ref, w_ref in zip(refs[:n_in], refs[n_in:2 * n_in]):
        acc = acc + jnp.dot(a_ref[...].astype(_BF16), w_ref[...], preferred_element_type=_F32)
    o_ref[...] = acc


def _matmul_residual(a_list, w_list, resid, row_tile=ROW_TILE):
    n, d = resid.shape
    n_in = len(a_list)
    assert n % row_tile == 0
    return pl.pallas_call(
        functools.partial(_matmul_residual_kernel, n_in=n_in),
        grid=(n // row_tile,),
        in_specs=[pl.BlockSpec((row_tile, a.shape[1]), lambda i: (i, 0)) for a in a_list]
                 + [pl.BlockSpec(w.shape, lambda i: (0, 0)) for w in w_list]
                 + [pl.BlockSpec((row_tile, d), lambda i: (i, 0))],
        out_specs=pl.BlockSpec((row_tile, d), lambda i: (i, 0)),
        out_shape=jax.ShapeDtypeStruct((n, d), _F32),
        input_output_aliases={2 * n_in: 0},
        compiler_params=_params(1),
        name="matmul_residual",
    )(*a_list, *w_list, resid)


def _ffn_kernel(x_ref, g_ref, wg_ref, wu_ref, wo_ref, o_ref, *, chunk):
    x = x_ref[...]
    xn = _rms_rows(x, g_ref[...]).astype(_BF16)
    o_ref[...] = x
    hidden = wg_ref.shape[1]
    for c0 in range(0, hidden, chunk):
        gate = jnp.dot(xn, wg_ref[:, c0:c0 + chunk], preferred_element_type=_F32)
        up = jnp.dot(xn, wu_ref[:, c0:c0 + chunk], preferred_element_type=_F32)
        h = (gate * jax.nn.sigmoid(gate) * up).astype(_BF16)
        o_ref[...] += jnp.dot(h, wo_ref[c0:c0 + chunk, :], preferred_element_type=_F32)


def _ffn(x, g, wg, wu, wo, row_tile=ROW_TILE):
    n, d = x.shape
    hidden = wg.shape[1]
    chunk = 256
    assert n % row_tile == 0 and hidden % chunk == 0
    const = lambda i: (0, 0)
    return pl.pallas_call(
        functools.partial(_ffn_kernel, chunk=chunk),
        grid=(n // row_tile,),
        in_specs=[pl.BlockSpec((row_tile, d), lambda i: (i, 0)),
                  pl.BlockSpec((1, d), const),
                  pl.BlockSpec((d, hidden), const, pipeline_mode=pl.Buffered(1)),
                  pl.BlockSpec((d, hidden), const, pipeline_mode=pl.Buffered(1)),
                  pl.BlockSpec((hidden, d), const, pipeline_mode=pl.Buffered(1))],
        out_specs=pl.BlockSpec((row_tile, d), lambda i: (i, 0)),
        out_shape=jax.ShapeDtypeStruct((n, d), _F32),
        input_output_aliases={0: 0},
        compiler_params=_params(1),
        name="ffn",
    )(x, g.reshape(1, d), wg, wu, wo)


def _mem_attn_kernel(q_ref, k_ref, v_ref, o_ref):
    scale = MEM_DH ** -0.5
    for h in range(MEM_HEADS):
        cols = slice(h * MEM_DH, (h + 1) * MEM_DH)
        q = q_ref[:, cols].astype(_BF16)
        k = k_ref[:, cols].astype(_BF16)
        v = v_ref[:, cols].astype(_BF16)
        s = lax.dot_general(q, k, (((1,), (1,)), ((), ())), preferred_element_type=_F32) * scale
        p = jnp.exp(s - jnp.max(s, axis=-1, keepdims=True))
        inv = 1.0 / jnp.sum(p, axis=-1, keepdims=True)
        o = jnp.dot(p.astype(_BF16), v, preferred_element_type=_F32) * inv
        o_ref[:, cols] = o.astype(o_ref.dtype)


def _mem_attn(q, k, v, *, n_batch, q_len, q_row0, q_tile, k_col_block, v_col_block, kv_block0=0):
    assert q_len % q_tile == 0 and q_row0 % q_tile == 0
    tiles = q_len // q_tile
    t0 = q_row0 // q_tile
    width = MEM_HEADS * MEM_DH
    return pl.pallas_call(
        _mem_attn_kernel,
        grid=(n_batch, tiles),
        in_specs=[pl.BlockSpec((q_tile, width), lambda b, i: (t0 + b * tiles + i, 0)),
                  pl.BlockSpec((MEM_TOKENS, width), lambda b, i: (kv_block0 + b, k_col_block)),
                  pl.BlockSpec((MEM_TOKENS, width), lambda b, i: (kv_block0 + b, v_col_block))],
        out_specs=pl.BlockSpec((q_tile, width), lambda b, i: (b * tiles + i, 0)),
        out_shape=jax.ShapeDtypeStruct((n_batch * q_len, width), _BF16 if q_tile % 16 == 0 else _F32),
        compiler_params=_params(2),
        name="mem_attn",
    )(q, k, v)


def _rms_kernel(x_ref, g_ref, o_ref):
    o_ref[...] = _rms_rows(x_ref[...], g_ref[...])


def _rms(x, g, row_tile=ROW_TILE):
    n, d = x.shape
    return pl.pallas_call(
        _rms_kernel,
        grid=(n // row_tile,),
        in_specs=[pl.BlockSpec((row_tile, d), lambda i: (i, 0)),
                  pl.BlockSpec((1, d), lambda i: (0, 0))],
        out_specs=pl.BlockSpec((row_tile, d), lambda i: (i, 0)),
        out_shape=jax.ShapeDtypeStruct((n, d), _F32),
        compiler_params=_params(1),
        name="final_rms",
    )(x, g.reshape(1, d))


def _rms_matmul_split_kernel(x_ref, g_ref, w_ref, *o_refs, col_chunk):
    xn = _rms_rows(x_ref[...], g_ref[...]).astype(_BF16)
    c0 = 0
    for o_ref in o_refs:
        width = o_ref.shape[1]
        for s0 in range(0, width, col_chunk):
            s1 = min(width, s0 + col_chunk)
            o_ref[:, s0:s1] = jnp.dot(xn, w_ref[:, c0 + s0:c0 + s1], preferred_element_type=_F32)
        c0 += width


def _rms_matmul_split(x, g, w_bf16, widths, row_tile=ROW_TILE):
    n, d = x.shape
    assert n % row_tile == 0 and sum(widths) == w_bf16.shape[1] and all(w % LANES == 0 for w in widths)
    return pl.pallas_call(
        functools.partial(_rms_matmul_split_kernel, col_chunk=512),
        grid=(n // row_tile,),
        in_specs=[pl.BlockSpec((row_tile, d), lambda i: (i, 0)),
                  pl.BlockSpec((1, d), lambda i: (0, 0)),
                  pl.BlockSpec((d, w_bf16.shape[1]), lambda i: (0, 0))],
        out_specs=[pl.BlockSpec((row_tile, w), lambda i: (i, 0)) for w in widths],
        out_shape=[jax.ShapeDtypeStruct((n, w), _F32) for w in widths],
        compiler_params=_params(1),
        name="rms_matmul_split",
    )(x, g.reshape(1, d), w_bf16)


def _rope_apply(x, cos, sin_up, sin_dn):
    width = x.shape[1]
    reps = width // cos.shape[1]
    if reps > 1:
        cos, sin_up, sin_dn = [jnp.concatenate([t] * reps, axis=1) for t in (cos, sin_up, sin_dn)]
    return x * cos + pltpu.roll(x, width - ROT_DIM // 2, 1) * sin_up + pltpu.roll(x, ROT_DIM // 2, 1) * sin_dn


def _odd_proj_kernel(x_ref, g_ref, w_ref, gb_ref, cos_ref, sup_ref, sdn_ref,
                     q_ref, qr_ref, kc_ref, vc_ref, ks_ref, vs_ref, kw_ref, vw_ref, gate_ref):
    xn = _rms_rows(x_ref[...], g_ref[...]).astype(_BF16)
    cos, sup, sdn = cos_ref[...], sup_ref[...], sdn_ref[...]
    kvw = C_KV_WIDTH
    for h0 in range(0, C_WIDTH, 512):
        q = jnp.dot(xn, w_ref[:, h0:h0 + 512], preferred_element_type=_F32)
        q_ref[:, h0:h0 + 512] = q
        qr_ref[:, h0:h0 + 512] = _rope_apply(q, cos, sup, sdn)
    c0 = C_WIDTH
    for o_ref, rot in ((kc_ref, False), (vc_ref, False), (ks_ref, True), (vs_ref, False), (kw_ref, True), (vw_ref, False)):
        y = jnp.dot(xn, w_ref[:, c0:c0 + kvw], preferred_element_type=_F32)
        o_ref[...] = _rope_apply(y, cos, sup, sdn) if rot else y
        c0 += kvw
    gl = jnp.dot(xn, w_ref[:, c0:c0 + LANES], preferred_element_type=_F32)
    gate_ref[...] = jax.nn.sigmoid(gl + gb_ref[...])


def _rope_tables(pos):
    half = ROT_DIM // 2
    inv_freq = ROPE_THETA ** (-jnp.arange(half, dtype=_F32) / half)
    ang = pos.astype(_F32)[:, None] * inv_freq[None, :]
    cos, sin = jnp.cos(ang), jnp.sin(ang)
    rest = NSA_DH - ROT_DIM
    ones = jnp.ones((pos.shape[0], rest), _F32)
    zeros = jnp.zeros((pos.shape[0], rest), _F32)
    z8 = jnp.zeros_like(sin)
    head_cos = jnp.concatenate([cos, cos, ones], axis=1)
    head_up = jnp.concatenate([-sin, z8, zeros], axis=1)
    head_dn = jnp.concatenate([z8, sin, zeros], axis=1)
    return [jnp.tile(t, (1, NSA_KV_HEADS)) for t in (head_cos, head_up, head_dn)]


def _odd_proj(x, g, w_in, gate_bias, pos_p, pos_s, n_p, row_tile=ROW_TILE):
    n, d = x.shape
    L, SL = pos_p.shape[0], pos_s.shape[0]
    assert L % row_tile == 0 and n_p % row_tile == 0 and (n - n_p) % row_tile == 0 and row_tile % SL == 0
    head = np.arange(NSA_HEADS)
    gl_cols = np.concatenate([C_WIDTH + 6 * C_KV_WIDTH + head * 3 + br for br in range(3)])
    w = jnp.concatenate([w_in[:, :C_WIDTH + 6 * C_KV_WIDTH], w_in[:, gl_cols],
                         jnp.zeros((d, LANES - 3 * NSA_HEADS), w_in.dtype)], axis=1).astype(_BF16)
    gb = jnp.concatenate([gate_bias[gl_cols - (C_WIDTH + 6 * C_KV_WIDTH)], jnp.zeros((LANES - 3 * NSA_HEADS,), _F32)]).reshape(1, LANES)
    tables = _rope_tables(jnp.concatenate([pos_p, jnp.tile(pos_s, row_tile // SL)]))
    tiles_per_seq = L // row_tile
    n_p_tiles = n_p // row_tile
    tab_map = lambda i: (jnp.where(i < n_p_tiles, i % tiles_per_seq, tiles_per_seq), 0)
    row_map = lambda i: (i, 0)
    const = lambda i: (0, 0)
    widths = [C_WIDTH, C_WIDTH] + [C_KV_WIDTH] * 6 + [LANES]
    return pl.pallas_call(
        _odd_proj_kernel,
        grid=(n // row_tile,),
        in_specs=[pl.BlockSpec((row_tile, d), row_map),
                  pl.BlockSpec((1, d), const),
                  pl.BlockSpec((d, w.shape[1]), const),
                  pl.BlockSpec((1, LANES), const)]
                 + [pl.BlockSpec((row_tile, C_KV_WIDTH), tab_map)] * 3,
        out_specs=[pl.BlockSpec((row_tile, wd), row_map) for wd in widths],
        out_shape=[jax.ShapeDtypeStruct((n, wd), _F32) for wd in widths],
        compiler_params=_params(1),
        name="odd_proj",
    )(x, g.reshape(1, d), w, gb, *tables)


MASKED = -2e30


def _softmax_step(s, v_bf16, m, l, acc):
    m_new = jnp.maximum(m, jnp.max(s, axis=-1, keepdims=True))
    alpha = jnp.exp(m - m_new)
    p = jnp.exp(s - m_new)
    l = alpha * l + jnp.sum(p, axis=-1, keepdims=True)
    acc = alpha * acc + jnp.dot(p.astype(_BF16), v_bf16, preferred_element_type=_F32)
    return m_new, l, acc


def _fox_prompt_kernel(q_ref, k_ref, v_ref, c_ref, o_ref, *, tile):
    i = pl.program_id(1)
    scale = FOX_DH ** -0.5
    row = lax.broadcasted_iota(jnp.int32, (tile, tile), 0)
    col = lax.broadcasted_iota(jnp.int32, (tile, tile), 1)
    diag_bias = jnp.where(col <= row, 0.0, MASKED)
    for h in range(FOX_HEADS):
        cols = slice(h * FOX_DH, (h + 1) * FOX_DH)
        q = (q_ref[:, cols] * scale).astype(_BF16)

        def logits(j, cols=cols, q=q, h=h):
            k0 = pl.multiple_of(j * tile, tile)
            k = k_ref[pl.ds(k0, tile), cols].astype(_BF16)
            v = v_ref[pl.ds(k0, tile), cols].astype(_BF16)
            s = lax.dot_general(q, k, (((1,), (1,)), ((), ())), preferred_element_type=_F32)
            return s - c_ref[0, h:h + 1, pl.ds(k0, tile)], v

        def body(j, carry, logits=logits):
            s, v = logits(j)
            return _softmax_step(s, v, *carry)

        init = (jnp.full((tile, 1), NEG_BIG, _F32), jnp.zeros((tile, 1), _F32), jnp.zeros((tile, FOX_DH), _F32))
        carry = lax.fori_loop(0, i, body, init)
        s, v = logits(i)
        m, l, acc = _softmax_step(s + diag_bias, v, *carry)
        o_ref[:, cols] = acc / jnp.maximum(l, 1e-30)


def _fox_prompt(q, k, v, c_rows, *, n_batch, seq, tile):
    assert seq % tile == 0
    tiles = seq // tile
    return pl.pallas_call(
        functools.partial(_fox_prompt_kernel, tile=tile),
        grid=(n_batch, tiles),
        in_specs=[pl.BlockSpec((tile, B_WIDTH), lambda b, i: (b * tiles + i, 0)),
                  pl.BlockSpec((seq, B_WIDTH), lambda b, i: (b, 0)),
                  pl.BlockSpec((seq, B_WIDTH), lambda b, i: (b, 0)),
                  pl.BlockSpec((1, FOX_HEADS, seq), lambda b, i: (b, 0, 0))],
        out_specs=pl.BlockSpec((tile, B_WIDTH), lambda b, i: (b * tiles + i, 0)),
        out_shape=jax.ShapeDtypeStruct((n_batch * seq, B_WIDTH), _F32),
        compiler_params=_params(2),
        name="fox_prompt",
    )(q, k, v, c_rows)


def _split3_dot(x, w_bf16):
    hi = x.astype(_BF16)
    r1 = x - hi.astype(_F32)
    mid = r1.astype(_BF16)
    lo = (r1 - mid.astype(_F32)).astype(_BF16)
    dot = lambda a: jnp.dot(a, w_bf16, preferred_element_type=_F32)
    return dot(hi) + dot(mid) + dot(lo)


def _compress_into(x2_ref, n_sub, wpair_ref, w1_ref, w2_ref, pos_ref, out_ref):
    pair_rows = []
    for p in range(NSA_KV_HEADS // 2):
        pieces = [x2_ref[pl.ds(2 * r + p, n_sub, stride=2 * CMP_STRIDE), :] for r in range(CMP_STRIDE)]
        pair_rows.append(jnp.concatenate(pieces, axis=1))
    x = jnp.concatenate(pair_rows, axis=0).astype(_BF16)
    both = jnp.dot(x, wpair_ref[...], preferred_element_type=_F32)
    bias = jnp.dot(pos_ref[...].astype(_BF16), w1_ref[...], preferred_element_type=_F32)
    hid_w = 2 * NSA_DH
    for kv in range(NSA_KV_HEADS):
        p, s = divmod(kv, 2)
        rows = slice(p * n_sub, (p + 1) * n_sub)
        first = both[rows, s * hid_w:(s + 1) * hid_w]
        second = both[rows, (2 + s) * hid_w:(3 + s) * hid_w]
        pre = first + pltpu.roll(second, n_sub - 1, 0) + bias
        hid = (pre * jax.nn.sigmoid(pre)).astype(_BF16)
        out_ref[0:n_sub, kv * NSA_DH:(kv + 1) * NSA_DH] = jnp.dot(hid, w2_ref[...], preferred_element_type=_F32)


def _cmp_branch(q_ref, gate_ref, kcb_ref, vcb_ref, c2s_ref, o_ref, sel_ref, q_pos, n_sel):
    rows = q_ref.shape[0]
    nb = kcb_ref.shape[0]
    scale = NSA_DH ** -0.5
    blk_n = lax.broadcasted_iota(jnp.int32, (rows, nb), 1)
    cbias = jnp.where(blk_n * CMP_STRIDE + (CMP_LEN - 1) <= q_pos, 0.0, MASKED)
    lane = lax.broadcasted_iota(jnp.int32, (rows, LANES), 1)
    cur = q_pos // SEL_BLOCK
    causal = (lane <= cur) & (lane < n_sel)
    forced = (lane == 0) | (lane == cur) | (lane == cur - 1)
    for kv in range(NSA_KV_HEADS):
        kcols = slice(kv * NSA_DH, (kv + 1) * NSA_DH)
        kcb = kcb_ref[:, kcols].astype(_BF16)
        vcb = vcb_ref[:, kcols].astype(_BF16)
        p_sum = jnp.zeros((rows, nb), _F32)
        for g in range(NSA_GROUP):
            h = kv * NSA_GROUP + g
            hcols = slice(h * NSA_DH, (h + 1) * NSA_DH)
            q = (q_ref[:, hcols] * scale).astype(_BF16)
            s = lax.dot_general(q, kcb, (((1,), (1,)), ((), ())), preferred_element_type=_F32) + cbias
            m = jnp.maximum(jnp.max(s, axis=-1, keepdims=True), NEG_BIG)
            p = jnp.exp(s - m)
            p = p / jnp.maximum(jnp.sum(p, axis=-1, keepdims=True), 1e-30)
            p_sum = p_sum + p
            o = jnp.dot(p.astype(_BF16), vcb, preferred_element_type=_F32)
            o_ref[:, hcols] = o * gate_ref[:, h:h + 1]
        imp = _split3_dot(p_sum, c2s_ref[...])
        imp = jnp.where(causal, jnp.where(forced, -NEG_BIG, imp), NEG_BIG)
        ahead = jnp.zeros((rows, LANES), _F32)
        for j in range(n_sel):
            cj = imp[:, j:j + 1]
            after_j = jnp.where(lane[0:1] > j, 1.0, 0.0)
            ahead = ahead + jnp.where(cj > imp, 1.0, 0.0) + jnp.where(cj == imp, after_j, 0.0)
        sel = jnp.where((ahead < float(min(SEL_TOP_N, n_sel))) & (imp > 0.5 * NEG_BIG), 1.0, 0.0)
        sel_ref[:, kv * LANES:(kv + 1) * LANES] = sel


def _cmp_to_sel(nb):
    c_start = np.arange(nb)[:, None] * CMP_STRIDE
    s_start = np.arange(LANES)[None, :] * SEL_BLOCK
    shared = np.minimum(c_start + CMP_LEN, s_start + SEL_BLOCK) - np.maximum(c_start, s_start)
    return jnp.asarray(np.clip(shared, 0, None).astype(np.float32) / CMP_LEN, _BF16)


def _sel_expand(n_keys):
    return jnp.asarray((np.arange(n_keys)[None, :] // SEL_BLOCK == np.arange(LANES)[:, None]).astype(np.float32), _BF16)


def _nsa_cmp_prompt_kernel(q_ref, gate_ref, kc_ref, vc_ref, wpk_ref, w1k_ref, w2k_ref, posk_ref,
                           wpv_ref, w1v_ref, w2v_ref, posv_ref, c2s_ref, o_ref, sel_ref, kcb_ref, vcb_ref,
                           *, tile, n_sub, n_sel):
    i = pl.program_id(1)

    @pl.when(i == 0)
    def _():
        kcb_ref[...] = jnp.zeros_like(kcb_ref)
        vcb_ref[...] = jnp.zeros_like(vcb_ref)
        _compress_into(kc_ref, n_sub, wpk_ref, w1k_ref, w2k_ref, posk_ref, kcb_ref)
        _compress_into(vc_ref, n_sub, wpv_ref, w1v_ref, w2v_ref, posv_ref, vcb_ref)

    q_pos = i * tile + lax.broadcasted_iota(jnp.int32, (tile, 1), 0)
    _cmp_branch(q_ref, gate_ref, kcb_ref, vcb_ref, c2s_ref, o_ref, sel_ref, q_pos, n_sel)


def _cmp_weights(cmp_pos, cmp_w1, cmp_w2):
    out = []
    eye2 = jnp.eye(2, dtype=_F32)
    for idx in range(2):
        w1r = cmp_w1[idx].reshape(CMP_LEN, NSA_DH, 2 * NSA_DH)
        halves = [jnp.einsum('rdj,st->rsdtj', w1r[h * CMP_STRIDE:(h + 1) * CMP_STRIDE], eye2).reshape(
            CMP_STRIDE * 2 * NSA_DH, 4 * NSA_DH) for h in range(2)]
        out += [jnp.concatenate(halves, axis=1).astype(_BF16), cmp_w1[idx].astype(_BF16), cmp_w2[idx].astype(_BF16),
                cmp_pos[idx].reshape(1, CMP_LEN * NSA_DH)]
    return out


def _cmp_weight_specs(const):
    return [pl.BlockSpec((CMP_STRIDE * 2 * NSA_DH, 8 * NSA_DH), const),
            pl.BlockSpec((CMP_LEN * NSA_DH, 2 * NSA_DH), const),
            pl.BlockSpec((2 * NSA_DH, NSA_DH), const),
            pl.BlockSpec((1, CMP_LEN * NSA_DH), const)] * 2


def _nsa_cmp_prompt(q, gates, kc, vc, cmp_pos, cmp_w1, cmp_w2, *, n_batch, seq, tile):
    assert seq % tile == 0 and seq % SEL_BLOCK == 0
    tiles = seq // tile
    n_sub = seq // CMP_STRIDE
    nb = -(-n_sub // LANES) * LANES
    n_sel = seq // SEL_BLOCK
    row_map = lambda b, i: (b * tiles + i, 0)
    seq_map = lambda b, i: (b, 0)
    const = lambda b, i: (0, 0)
    return pl.pallas_call(
        functools.partial(_nsa_cmp_prompt_kernel, tile=tile, n_sub=n_sub, n_sel=n_sel),
        grid=(n_batch, tiles),
        in_specs=[pl.BlockSpec((tile, C_WIDTH), row_map),
                  pl.BlockSpec((tile, LANES), row_map),
                  pl.BlockSpec((2 * seq, LANES), seq_map),
                  pl.BlockSpec((2 * seq, LANES), seq_map)] + _cmp_weight_specs(const)
                 + [pl.BlockSpec((nb, LANES), const)],
        out_specs=[pl.BlockSpec((tile, C_WIDTH), row_map),
                   pl.BlockSpec((tile, NSA_KV_HEADS * LANES), row_map)],
        out_shape=[jax.ShapeDtypeStruct((n_batch * seq, C_WIDTH), _F32),
                   jax.ShapeDtypeStruct((n_batch * seq, NSA_KV_HEADS * LANES), _F32)],
        scratch_shapes=[pltpu.VMEM((nb, C_KV_WIDTH), _F32), pltpu.VMEM((nb, C_KV_WIDTH), _F32)],
        compiler_params=_params(2),
        name="nsa_cmp_prompt",
    )(q, gates, kc.reshape(-1, LANES), vc.reshape(-1, LANES), *_cmp_weights(cmp_pos, cmp_w1, cmp_w2), _cmp_to_sel(nb))


def _nsa_selwin_prompt_kernel(qr_ref, gate_ref, sel_ref, ocmp_ref, ks_ref, vs_ref, kw_ref, vw_ref, exp_ref,
                              o_ref, selbias_ref, *, tile):
    i = pl.program_id(1)
    scale = NSA_DH ** -0.5
    q0 = i * tile
    row = lax.broadcasted_iota(jnp.int32, (tile, tile), 0) + q0
    col = lax.broadcasted_iota(jnp.int32, (tile, tile), 1)
    rows4 = NSA_GROUP * tile
    init = (jnp.full((rows4, 1), NEG_BIG, _F32), jnp.zeros((rows4, 1), _F32), jnp.zeros((rows4, NSA_DH), _F32))
    first_win = jnp.maximum(i - (WINDOW + tile - 1) // tile, 0)
    for kv in range(NSA_KV_HEADS):
        kcols = slice(kv * NSA_DH, (kv + 1) * NSA_DH)
        picked = jnp.dot(sel_ref[:, kv * LANES:(kv + 1) * LANES].astype(_BF16), exp_ref[...], preferred_element_type=_F32)
        selbias_ref[...] = (1.0 - picked) * MASKED
        q = jnp.concatenate(
            [(qr_ref[:, (kv * NSA_GROUP + g) * NSA_DH:(kv * NSA_GROUP + g + 1) * NSA_DH] * scale).astype(_BF16)
             for g in range(NSA_GROUP)], axis=0)

        def sel_body(j, carry, q=q, kcols=kcols):
            k0 = pl.multiple_of(j * tile, tile)
            k = ks_ref[pl.ds(k0, tile), kcols].astype(_BF16)
            v = vs_ref[pl.ds(k0, tile), kcols].astype(_BF16)
            bias = selbias_ref[:, pl.ds(k0, tile)] + jnp.where(col + k0 <= row, 0.0, MASKED)
            s = lax.dot_general(q, k, (((1,), (1,)), ((), ())), preferred_element_type=_F32)
            return _softmax_step(s + jnp.concatenate([bias] * NSA_GROUP, axis=0), v, *carry)

        def win_body(j, carry, q=q, kcols=kcols):
            k0 = pl.multiple_of(j * tile, tile)
            k = kw_ref[pl.ds(k0, tile), kcols].astype(_BF16)
            v = vw_ref[pl.ds(k0, tile), kcols].astype(_BF16)
            dist = row - (col + k0)
            bias = jnp.where((dist >= 0) & (dist < WINDOW), 0.0, MASKED)
            s = lax.dot_general(q, k, (((1,), (1,)), ((), ())), preferred_element_type=_F32)
            return _softmax_step(s + jnp.concatenate([bias] * NSA_GROUP, axis=0), v, *carry)

        _, l_s, acc_s = lax.fori_loop(0, i + 1, sel_body, init)
        _, l_w, acc_w = lax.fori_loop(first_win, i + 1, win_body, init)
        o_s = acc_s / jnp.maximum(l_s, 1e-30)
        o_w = acc_w / jnp.maximum(l_w, 1e-30)
        for g in range(NSA_GROUP):
            h = kv * NSA_GROUP + g
            hcols = slice(h * NSA_DH, (h + 1) * NSA_DH)
            rws = slice(g * tile, (g + 1) * tile)
            o_ref[:, hcols] = (ocmp_ref[:, hcols]
                               + o_s[rws] * gate_ref[:, NSA_HEADS + h:NSA_HEADS + h + 1]
                               + o_w[rws] * gate_ref[:, 2 * NSA_HEADS + h:2 * NSA_HEADS + h + 1])


def _nsa_selwin_prompt(qr, gates, sel, o_cmp, ks, vs, kw, vw, *, n_batch, seq, tile):
    assert seq % tile == 0
    tiles = seq // tile
    row_map = lambda b, i: (b * tiles + i, 0)
    seq_map = lambda b, i: (b, 0)
    return pl.pallas_call(
        functools.partial(_nsa_selwin_prompt_kernel, tile=tile),
        grid=(n_batch, tiles),
        in_specs=[pl.BlockSpec((tile, C_WIDTH), row_map),
                  pl.BlockSpec((tile, LANES), row_map),
                  pl.BlockSpec((tile, NSA_KV_HEADS * LANES), row_map),
                  pl.BlockSpec((tile, C_WIDTH), row_map)]
                 + [pl.BlockSpec((seq, C_KV_WIDTH), seq_map)] * 4
                 + [pl.BlockSpec((LANES, seq), lambda b, i: (0, 0))],
        out_specs=pl.BlockSpec((tile, C_WIDTH), row_map),
        out_shape=jax.ShapeDtypeStruct((n_batch * seq, C_WIDTH), _F32),
        scratch_shapes=[pltpu.VMEM((tile, seq), _F32)],
        compiler_params=_params(2),
        name="nsa_selwin_prompt",
    )(qr, gates, sel, o_cmp, ks, vs, kw, vw, _sel_expand(seq))


def _page_specs(block, n_pages, page0=0):
    return [pl.BlockSpec(block, functools.partial(lambda b, pt, p: (page0 + pt[b, p],) + (0,) * (len(block) - 1), p=p))
            for p in range(n_pages)]


def _pad_rows_to(x, rows):
    return jnp.concatenate([x, jnp.zeros((rows - x.shape[0], x.shape[1]), x.dtype)], axis=0)


def _pages_by_head(pool):
    e, n_pool, page, heads, dh = pool.shape
    return jnp.transpose(pool, (0, 1, 3, 4, 2)).reshape(e * n_pool, heads, dh, page)


_NN = (((1,), (0,)), ((), ()))
_NT = (((1,), (1,)), ((), ()))


def _fox_sample_kernel(pt_ref, q_ref, kn_ref, vn_ref, cn_ref, *refs, n_pages, page):
    del pt_ref
    k_refs, v_refs, c_refs = refs[:n_pages], refs[n_pages:2 * n_pages], refs[2 * n_pages:3 * n_pages]
    o_ref, kall_ref, vall_ref, call_ref = refs[3 * n_pages:]
    sl = q_ref.shape[0]
    carry = jnp.zeros((FOX_HEADS, 1), _F32)
    for p in range(n_pages):
        kall_ref[:, :, p * page:(p + 1) * page] = k_refs[p][0].astype(_BF16)
        vall_ref[:, :, p * page:(p + 1) * page] = v_refs[p][0].astype(_BF16)
        c_page = c_refs[p][0] + carry
        call_ref[:, p * page:(p + 1) * page] = c_page
        carry = c_page[:, page - 1:page]
    c_new = cn_ref[0] + carry
    lane = lax.broadcasted_iota(jnp.int32, (sl, LANES), 1)
    q_idx = lax.broadcasted_iota(jnp.int32, (sl, LANES), 0)
    new_bias = jnp.where(lane <= q_idx, 0.0, MASKED)
    scale = FOX_DH ** -0.5
    for h in range(FOX_HEADS):
        cols = slice(h * FOX_DH, (h + 1) * FOX_DH)
        q = (q_ref[:, cols] * scale).astype(_BF16)
        s_past = lax.dot_general(q, kall_ref[h], _NN, preferred_element_type=_F32) - call_ref[h:h + 1, :]
        k_new = _pad_rows_to(kn_ref[:, cols], LANES).astype(_BF16)
        v_new = _pad_rows_to(vn_ref[:, cols], LANES).astype(_BF16)
        s_new = lax.dot_general(q, k_new, _NT, preferred_element_type=_F32) - c_new[h:h + 1, :] + new_bias
        m = jnp.maximum(jnp.max(s_past, axis=-1, keepdims=True), jnp.max(s_new, axis=-1, keepdims=True))
        p_past = jnp.exp(s_past - m)
        p_new = jnp.exp(s_new - m)
        l = jnp.sum(p_past, axis=-1, keepdims=True) + jnp.sum(p_new, axis=-1, keepdims=True)
        acc = (lax.dot_general(p_past.astype(_BF16), vall_ref[h], _NT, preferred_element_type=_F32)
               + jnp.dot(p_new.astype(_BF16), v_new, preferred_element_type=_F32))
        o_ref[:, cols] = acc / l


def _fox_sample(q, k, v, c_new, k_pages, v_pages, c_pool, page_table, *, row0, sl, page0):
    nb, n_pages = page_table.shape
    page = k_pages.shape[3]
    assert row0 % sl == 0 and sl <= LANES
    r0 = row0 // sl
    row_spec = pl.BlockSpec((sl, B_WIDTH), lambda b, pt: (r0 + b, 0))
    past = n_pages * page
    return pl.pallas_call(
        functools.partial(_fox_sample_kernel, n_pages=n_pages, page=page),
        grid_spec=pltpu.PrefetchScalarGridSpec(
            num_scalar_prefetch=1,
            grid=(nb,),
            in_specs=[row_spec, row_spec, row_spec, pl.BlockSpec((1, FOX_HEADS, LANES), lambda b, pt: (b, 0, 0))]
                     + _page_specs((1, FOX_HEADS, FOX_DH, page), n_pages, page0) * 2
                     + _page_specs((1, FOX_HEADS, page), n_pages, page0),
            out_specs=pl.BlockSpec((sl, B_WIDTH), lambda b, pt: (b, 0)),
            scratch_shapes=[pltpu.VMEM((FOX_HEADS, FOX_DH, past), _BF16), pltpu.VMEM((FOX_HEADS, FOX_DH, past), _BF16),
                            pltpu.VMEM((FOX_HEADS, past), _F32)]),
        out_shape=jax.ShapeDtypeStruct((nb * sl, B_WIDTH), _F32),
        compiler_params=_params(1),
        name="fox_sample",
    )(page_table, q, k, v, c_new, *([k_pages] * n_pages), *([v_pages] * n_pages), *([c_pool] * n_pages))


def _nsa_sample_kernel(pt_ref, q_ref, qr_ref, gate_ref, ksn_ref, vsn_ref, kwn_ref, vwn_ref, kwc_ref, vwc_ref,
                       wpk_ref, w1k_ref, w2k_ref, posk_ref, wpv_ref, w1v_ref, w2v_ref, posv_ref, c2s_ref, exp_ref,
                       *refs, n_pages, page, past_len, n_sel):
    del pt_ref
    ck_refs, cv_refs = refs[:n_pages], refs[n_pages:2 * n_pages]
    sk_refs, sv_refs = refs[2 * n_pages:3 * n_pages], refs[3 * n_pages:4 * n_pages]
    o_ref, xk_ref, xv_ref, kcb_ref, vcb_ref, ksel_ref, vsel_ref, ocmp_ref, sel_ref = refs[4 * n_pages:]
    sl = q_ref.shape[0]
    scale = NSA_DH ** -0.5
    for p in range(n_pages):
        xk_ref[2 * p * page:2 * (p + 1) * page, :] = ck_refs[p][0]
        xv_ref[2 * p * page:2 * (p + 1) * page, :] = cv_refs[p][0]
        ksel_ref[:, :, p * page:(p + 1) * page] = sk_refs[p][0].astype(_BF16)
        vsel_ref[:, :, p * page:(p + 1) * page] = sv_refs[p][0].astype(_BF16)
    n_sub = past_len // CMP_STRIDE
    _compress_into(xk_ref, n_sub, wpk_ref, w1k_ref, w2k_ref, posk_ref, kcb_ref)
    _compress_into(xv_ref, n_sub, wpv_ref, w1v_ref, w2v_ref, posv_ref, vcb_ref)
    q_pos = past_len + lax.broadcasted_iota(jnp.int32, (sl, 1), 0)
    _cmp_branch(q_ref, gate_ref, kcb_ref, vcb_ref, c2s_ref, ocmp_ref, sel_ref, q_pos, n_sel)

    lane = lax.broadcasted_iota(jnp.int32, (sl, LANES), 1)
    q_idx = lax.broadcasted_iota(jnp.int32, (sl, LANES), 0)
    new_ok = lane <= q_idx
    cache_rows = kwc_ref.shape[0]
    c_lane = lax.broadcasted_iota(jnp.int32, (sl, cache_rows), 1)
    c_qidx = lax.broadcasted_iota(jnp.int32, (sl, cache_rows), 0)
    c_dist = (past_len + c_qidx) - (past_len - cache_rows + c_lane)
    win_cache_bias = jnp.where((c_dist >= 0) & (c_dist < WINDOW), 0.0, MASKED)
    win_new_bias = jnp.where(new_ok, 0.0, MASKED)
    tile4 = lambda b: jnp.concatenate([b] * NSA_GROUP, axis=0)
    nt = (((1,), (1,)), ((), ()))
    for kv in range(NSA_KV_HEADS):
        kcols = slice(kv * NSA_DH, (kv + 1) * NSA_DH)
        q = jnp.concatenate(
            [(qr_ref[:, (kv * NSA_GROUP + g) * NSA_DH:(kv * NSA_GROUP + g + 1) * NSA_DH] * scale).astype(_BF16)
             for g in range(NSA_GROUP)], axis=0)
        sel = sel_ref[:, kv * LANES:(kv + 1) * LANES]
        picked = jnp.dot(sel.astype(_BF16), exp_ref[...], preferred_element_type=_F32)
        new_blk = past_len // SEL_BLOCK
        sel_new_bias = jnp.where(new_ok & (sel[:, new_blk:new_blk + 1] > 0.5), 0.0, MASKED)
        s_a = lax.dot_general(q, ksel_ref[kv], _NN, preferred_element_type=_F32) + tile4((1.0 - picked) * MASKED)
        k_new = _pad_rows_to(ksn_ref[:, kcols], LANES).astype(_BF16)
        s_b = lax.dot_general(q, k_new, nt, preferred_element_type=_F32) + tile4(sel_new_bias)
        m = jnp.maximum(jnp.maximum(jnp.max(s_a, axis=-1, keepdims=True), jnp.max(s_b, axis=-1, keepdims=True)), NEG_BIG)
        p_a, p_b = jnp.exp(s_a - m), jnp.exp(s_b - m)
        l = jnp.sum(p_a, axis=-1, keepdims=True) + jnp.sum(p_b, axis=-1, keepdims=True)
        o_sel = (lax.dot_general(p_a.astype(_BF16), vsel_ref[kv], _NT, preferred_element_type=_F32)
                 + jnp.dot(p_b.astype(_BF16), _pad_rows_to(vsn_ref[:, kcols], LANES).astype(_BF16),
                           preferred_element_type=_F32)) / jnp.maximum(l, 1e-30)
        s_a = lax.dot_general(q, kwc_ref[:, kcols].astype(_BF16), nt, preferred_element_type=_F32) + tile4(win_cache_bias)
        k_new = _pad_rows_to(kwn_ref[:, kcols], LANES).astype(_BF16)
        s_b = lax.dot_general(q, k_new, nt, preferred_element_type=_F32) + tile4(win_new_bias)
        m = jnp.maximum(jnp.maximum(jnp.max(s_a, axis=-1, keepdims=True), jnp.max(s_b, axis=-1, keepdims=True)), NEG_BIG)
        p_a, p_b = jnp.exp(s_a - m), jnp.exp(s_b - m)
        l = jnp.sum(p_a, axis=-1, keepdims=True) + jnp.sum(p_b, axis=-1, keepdims=True)
        o_win = (jnp.dot(p_a.astype(_BF16), vwc_ref[:, kcols].astype(_BF16), preferred_element_type=_F32)
                 + jnp.dot(p_b.astype(_BF16), _pad_rows_to(vwn_ref[:, kcols], LANES).astype(_BF16),
                           preferred_element_type=_F32)) / jnp.maximum(l, 1e-30)
        for g in range(NSA_GROUP):
            h = kv * NSA_GROUP + g
            hcols = slice(h * NSA_DH, (h + 1) * NSA_DH)
            rws = slice(g * sl, (g + 1) * sl)
            o_ref[:, hcols] = (ocmp_ref[:, hcols]
                               + o_sel[rws] * gate_ref[:, NSA_HEADS + h:NSA_HEADS + h + 1]
                               + o_win[rws] * gate_ref[:, 2 * NSA_HEADS + h:2 * NSA_HEADS + h + 1])


def _nsa_sample(q, qr, gates, ks, vs, kw, vw, win_k, win_v, cmp_k_rows, cmp_v_rows, sel_k_pages, sel_v_pages,
                page_table, cmp_pos, cmp_w1, cmp_w2, *, row0, sl, page0, win_block0, w_buf):
    nb, n_pages = page_table.shape
    page = sel_k_pages.shape[3]
    past_len = n_pages * page
    cache_rows = w_buf
    assert row0 % sl == 0 and sl <= SEL_BLOCK and past_len % SEL_BLOCK == 0 and past_len // CMP_STRIDE == LANES
    assert past_len >= cache_rows and win_k.shape[0] % cache_rows == 0
    n_sel = past_len // SEL_BLOCK + 1
    r0 = row0 // sl
    row = lambda w: pl.BlockSpec((sl, w), lambda b, pt: (r0 + b, 0))
    const = lambda b, pt: (0, 0)
    return pl.pallas_call(
        functools.partial(_nsa_sample_kernel, n_pages=n_pages, page=page, past_len=past_len, n_sel=n_sel),
        grid_spec=pltpu.PrefetchScalarGridSpec(
            num_scalar_prefetch=1,
            grid=(nb,),
            in_specs=[row(C_WIDTH), row(C_WIDTH), row(LANES)] + [row(C_KV_WIDTH)] * 4
                     + [pl.BlockSpec((cache_rows, C_KV_WIDTH), lambda b, pt: (win_block0 + b, 0))] * 2
                     + _cmp_weight_specs(const)
                     + [pl.BlockSpec((LANES, LANES), const), pl.BlockSpec((LANES, past_len), const)]
                     + _page_specs((1, 2 * page, LANES), n_pages, page0) * 2
                     + _page_specs((1, NSA_KV_HEADS, NSA_DH, page), n_pages, page0) * 2,
            out_specs=pl.BlockSpec((sl, C_WIDTH), lambda b, pt: (b, 0)),
            scratch_shapes=[pltpu.VMEM((2 * past_len, LANES), _F32), pltpu.VMEM((2 * past_len, LANES), _F32),
                            pltpu.VMEM((LANES, C_KV_WIDTH), _F32), pltpu.VMEM((LANES, C_KV_WIDTH), _F32),
                            pltpu.VMEM((NSA_KV_HEADS, NSA_DH, past_len), _BF16),
                            pltpu.VMEM((NSA_KV_HEADS, NSA_DH, past_len), _BF16),
                            pltpu.VMEM((sl, C_WIDTH), _F32), pltpu.VMEM((sl, NSA_KV_HEADS * LANES), _F32)]),
        out_shape=jax.ShapeDtypeStruct((nb * sl, C_WIDTH), _F32),
        compiler_params=_params(1),
        name="nsa_sample",
    )(page_table, q, qr, gates, ks, vs, kw, vw, win_k, win_v, *_cmp_weights(cmp_pos, cmp_w1, cmp_w2),
      _cmp_to_sel(LANES), _sel_expand(past_len),
      *([cmp_k_rows] * n_pages), *([cmp_v_rows] * n_pages), *([sel_k_pages] * n_pages), *([sel_v_pages] * n_pages))


SUBLANES = 8


def _split_hi_lo(x):
    hi = x.astype(_BF16)
    return hi, (x - hi.astype(_F32)).astype(_BF16)


def _split_hi_mid_lo(x):
    hi = x.astype(_BF16)
    r1 = x - hi.astype(_F32)
    mid = r1.astype(_BF16)
    return hi, mid, (r1 - mid.astype(_F32)).astype(_BF16)


def _dot_two_term(a, b):
    a_hi, a_lo = _split_hi_lo(a)
    b_hi, b_lo = _split_hi_lo(b)
    dot = lambda x, y: jnp.dot(x, y, preferred_element_type=_F32)
    return dot(a_hi, b_hi) + dot(a_hi, b_lo) + dot(a_lo, b_hi)


def _unit_lower_inverse(at, n_rows):
    c = at.shape[0]
    sub = lax.broadcasted_iota(jnp.int32, (SUBLANES, c), 0)
    lane = lax.broadcasted_iota(jnp.int32, (SUBLANES, c), 1)
    slabs = [jnp.where(lane == sub + SUBLANES * r, 1.0, 0.0) for r in range(c // SUBLANES)]
    for i in range(1, n_rows):
        n_slab = (i + SUBLANES - 1) // SUBLANES
        acc = at[0:SUBLANES, i:i + 1] * slabs[0]
        for r in range(1, n_slab):
            acc = acc + at[r * SUBLANES:(r + 1) * SUBLANES, i:i + 1] * slabs[r]
        row = jnp.where(lane[0:1] == i, 1.0, 0.0) - jnp.sum(acc, axis=0, keepdims=True)
        r_i = i // SUBLANES
        slabs[r_i] = jnp.where(sub == i % SUBLANES, row, slabs[r_i])
    return slabs


def _gdn_kernel(x_ref, z_ref, gb_ref, buf_ref, s0_ref, convw_ref, alog_ref, dtb_ref, norm_ref,
                o_ref, sfin_ref, bufout_ref, state_ref, tail_ref, *, rows, n_chunks):
    c_idx = pl.program_id(1)
    C = GDN_CHUNK
    valid = rows

    @pl.when(c_idx == 0)
    def _():
        state_ref[...] = s0_ref[0]
        tail_ref[...] = jnp.zeros_like(tail_ref)
        tail_ref[SUBLANES - (GDN_CONV - 1):SUBLANES, :] = buf_ref[0]

    pad = lambda a: a if rows == C else jnp.concatenate([a, jnp.zeros((C - rows, a.shape[1]), a.dtype)], axis=0)
    x = pad(x_ref[...])
    ext = jnp.concatenate([tail_ref[...], x], axis=0)
    w = convw_ref[...]
    y = ext[SUBLANES:SUBLANES + C] * w[GDN_CONV - 1:GDN_CONV]
    for i in range(1, GDN_CONV):
        y = y + ext[SUBLANES - i:SUBLANES - i + C] * w[GDN_CONV - 1 - i:GDN_CONV - i]
    y = y * jax.nn.sigmoid(y)
    tail_ref[...] = x[C - SUBLANES:C]

    @pl.when(c_idx == n_chunks - 1)
    def _():
        bufout_ref[0] = ext[SUBLANES + valid - (GDN_CONV - 1):SUBLANES + valid]

    row_i = lax.broadcasted_iota(jnp.int32, (C, 1), 0)
    live = jnp.where(row_i < valid, 1.0, 0.0)
    gb_in = pad(gb_ref[...])
    lane = lax.broadcasted_iota(jnp.int32, (C, LANES), 1)
    t = gb_in + dtb_ref[...]
    softplus = jnp.maximum(t, 0.0) + jnp.log(1.0 + jnp.exp(-jnp.abs(t)))
    g_all = jnp.where(lane < GDN_HEADS, -jnp.exp(alog_ref[...]) * softplus, 0.0) * live
    beta_all = jnp.where((lane >= GDN_HEADS) & (lane < 2 * GDN_HEADS), jax.nn.sigmoid(gb_in), 0.0) * live
    ii = lax.broadcasted_iota(jnp.int32, (C, C), 0)
    jj = lax.broadcasted_iota(jnp.int32, (C, C), 1)
    tri = jnp.where(ii >= jj, 1.0, 0.0).astype(_BF16)
    gc_all = sum(jnp.dot(tri, part, preferred_element_type=_F32)
                 for part in _split_hi_mid_lo(g_all))
    mix = gc_all + beta_all
    eye = jnp.where(lax.broadcasted_iota(jnp.int32, (SUBLANES, LANES), 0)
                    == lax.broadcasted_iota(jnp.int32, (SUBLANES, LANES), 1), 1.0, 0.0).astype(_BF16)
    nt = (((1,), (1,)), ((), ()))
    mix_rows = sum(lax.dot_general(eye, part, nt, preferred_element_type=_F32)
                   for part in _split_hi_mid_lo(mix))
    scale = GDN_DK ** -0.5
    for h in range(GDN_HEADS):
        q = y[:, h * GDN_DK:(h + 1) * GDN_DK]
        k = y[:, A_QK + h * GDN_DK:A_QK + (h + 1) * GDN_DK]
        v = y[:, 2 * A_QK + h * GDN_DV:2 * A_QK + (h + 1) * GDN_DV] * live
        q = q * lax.rsqrt(jnp.sum(q * q, axis=-1, keepdims=True) + 1e-6) * (live * scale)
        k = k * lax.rsqrt(jnp.sum(k * k, axis=-1, keepdims=True) + 1e-6) * live
        g_col, b_col = gc_all[:, h:h + 1], beta_all[:, GDN_HEADS + h:GDN_HEADS + h + 1]
        g_row, b_row = mix_rows[h:h + 1, :], mix_rows[GDN_HEADS + h:GDN_HEADS + h + 1, :]
        k_bf = k.astype(_BF16)
        kk = lax.dot_general(k_bf, k_bf, nt, preferred_element_type=_F32)
        qk = lax.dot_general(q.astype(_BF16), k_bf, nt, preferred_element_type=_F32)
        qk = qk * jnp.exp(jnp.where(ii >= jj, g_col - g_row, NEG_BIG))
        at = b_row * kk * jnp.exp(jnp.where(jj > ii, g_row - g_col, NEG_BIG))
        t_inv = jnp.concatenate(_unit_lower_inverse(at, valid), axis=0)
        rhs = jnp.concatenate([v * b_col, k * (b_col * jnp.exp(g_col))], axis=1)
        sol = _dot_two_term(t_inv, rhs)
        u, w_ = sol[:, :GDN_DV], sol[:, GDN_DV:]
        s = state_ref[h]
        s_bf = s.astype(_BF16)
        v_new = u - jnp.dot(w_.astype(_BF16), s_bf, preferred_element_type=_F32)
        v_new_bf = v_new.astype(_BF16)
        o = (jnp.dot((q * jnp.exp(g_col)).astype(_BF16), s_bf, preferred_element_type=_F32)
             + jnp.dot(qk.astype(_BF16), v_new_bf, preferred_element_type=_F32))
        g_last = g_col[C - 1:C, :]
        k_dec = (k * jnp.exp(g_last - g_col)).astype(_BF16)
        state_ref[h] = s * jnp.exp(g_last) + lax.dot_general(k_dec, v_new_bf, (((0,), (0,)), ((), ())),
                                                             preferred_element_type=_F32)
        o = o * lax.rsqrt(jnp.mean(o * o, axis=-1, keepdims=True) + RMS_EPS) * norm_ref[...]
        zh = pad(z_ref[:, h * GDN_DV:(h + 1) * GDN_DV])
        o = o * (zh * jax.nn.sigmoid(zh))
        o_ref[:, h * GDN_DV:(h + 1) * GDN_DV] = o[:rows]

    @pl.when(c_idx == n_chunks - 1)
    def _():
        sfin_ref[0] = state_ref[...]


def _gdn(qkv, z, gates, conv_buf, s0, conv_w, a_log, dt_bias, gdn_norm, *, n_batch, seq, row0):
    rows = min(GDN_CHUNK, seq)
    assert seq % rows == 0 and row0 % rows == 0 and (rows == GDN_CHUNK or seq == rows) and rows % SUBLANES == 0
    n_chunks = seq // rows
    r0 = row0 // rows
    row_map = lambda b, c: (r0 + b * n_chunks + c, 0)
    const = lambda b, c: (0, 0)
    lane_pad = lambda vec: jnp.zeros((1, LANES), _F32).at[0, :vec.shape[0]].set(vec)
    return pl.pallas_call(
        functools.partial(_gdn_kernel, rows=rows, n_chunks=n_chunks),
        grid=(n_batch, n_chunks),
        in_specs=[pl.BlockSpec((rows, A_CONV_CH), row_map),
                  pl.BlockSpec((rows, A_WIDTH), row_map),
                  pl.BlockSpec((rows, LANES), row_map),
                  pl.BlockSpec((1, GDN_CONV - 1, A_CONV_CH), lambda b, c: (b, 0, 0)),
                  pl.BlockSpec((1, GDN_HEADS, GDN_DK, GDN_DV), lambda b, c: (b, 0, 0, 0)),
                  pl.BlockSpec((GDN_CONV, A_CONV_CH), const),
                  pl.BlockSpec((1, LANES), const),
                  pl.BlockSpec((1, LANES), const),
                  pl.BlockSpec((1, GDN_DV), const)],
        out_specs=[pl.BlockSpec((rows, A_WIDTH), lambda b, c: (b * n_chunks + c, 0)),
                   pl.BlockSpec((1, GDN_HEADS, GDN_DK, GDN_DV), lambda b, c: (b, 0, 0, 0)),
                   pl.BlockSpec((1, GDN_CONV - 1, A_CONV_CH), lambda b, c: (b, 0, 0))],
        out_shape=[jax.ShapeDtypeStruct((n_batch * seq, A_WIDTH), _F32),
                   jax.ShapeDtypeStruct((n_batch, GDN_HEADS, GDN_DK, GDN_DV), _F32),
                   jax.ShapeDtypeStruct((n_batch, GDN_CONV - 1, A_CONV_CH), _F32)],
        scratch_shapes=[pltpu.VMEM((GDN_HEADS, GDN_DK, GDN_DV), _F32), pltpu.VMEM((SUBLANES, A_CONV_CH), _F32)],
        compiler_params=_params(2),
        name="gdn",
    )(qkv, z, gates, conv_buf, s0, conv_w, lane_pad(a_log), lane_pad(dt_bias), gdn_norm.reshape(1, GDN_DV))


def _stack_layers(states):
    return [jnp.stack(a) for a in zip(*states)]


def kernel(x_prompt, x_sample, state_gdn_s, state_gdn_conv, cache_fox_k, cache_fox_v, cache_fox_logf, cache_nsa_cmp_k, cache_nsa_cmp_v, cache_nsa_sel_k, cache_nsa_sel_v, cache_nsa_win_k, cache_nsa_win_v, cache_mem_k, cache_mem_v, page_table, mem_prompt, norm_mix, norm_xattn, norm_mem, norm_ffn, norm_final, w_in_even, b_forget, gdn_conv_w, gdn_a_log, gdn_dt_bias, gdn_norm, w_out_even, w_in_odd, nsa_gate_bias, nsa_cmp_pos, nsa_cmp_w1, nsa_cmp_w2, w_out_odd, w_mem_q, w_mem_kv, w_mem_o, w_ffn_in, w_ffn_out):
    B, L, D = x_prompt.shape
    SB, SL, _ = x_sample.shape
    depth = norm_mix.shape[0]
    n_p = B * L
    n_s = SB * SL
    past_len = page_table.shape[1] * cache_fox_k.shape[2]
    pos_p = jnp.arange(L, dtype=jnp.int32)
    pos_s = past_len + jnp.arange(SL, dtype=jnp.int32)
    ffn_hidden = w_ffn_out.shape[1]
    row_tile = min(ROW_TILE, L)
    attn_tile = min(ATTN_TILE, L)
    mem_width = MEM_HEADS * MEM_DH
    keep = min(WINDOW, L)

    def rows_p(a, *shape):
        return a[:n_p].reshape(B, L, *shape)

    def rows_s(a, *shape):
        return a[n_p:].reshape(SB, SL, *shape)

    n_pool, page = cache_fox_k.shape[1:3]
    w_buf = cache_nsa_win_k.shape[2]
    fox_k_pages, fox_v_pages = _pages_by_head(cache_fox_k), _pages_by_head(cache_fox_v)
    fox_c_pool = jnp.transpose(jnp.cumsum(cache_fox_logf, axis=2), (0, 1, 3, 2)).reshape(-1, FOX_HEADS, page)
    sel_k_pages, sel_v_pages = _pages_by_head(cache_nsa_sel_k), _pages_by_head(cache_nsa_sel_v)
    cmp_k_rows = cache_nsa_cmp_k.reshape(-1, 2 * page, LANES)
    cmp_v_rows = cache_nsa_cmp_v.reshape(-1, 2 * page, LANES)
    win_k_rows = cache_nsa_win_k.reshape(-1, C_KV_WIDTH)
    win_v_rows = cache_nsa_win_v.reshape(-1, C_KV_WIDTH)
    mem_k_rows = cache_mem_k.reshape(-1, MEM_HEADS * MEM_DH)
    mem_v_rows = cache_mem_v.reshape(-1, MEM_HEADS * MEM_DH)

    x = jnp.concatenate([x_prompt.reshape(n_p, D), x_sample.reshape(n_s, D)], axis=0)
    mem_flat = mem_prompt.reshape(B * MEM_TOKENS, D)
    even_p, even_s, odd_p, odd_s, mem_p = [], [], [], [], []
    for layer in range(depth):
        if layer % 2 == 0:
            e = layer // 2
            w = w_in_even[e]
            gate_cols = A_CONV_CH + 2 * GDN_HEADS
            w_perm = jnp.concatenate(
                [w[:, :A_CONV_CH], w[:, gate_cols:gate_cols + A_WIDTH + 3 * B_WIDTH], w[:, A_CONV_CH:gate_cols],
                 w[:, gate_cols + A_WIDTH + 3 * B_WIDTH:],
                 jnp.zeros((D, LANES - 2 * GDN_HEADS - FOX_HEADS), w.dtype)], axis=1).astype(_BF16)
            qkv_a, z, q_b, k_b, v_b, small = _rms_matmul_split(
                x, norm_mix[layer], w_perm, [A_CONV_CH, A_WIDTH, B_WIDTH, B_WIDTH, B_WIDTH, LANES], row_tile)
            logf = jax.nn.log_sigmoid(small[:, 2 * GDN_HEADS:2 * GDN_HEADS + FOX_HEADS] + b_forget[e])
            gdn_w = (gdn_conv_w[e], gdn_a_log[e], gdn_dt_bias[e], gdn_norm[e])
            c_rows = jnp.transpose(jnp.cumsum(rows_p(logf, FOX_HEADS), axis=1), (0, 2, 1))
            o_b_p = _fox_prompt(q_b, k_b, v_b, c_rows, n_batch=B, seq=L, tile=attn_tile)
            o_a_p, s_p, buf_p = _gdn(qkv_a, z, small, jnp.zeros((B, GDN_CONV - 1, A_CONV_CH), _F32),
                                     jnp.zeros((B, GDN_HEADS, GDN_DK, GDN_DV), _F32), *gdn_w,
                                     n_batch=B, seq=L, row0=0)
            even_p.append((s_p, buf_p, rows_p(k_b, FOX_HEADS, FOX_DH), rows_p(v_b, FOX_HEADS, FOX_DH),
                           rows_p(logf, FOX_HEADS)))
            c_new = jnp.transpose(jnp.cumsum(rows_s(logf, FOX_HEADS), axis=1), (0, 2, 1))
            c_new = jnp.pad(c_new, ((0, 0), (0, 0), (0, LANES - SL)))
            o_b_s = _fox_sample(q_b, k_b, v_b, c_new, fox_k_pages, fox_v_pages, fox_c_pool, page_table,
                                row0=n_p, sl=SL, page0=e * n_pool)
            o_a_s, s_s, buf_s = _gdn(qkv_a, z, small, state_gdn_conv[e], state_gdn_s[e], *gdn_w,
                                     n_batch=SB, seq=SL, row0=n_p)
            even_s.append((s_s, buf_s, rows_s(k_b, FOX_HEADS, FOX_DH), rows_s(v_b, FOX_HEADS, FOX_DH),
                           rows_s(logf, FOX_HEADS)))
            o_a = jnp.concatenate([o_a_p, o_a_s], axis=0)
            o_b = jnp.concatenate([o_b_p, o_b_s], axis=0)
            w_out = w_out_even[e].astype(_BF16)
            x = _matmul_residual([o_a, o_b], [w_out[:A_WIDTH], w_out[A_WIDTH:]], x, row_tile)
        else:
            o = layer // 2
            q, qr, kc, vc, ks, vs, kw, vw, gates = _odd_proj(
                x, norm_mix[layer], w_in_odd[o], nsa_gate_bias[o], pos_p, pos_s, n_p, row_tile)
            cmp_w = (nsa_cmp_pos[o], nsa_cmp_w1[o], nsa_cmp_w2[o])
            o_cmp, sel = _nsa_cmp_prompt(q, gates, kc, vc, *cmp_w, n_batch=B, seq=L, tile=attn_tile)
            o_p = _nsa_selwin_prompt(qr, gates, sel, o_cmp, ks, vs, kw, vw, n_batch=B, seq=L, tile=attn_tile)
            kv4 = (NSA_KV_HEADS, NSA_DH)
            odd_p.append((rows_p(kc, *kv4), rows_p(vc, *kv4), rows_p(ks, *kv4), rows_p(vs, *kv4),
                          rows_p(kw, *kv4)[:, L - keep:], rows_p(vw, *kv4)[:, L - keep:]))
            o_s = _nsa_sample(q, qr, gates, ks, vs, kw, vw, win_k_rows, win_v_rows,
                              cmp_k_rows, cmp_v_rows, sel_k_pages, sel_v_pages, page_table, *cmp_w,
                              row0=n_p, sl=SL, page0=o * n_pool, win_block0=o * SB, w_buf=w_buf)
            new_wk = jnp.concatenate([cache_nsa_win_k[o], rows_s(kw, *kv4)], axis=1)[:, SL:]
            new_wv = jnp.concatenate([cache_nsa_win_v[o], rows_s(vw, *kv4)], axis=1)[:, SL:]
            odd_s.append((rows_s(kc, *kv4), rows_s(vc, *kv4), rows_s(ks, *kv4), rows_s(vs, *kv4), new_wk, new_wv))
            o_all = jnp.concatenate([o_p, o_s], axis=0)
            x = _matmul_residual([o_all], [w_out_odd[o].astype(_BF16)], x, row_tile)

        mkv = _rms_matmul(mem_flat, norm_mem[layer], w_mem_kv[layer].astype(_BF16), row_tile=min(ROW_TILE, B * MEM_TOKENS))
        mem_p.append((mkv[:, :mem_width].reshape(B, MEM_TOKENS, MEM_HEADS, MEM_DH),
                      mkv[:, mem_width:].reshape(B, MEM_TOKENS, MEM_HEADS, MEM_DH)))
        q = _rms_matmul(x, norm_xattn[layer], w_mem_q[layer].astype(_BF16), row_tile=row_tile)
        o_p = _mem_attn(q, mkv, mkv, n_batch=B, q_len=L, q_row0=0, q_tile=row_tile, k_col_block=0, v_col_block=1)
        o_s = _mem_attn(q, mem_k_rows, mem_v_rows, n_batch=SB, q_len=SL, q_row0=n_p, q_tile=SL,
                        k_col_block=0, v_col_block=0, kv_block0=layer * SB)
        o_mem = jnp.concatenate([o_p.astype(_F32), o_s], axis=0)
        x = _matmul_residual([o_mem], [w_mem_o[layer].astype(_BF16)], x, row_tile)

        x = _ffn(x, norm_ffn[layer], w_ffn_in[layer][:, :ffn_hidden].astype(_BF16),
                 w_ffn_in[layer][:, ffn_hidden:].astype(_BF16), w_ffn_out[layer].astype(_BF16), row_tile)

    y = _rms(x, norm_final, row_tile)
    y_prompt = y[:n_p].reshape(B, L, D)
    y_sample = y[n_p:].reshape(SB, SL, D)
    p_gdn_s, p_gdn_conv, p_fox_k, p_fox_v, p_fox_logf = _stack_layers(even_p)
    s_gdn_s, s_gdn_conv, s_fox_k, s_fox_v, s_fox_logf = _stack_layers(even_s)
    p_cmp_k, p_cmp_v, p_sel_k, p_sel_v, p_win_k, p_win_v = _stack_layers(odd_p)
    s_cmp_k, s_cmp_v, s_sel_k, s_sel_v, s_win_k, s_win_v = _stack_layers(odd_s)
    p_mem_k, p_mem_v = _stack_layers(mem_p)
    return (y_prompt, y_sample,
            p_gdn_s, p_gdn_conv, p_fox_k, p_fox_v, p_fox_logf,
            p_cmp_k, p_cmp_v, p_sel_k, p_sel_v, p_win_k, p_win_v, p_mem_k, p_mem_v,
            s_gdn_s, s_gdn_conv, s_fox_k, s_fox_v, s_fox_logf,
            s_cmp_k, s_cmp_v, s_sel_k, s_sel_v, s_win_k, s_win_v)
```

```python
import functools

import jax
import jax.numpy as jnp
import numpy as np
from jax import lax
from jax.experimental import pallas as pl
from jax.experimental.pallas import tpu as pltpu

D_MODEL = 1024
RMS_EPS = 1e-6
NEG_BIG = -1e30

GDN_HEADS = 4
GDN_DK = 128
GDN_DV = 128
GDN_CONV = 4
GDN_CHUNK = 64
A_QK = GDN_HEADS * GDN_DK
A_WIDTH = GDN_HEADS * GDN_DV
A_CONV_CH = 2 * A_QK + A_WIDTH

FOX_HEADS = 8
FOX_DH = 64
B_WIDTH = FOX_HEADS * FOX_DH

NSA_HEADS = 16
NSA_KV_HEADS = 4
NSA_GROUP = NSA_HEADS // NSA_KV_HEADS
NSA_DH = 64
C_WIDTH = NSA_HEADS * NSA_DH
C_KV_WIDTH = NSA_KV_HEADS * NSA_DH
CMP_LEN = 32
CMP_STRIDE = CMP_LEN // 2
SEL_BLOCK = 64
SEL_TOP_N = 16
WINDOW = 512

ROPE_THETA = 500000.0
ROT_DIM = NSA_DH // 4

MEM_TOKENS = 256
MEM_HEADS = 4
MEM_DH = D_MODEL // MEM_HEADS

LANES = 128
VMEM_LIMIT_BYTES = 56 * 1024 * 1024
ROW_TILE = 512
ATTN_TILE = 256

_BF16 = jnp.bfloat16
_F32 = jnp.float32


def _params(n_grid_dims):
    return pltpu.CompilerParams(
        dimension_semantics=("arbitrary",) * n_grid_dims,
        vmem_limit_bytes=VMEM_LIMIT_BYTES)


def _rms_rows(x, g):
    return x * lax.rsqrt(jnp.mean(x * x, axis=-1, keepdims=True) + RMS_EPS) * g


def _rms_matmul_kernel(x_ref, g_ref, w_ref, o_ref, *, col_chunk):
    xn = _rms_rows(x_ref[...], g_ref[...]).astype(_BF16)
    width = w_ref.shape[1]
    for c0 in range(0, width, col_chunk):
        c1 = min(width, c0 + col_chunk)
        o_ref[:, c0:c1] = jnp.dot(xn, w_ref[:, c0:c1], preferred_element_type=_F32).astype(o_ref.dtype)


def _rms_matmul(x, g, w_bf16, out_dtype=_F32, row_tile=ROW_TILE):
    n, d = x.shape
    width = w_bf16.shape[1]
    assert n % row_tile == 0 and width % LANES == 0
    return pl.pallas_call(
        functools.partial(_rms_matmul_kernel, col_chunk=512),
        grid=(n // row_tile,),
        in_specs=[pl.BlockSpec((row_tile, d), lambda i: (i, 0)),
                  pl.BlockSpec((1, d), lambda i: (0, 0)),
                  pl.BlockSpec((d, width), lambda i: (0, 0))],
        out_specs=pl.BlockSpec((row_tile, width), lambda i: (i, 0)),
        out_shape=jax.ShapeDtypeStruct((n, width), out_dtype),
        compiler_params=_params(1),
        name="rms_matmul",
    )(x, g.reshape(1, d), w_bf16)


def _matmul_residual_kernel(*refs, n_in):
    r_ref, o_ref = refs[2 * n_in], refs[2 * n_in + 1]
    acc = r_ref[...]
    for a_ref, w_ref in zip(refs[:n_in], refs[n_in:2 * n_in]):
        acc = acc + jnp.dot(a_ref[...].astype(_BF16), w_ref[...], preferred_element_type=_F32)
    o_ref[...] = acc


def _matmul_residual(a_list, w_list, resid, row_tile=ROW_TILE):
    n, d = resid.shape
    n_in = len(a_list)
    assert n % row_tile == 0
    return pl.pallas_call(
        functools.partial(_matmul_residual_kernel, n_in=n_in),
        grid=(n // row_tile,),
        in_specs=[pl.BlockSpec((row_tile, a.shape[1]), lambda i: (i, 0)) for a in a_list]
                 + [pl.BlockSpec(w.shape, lambda i: (0, 0)) for w in w_list]
                 + [pl.BlockSpec((row_tile, d), lambda i: (i, 0))],
        out_specs=pl.BlockSpec((row_tile, d), lambda i: (i, 0)),
        out_shape=jax.ShapeDtypeStruct((n, d), _F32),
        input_output_aliases={2 * n_in: 0},
        compiler_params=_params(1),
        name="matmul_residual",
    )(*a_list, *w_list, resid)


def _ffn_kernel(x_ref, g_ref, wg_ref, wu_ref, wo_ref, o_ref, *, chunk):
    x = x_ref[...]
    xn = _rms_rows(x, g_ref[...]).astype(_BF16)
    o_ref[...] = x
    hidden = wg_ref.shape[1]
    for c0 in range(0, hidden, chunk):
        gate = jnp.dot(xn, wg_ref[:, c0:c0 + chunk], preferred_element_type=_F32)
        up = jnp.dot(xn, wu_ref[:, c0:c0 + chunk], preferred_element_type=_F32)
        h = (gate * jax.nn.sigmoid(gate) * up).astype(_BF16)
        o_ref[...] += jnp.dot(h, wo_ref[c0:c0 + chunk, :], preferred_element_type=_F32)


def _ffn(x, g, wg, wu, wo, row_tile=ROW_TILE):
    n, d = x.shape
    hidden = wg.shape[1]
    chunk = 256
    assert n % row_tile == 0 and hidden % chunk == 0
    const = lambda i: (0, 0)
    return pl.pallas_call(
        functools.partial(_ffn_kernel, chunk=chunk),
        grid=(n // row_tile,),
        in_specs=[pl.BlockSpec((row_tile, d), lambda i: (i, 0)),
                  pl.BlockSpec((1, d), const),
                  pl.BlockSpec((d, hidden), const, pipeline_mode=pl.Buffered(1)),
                  pl.BlockSpec((d, hidden), const, pipeline_mode=pl.Buffered(1)),
                  pl.BlockSpec((hidden, d), const, pipeline_mode=pl.Buffered(1))],
        out_specs=pl.BlockSpec((row_tile, d), lambda i: (i, 0)),
        out_shape=jax.ShapeDtypeStruct((n, d), _F32),
        input_output_aliases={0: 0},
        compiler_params=_params(1),
        name="ffn",
    )(x, g.reshape(1, d), wg, wu, wo)


def _mem_attn_kernel(q_ref, k_ref, v_ref, o_ref):
    scale = MEM_DH ** -0.5
    for h in range(MEM_HEADS):
        cols = slice(h * MEM_DH, (h + 1) * MEM_DH)
        q = q_ref[:, cols].astype(_BF16)
        k = k_ref[:, cols].astype(_BF16)
        v = v_ref[:, cols].astype(_BF16)
        s = lax.dot_general(q, k, (((1,), (1,)), ((), ())), preferred_element_type=_F32) * scale
        p = jnp.exp(s - jnp.max(s, axis=-1, keepdims=True))
        inv = 1.0 / jnp.sum(p, axis=-1, keepdims=True)
        o = jnp.dot(p.astype(_BF16), v, preferred_element_type=_F32) * inv
        o_ref[:, cols] = o.astype(o_ref.dtype)


def _mem_attn(q, k, v, *, n_batch, q_len, q_row0, q_tile, k_col_block, v_col_block, kv_block0=0):
    assert q_len % q_tile == 0 and q_row0 % q_tile == 0
    tiles = q_len // q_tile
    t0 = q_row0 // q_tile
    width = MEM_HEADS * MEM_DH
    return pl.pallas_call(
        _mem_attn_kernel,
        grid=(n_batch, tiles),
        in_specs=[pl.BlockSpec((q_tile, width), lambda b, i: (t0 + b * tiles + i, 0)),
                  pl.BlockSpec((MEM_TOKENS, width), lambda b, i: (kv_block0 + b, k_col_block)),
                  pl.BlockSpec((MEM_TOKENS, width), lambda b, i: (kv_block0 + b, v_col_block))],
        out_specs=pl.BlockSpec((q_tile, width), lambda b, i: (b * tiles + i, 0)),
        out_shape=jax.ShapeDtypeStruct((n_batch * q_len, width), _BF16 if q_tile % 16 == 0 else _F32),
        compiler_params=_params(2),
        name="mem_attn",
    )(q, k, v)


def _rms_kernel(x_ref, g_ref, o_ref):
    o_ref[...] = _rms_rows(x_ref[...], g_ref[...])


def _rms(x, g, row_tile=ROW_TILE):
    n, d = x.shape
    return pl.pallas_call(
        _rms_kernel,
        grid=(n // row_tile,),
        in_specs=[pl.BlockSpec((row_tile, d), lambda i: (i, 0)),
                  pl.BlockSpec((1, d), lambda i: (0, 0))],
        out_specs=pl.BlockSpec((row_tile, d), lambda i: (i, 0)),
        out_shape=jax.ShapeDtypeStruct((n, d), _F32),
        compiler_params=_params(1),
        name="final_rms",
    )(x, g.reshape(1, d))


def _rms_matmul_split_kernel(x_ref, g_ref, w_ref, *o_refs, col_chunk):
    xn = _rms_rows(x_ref[...], g_ref[...]).astype(_BF16)
    c0 = 0
    for o_ref in o_refs:
        width = o_ref.shape[1]
        for s0 in range(0, width, col_chunk):
            s1 = min(width, s0 + col_chunk)
            o_ref[:, s0:s1] = jnp.dot(xn, w_ref[:, c0 + s0:c0 + s1], preferred_element_type=_F32)
        c0 += width


def _rms_matmul_split(x, g, w_bf16, widths, row_tile=ROW_TILE):
    n, d = x.shape
    assert n % row_tile == 0 and sum(widths) == w_bf16.shape[1] and all(w % LANES == 0 for w in widths)
    return pl.pallas_call(
        functools.partial(_rms_matmul_split_kernel, col_chunk=512),
        grid=(n // row_tile,),
        in_specs=[pl.BlockSpec((row_tile, d), lambda i: (i, 0)),
                  pl.BlockSpec((1, d), lambda i: (0, 0)),
                  pl.BlockSpec((d, w_bf16.shape[1]), lambda i: (0, 0))],
        out_specs=[pl.BlockSpec((row_tile, w), lambda i: (i, 0)) for w in widths],
        out_shape=[jax.ShapeDtypeStruct((n, w), _F32) for w in widths],
        compiler_params=_params(1),
        name="rms_matmul_split",
    )(x, g.reshape(1, d), w_bf16)


def _rope_apply(x, cos, sin_up, sin_dn):
    width = x.shape[1]
    reps = width // cos.shape[1]
    if reps > 1:
        cos, sin_up, sin_dn = [jnp.concatenate([t] * reps, axis=1) for t in (cos, sin_up, sin_dn)]
    return x * cos + pltpu.roll(x, width - ROT_DIM // 2, 1) * sin_up + pltpu.roll(x, ROT_DIM // 2, 1) * sin_dn


def _odd_proj_kernel(x_ref, g_ref, w_ref, gb_ref, cos_ref, sup_ref, sdn_ref,
                     q_ref, qr_ref, kc_ref, vc_ref, ks_ref, vs_ref, kw_ref, vw_ref, gate_ref):
    xn = _rms_rows(x_ref[...], g_ref[...]).astype(_BF16)
    cos, sup, sdn = cos_ref[...], sup_ref[...], sdn_ref[...]
    kvw = C_KV_WIDTH
    for h0 in range(0, C_WIDTH, 512):
        q = jnp.dot(xn, w_ref[:, h0:h0 + 512], preferred_element_type=_F32)
        q_ref[:, h0:h0 + 512] = q
        qr_ref[:, h0:h0 + 512] = _rope_apply(q, cos, sup, sdn)
    c0 = C_WIDTH
    for o_ref, rot in ((kc_ref, False), (vc_ref, False), (ks_ref, True), (vs_ref, False), (kw_ref, True), (vw_ref, False)):
        y = jnp.dot(xn, w_ref[:, c0:c0 + kvw], preferred_element_type=_F32)
        o_ref[...] = _rope_apply(y, cos, sup, sdn) if rot else y
        c0 += kvw
    gl = jnp.dot(xn, w_ref[:, c0:c0 + LANES], preferred_element_type=_F32)
    gate_ref[...] = jax.nn.sigmoid(gl + gb_ref[...])


def _rope_tables(pos):
    half = ROT_DIM // 2
    inv_freq = ROPE_THETA ** (-jnp.arange(half, dtype=_F32) / half)
    ang = pos.astype(_F32)[:, None] * inv_freq[None, :]
    cos, sin = jnp.cos(ang), jnp.sin(ang)
    rest = NSA_DH - ROT_DIM
    ones = jnp.ones((pos.shape[0], rest), _F32)
    zeros = jnp.zeros((pos.shape[0], rest), _F32)
    z8 = jnp.zeros_like(sin)
    head_cos = jnp.concatenate([cos, cos, ones], axis=1)
    head_up = jnp.concatenate([-sin, z8, zeros], axis=1)
    head_dn = jnp.concatenate([z8, sin, zeros], axis=1)
    return [jnp.tile(t, (1, NSA_KV_HEADS)) for t in (head_cos, head_up, head_dn)]


def _odd_proj(x, g, w_in, gate_bias, pos_p, pos_s, n_p, row_tile=ROW_TILE):
    n, d = x.shape
    L, SL = pos_p.shape[0], pos_s.shape[0]
    assert L % row_tile == 0 and n_p % row_tile == 0 and (n - n_p) % row_tile == 0 and row_tile % SL == 0
    head = np.arange(NSA_HEADS)
    gl_cols = np.concatenate([C_WIDTH + 6 * C_KV_WIDTH + head * 3 + br for br in range(3)])
    w = jnp.concatenate([w_in[:, :C_WIDTH + 6 * C_KV_WIDTH], w_in[:, gl_cols],
                         jnp.zeros((d, LANES - 3 * NSA_HEADS), w_in.dtype)], axis=1).astype(_BF16)
    gb = jnp.concatenate([gate_bias[gl_cols - (C_WIDTH + 6 * C_KV_WIDTH)], jnp.zeros((LANES - 3 * NSA_HEADS,), _F32)]).reshape(1, LANES)
    tables = _rope_tables(jnp.concatenate([pos_p, jnp.tile(pos_s, row_tile // SL)]))
    tiles_per_seq = L // row_tile
    n_p_tiles = n_p // row_tile
    tab_map = lambda i: (jnp.where(i < n_p_tiles, i % tiles_per_seq, tiles_per_seq), 0)
    row_map = lambda i: (i, 0)
    const = lambda i: (0, 0)
    widths = [C_WIDTH, C_WIDTH] + [C_KV_WIDTH] * 6 + [LANES]
    return pl.pallas_call(
        _odd_proj_kernel,
        grid=(n // row_tile,),
        in_specs=[pl.BlockSpec((row_tile, d), row_map),
                  pl.BlockSpec((1, d), const),
                  pl.BlockSpec((d, w.shape[1]), const),
                  pl.BlockSpec((1, LANES), const)]
                 + [pl.BlockSpec((row_tile, C_KV_WIDTH), tab_map)] * 3,
        out_specs=[pl.BlockSpec((row_tile, wd), row_map) for wd in widths],
        out_shape=[jax.ShapeDtypeStruct((n, wd), _F32) for wd in widths],
        compiler_params=_params(1),
        name="odd_proj",
    )(x, g.reshape(1, d), w, gb, *tables)


MASKED = -2e30


def _softmax_step(s, v_bf16, m, l, acc):
    m_new = jnp.maximum(m, jnp.max(s, axis=-1, keepdims=True))
    alpha = jnp.exp(m - m_new)
    p = jnp.exp(s - m_new)
    l = alpha * l + jnp.sum(p, axis=-1, keepdims=True)
    acc = alpha * acc + jnp.dot(p.astype(_BF16), v_bf16, preferred_element_type=_F32)
    return m_new, l, acc


def _fox_prompt_kernel(q_ref, k_ref, v_ref, c_ref, o_ref, *, tile):
    i = pl.program_id(1)
    scale = FOX_DH ** -0.5
    row = lax.broadcasted_iota(jnp.int32, (tile, tile), 0)
    col = lax.broadcasted_iota(jnp.int32, (tile, tile), 1)
    diag_bias = jnp.where(col <= row, 0.0, MASKED)
    for h in range(FOX_HEADS):
        cols = slice(h * FOX_DH, (h + 1) * FOX_DH)
        q = (q_ref[:, cols] * scale).astype(_BF16)

        def logits(j, cols=cols, q=q, h=h):
            k0 = pl.multiple_of(j * tile, tile)
            k = k_ref[pl.ds(k0, tile), cols].astype(_BF16)
            v = v_ref[pl.ds(k0, tile), cols].astype(_BF16)
            s = lax.dot_general(q, k, (((1,), (1,)), ((), ())), preferred_element_type=_F32)
            return s - c_ref[0, h:h + 1, pl.ds(k0, tile)], v

        def body(j, carry, logits=logits):
            s, v = logits(j)
            return _softmax_step(s, v, *carry)

        init = (jnp.full((tile, 1), NEG_BIG, _F32), jnp.zeros((tile, 1), _F32), jnp.zeros((tile, FOX_DH), _F32))
        carry = lax.fori_loop(0, i, body, init)
        s, v = logits(i)
        m, l, acc = _softmax_step(s + diag_bias, v, *carry)
        o_ref[:, cols] = acc / jnp.maximum(l, 1e-30)


def _fox_prompt(q, k, v, c_rows, *, n_batch, seq, tile):
    assert seq % tile == 0
    tiles = seq // tile
    return pl.pallas_call(
        functools.partial(_fox_prompt_kernel, tile=tile),
        grid=(n_batch, tiles),
        in_specs=[pl.BlockSpec((tile, B_WIDTH), lambda b, i: (b * tiles + i, 0)),
                  pl.BlockSpec((seq, B_WIDTH), lambda b, i: (b, 0)),
                  pl.BlockSpec((seq, B_WIDTH), lambda b, i: (b, 0)),
                  pl.BlockSpec((1, FOX_HEADS, seq), lambda b, i: (b, 0, 0))],
        out_specs=pl.BlockSpec((tile, B_WIDTH), lambda b, i: (b * tiles + i, 0)),
        out_shape=jax.ShapeDtypeStruct((n_batch * seq, B_WIDTH), _F32),
        compiler_params=_params(2),
        name="fox_prompt",
    )(q, k, v, c_rows)


def _split3_dot(x, w_bf16):
    hi = x.astype(_BF16)
    r1 = x - hi.astype(_F32)
    mid = r1.astype(_BF16)
    lo = (r1 - mid.astype(_F32)).astype(_BF16)
    dot = lambda a: jnp.dot(a, w_bf16, preferred_element_type=_F32)
    return dot(hi) + dot(mid) + dot(lo)


def _compress_into(x2_ref, n_sub, wpair_ref, w1_ref, w2_ref, pos_ref, out_ref, row_stride=2, pair_stride=1):
    pair_rows = []
    for p in range(NSA_KV_HEADS // 2):
        pieces = [x2_ref[pl.ds(r * row_stride + p * pair_stride, n_sub, stride=row_stride * CMP_STRIDE), :]
                  for r in range(CMP_STRIDE)]
        pair_rows.append(jnp.concatenate(pieces, axis=1))
    x = jnp.concatenate(pair_rows, axis=0).astype(_BF16)
    both = jnp.dot(x, wpair_ref[...], preferred_element_type=_F32)
    bias = jnp.dot(pos_ref[...].astype(_BF16), w1_ref[...], preferred_element_type=_F32)
    hid_w = 2 * NSA_DH
    for kv in range(NSA_KV_HEADS):
        p, s = divmod(kv, 2)
        rows = slice(p * n_sub, (p + 1) * n_sub)
        first = both[rows, s * hid_w:(s + 1) * hid_w]
        second = both[rows, (2 + s) * hid_w:(3 + s) * hid_w]
        pre = first + pltpu.roll(second, n_sub - 1, 0) + bias
        hid = (pre * jax.nn.sigmoid(pre)).astype(_BF16)
        out_ref[0:n_sub, kv * NSA_DH:(kv + 1) * NSA_DH] = jnp.dot(hid, w2_ref[...], preferred_element_type=_F32)


def _cmp_branch(q_ref, gate_ref, kcb_ref, vcb_ref, c2s_ref, o_ref, sel_ref, q_pos, n_sel):
    rows = q_ref.shape[0]
    nb = kcb_ref.shape[0]
    scale = NSA_DH ** -0.5
    blk_n = lax.broadcasted_iota(jnp.int32, (rows, nb), 1)
    cbias = jnp.where(blk_n * CMP_STRIDE + (CMP_LEN - 1) <= q_pos, 0.0, MASKED)
    lane = lax.broadcasted_iota(jnp.int32, (rows, LANES), 1)
    cur = q_pos // SEL_BLOCK
    causal = (lane <= cur) & (lane < n_sel)
    forced = (lane == 0) | (lane == cur) | (lane == cur - 1)
    for kv in range(NSA_KV_HEADS):
        kcols = slice(kv * NSA_DH, (kv + 1) * NSA_DH)
        kcb = kcb_ref[:, kcols].astype(_BF16)
        vcb = vcb_ref[:, kcols].astype(_BF16)
        p_sum = jnp.zeros((rows, nb), _F32)
        for g in range(NSA_GROUP):
            h = kv * NSA_GROUP + g
            hcols = slice(h * NSA_DH, (h + 1) * NSA_DH)
            q = (q_ref[:, hcols] * scale).astype(_BF16)
            s = lax.dot_general(q, kcb, (((1,), (1,)), ((), ())), preferred_element_type=_F32) + cbias
            m = jnp.maximum(jnp.max(s, axis=-1, keepdims=True), NEG_BIG)
            p = jnp.exp(s - m)
            p = p / jnp.maximum(jnp.sum(p, axis=-1, keepdims=True), 1e-30)
            p_sum = p_sum + p
            o = jnp.dot(p.astype(_BF16), vcb, preferred_element_type=_F32)
            o_ref[:, hcols] = o * gate_ref[:, h:h + 1]
        imp = _split3_dot(p_sum, c2s_ref[...])
        imp = jnp.where(causal, jnp.where(forced, -NEG_BIG, imp), NEG_BIG)
        ahead = jnp.zeros((rows, LANES), _F32)
        for j in range(n_sel):
            cj = imp[:, j:j + 1]
            after_j = jnp.where(lane[0:1] > j, 1.0, 0.0)
            ahead = ahead + jnp.where(cj > imp, 1.0, 0.0) + jnp.where(cj == imp, after_j, 0.0)
        sel = jnp.where((ahead < float(min(SEL_TOP_N, n_sel))) & (imp > 0.5 * NEG_BIG), 1.0, 0.0)
        sel_ref[:, kv * LANES:(kv + 1) * LANES] = sel


def _cmp_to_sel(nb):
    c_start = np.arange(nb)[:, None] * CMP_STRIDE
    s_start = np.arange(LANES)[None, :] * SEL_BLOCK
    shared = np.minimum(c_start + CMP_LEN, s_start + SEL_BLOCK) - np.maximum(c_start, s_start)
    return jnp.asarray(np.clip(shared, 0, None).astype(np.float32) / CMP_LEN, _BF16)


def _sel_expand(n_keys):
    return jnp.asarray((np.arange(n_keys)[None, :] // SEL_BLOCK == np.arange(LANES)[:, None]).astype(np.float32), _BF16)


def _nsa_cmp_prompt_kernel(q_ref, gate_ref, kc_ref, vc_ref, wpk_ref, w1k_ref, w2k_ref, posk_ref,
                           wpv_ref, w1v_ref, w2v_ref, posv_ref, c2s_ref, o_ref, sel_ref, kcb_ref, vcb_ref,
                           *, tile, n_sub, n_sel):
    i = pl.program_id(1)

    @pl.when(i == 0)
    def _():
        kcb_ref[...] = jnp.zeros_like(kcb_ref)
        vcb_ref[...] = jnp.zeros_like(vcb_ref)
        _compress_into(kc_ref, n_sub, wpk_ref, w1k_ref, w2k_ref, posk_ref, kcb_ref)
        _compress_into(vc_ref, n_sub, wpv_ref, w1v_ref, w2v_ref, posv_ref, vcb_ref)

    q_pos = i * tile + lax.broadcasted_iota(jnp.int32, (tile, 1), 0)
    _cmp_branch(q_ref, gate_ref, kcb_ref, vcb_ref, c2s_ref, o_ref, sel_ref, q_pos, n_sel)


def _cmp_weights(cmp_pos, cmp_w1, cmp_w2):
    out = []
    eye2 = jnp.eye(2, dtype=_F32)
    for idx in range(2):
        w1r = cmp_w1[idx].reshape(CMP_LEN, NSA_DH, 2 * NSA_DH)
        halves = [jnp.einsum('rdj,st->rsdtj', w1r[h * CMP_STRIDE:(h + 1) * CMP_STRIDE], eye2).reshape(
            CMP_STRIDE * 2 * NSA_DH, 4 * NSA_DH) for h in range(2)]
        out += [jnp.concatenate(halves, axis=1).astype(_BF16), cmp_w1[idx].astype(_BF16), cmp_w2[idx].astype(_BF16),
                cmp_pos[idx].reshape(1, CMP_LEN * NSA_DH)]
    return out


def _cmp_weight_specs(const):
    return [pl.BlockSpec((CMP_STRIDE * 2 * NSA_DH, 8 * NSA_DH), const),
            pl.BlockSpec((CMP_LEN * NSA_DH, 2 * NSA_DH), const),
            pl.BlockSpec((2 * NSA_DH, NSA_DH), const),
            pl.BlockSpec((1, CMP_LEN * NSA_DH), const)] * 2


def _nsa_cmp_prompt(q, gates, kc, vc, cmp_pos, cmp_w1, cmp_w2, *, n_batch, seq, tile):
    assert seq % tile == 0 and seq % SEL_BLOCK == 0
    tiles = seq // tile
    n_sub = seq // CMP_STRIDE
    nb = -(-n_sub // LANES) * LANES
    n_sel = seq // SEL_BLOCK
    row_map = lambda b, i: (b * tiles + i, 0)
    seq_map = lambda b, i: (b, 0)
    const = lambda b, i: (0, 0)
    return pl.pallas_call(
        functools.partial(_nsa_cmp_prompt_kernel, tile=tile, n_sub=n_sub, n_sel=n_sel),
        grid=(n_batch, tiles),
        in_specs=[pl.BlockSpec((tile, C_WIDTH), row_map),
                  pl.BlockSpec((tile, LANES), row_map),
                  pl.BlockSpec((2 * seq, LANES), seq_map),
                  pl.BlockSpec((2 * seq, LANES), seq_map)] + _cmp_weight_specs(const)
                 + [pl.BlockSpec((nb, LANES), const)],
        out_specs=[pl.BlockSpec((tile, C_WIDTH), row_map),
                   pl.BlockSpec((tile, NSA_KV_HEADS * LANES), row_map)],
        out_shape=[jax.ShapeDtypeStruct((n_batch * seq, C_WIDTH), _F32),
                   jax.ShapeDtypeStruct((n_batch * seq, NSA_KV_HEADS * LANES), _F32)],
        scratch_shapes=[pltpu.VMEM((nb, C_KV_WIDTH), _F32), pltpu.VMEM((nb, C_KV_WIDTH), _F32)],
        compiler_params=_params(2),
        name="nsa_cmp_prompt",
    )(q, gates, kc.reshape(-1, LANES), vc.reshape(-1, LANES), *_cmp_weights(cmp_pos, cmp_w1, cmp_w2), _cmp_to_sel(nb))


def _nsa_selwin_prompt_kernel(qr_ref, gate_ref, sel_ref, ocmp_ref, ks_ref, vs_ref, kw_ref, vw_ref, exp_ref,
                              o_ref, selbias_ref, *, tile):
    i = pl.program_id(1)
    scale = NSA_DH ** -0.5
    q0 = i * tile
    row = lax.broadcasted_iota(jnp.int32, (tile, tile), 0) + q0
    col = lax.broadcasted_iota(jnp.int32, (tile, tile), 1)
    rows4 = NSA_GROUP * tile
    init = (jnp.full((rows4, 1), NEG_BIG, _F32), jnp.zeros((rows4, 1), _F32), jnp.zeros((rows4, NSA_DH), _F32))
    first_win = jnp.maximum(i - (WINDOW + tile - 1) // tile, 0)
    for kv in range(NSA_KV_HEADS):
        kcols = slice(kv * NSA_DH, (kv + 1) * NSA_DH)
        picked = jnp.dot(sel_ref[:, kv * LANES:(kv + 1) * LANES].astype(_BF16), exp_ref[...], preferred_element_type=_F32)
        selbias_ref[...] = (1.0 - picked) * MASKED
        q = jnp.concatenate(
            [(qr_ref[:, (kv * NSA_GROUP + g) * NSA_DH:(kv * NSA_GROUP + g + 1) * NSA_DH] * scale).astype(_BF16)
             for g in range(NSA_GROUP)], axis=0)

        def sel_body(j, carry, q=q, kcols=kcols):
            k0 = pl.multiple_of(j * tile, tile)
            k = ks_ref[pl.ds(k0, tile), kcols].astype(_BF16)
            v = vs_ref[pl.ds(k0, tile), kcols].astype(_BF16)
            bias = selbias_ref[:, pl.ds(k0, tile)] + jnp.where(col + k0 <= row, 0.0, MASKED)
            s = lax.dot_general(q, k, (((1,), (1,)), ((), ())), preferred_element_type=_F32)
            return _softmax_step(s + jnp.concatenate([bias] * NSA_GROUP, axis=0), v, *carry)

        def win_body(j, carry, q=q, kcols=kcols):
            k0 = pl.multiple_of(j * tile, tile)
            k = kw_ref[pl.ds(k0, tile), kcols].astype(_BF16)
            v = vw_ref[pl.ds(k0, tile), kcols].astype(_BF16)
            dist = row - (col + k0)
            bias = jnp.where((dist >= 0) & (dist < WINDOW), 0.0, MASKED)
            s = lax.dot_general(q, k, (((1,), (1,)), ((), ())), preferred_element_type=_F32)
            return _softmax_step(s + jnp.concatenate([bias] * NSA_GROUP, axis=0), v, *carry)

        _, l_s, acc_s = lax.fori_loop(0, i + 1, sel_body, init)
        _, l_w, acc_w = lax.fori_loop(first_win, i + 1, win_body, init)
        o_s = acc_s / jnp.maximum(l_s, 1e-30)
        o_w = acc_w / jnp.maximum(l_w, 1e-30)
        for g in range(NSA_GROUP):
            h = kv * NSA_GROUP + g
            hcols = slice(h * NSA_DH, (h + 1) * NSA_DH)
            rws = slice(g * tile, (g + 1) * tile)
            o_ref[:, hcols] = (ocmp_ref[:, hcols]
                               + o_s[rws] * gate_ref[:, NSA_HEADS + h:NSA_HEADS + h + 1]
                               + o_w[rws] * gate_ref[:, 2 * NSA_HEADS + h:2 * NSA_HEADS + h + 1])


def _nsa_selwin_prompt(qr, gates, sel, o_cmp, ks, vs, kw, vw, *, n_batch, seq, tile):
    assert seq % tile == 0
    tiles = seq // tile
    row_map = lambda b, i: (b * tiles + i, 0)
    seq_map = lambda b, i: (b, 0)
    return pl.pallas_call(
        functools.partial(_nsa_selwin_prompt_kernel, tile=tile),
        grid=(n_batch, tiles),
        in_specs=[pl.BlockSpec((tile, C_WIDTH), row_map),
                  pl.BlockSpec((tile, LANES), row_map),
                  pl.BlockSpec((tile, NSA_KV_HEADS * LANES), row_map),
                  pl.BlockSpec((tile, C_WIDTH), row_map)]
                 + [pl.BlockSpec((seq, C_KV_WIDTH), seq_map)] * 4
                 + [pl.BlockSpec((LANES, seq), lambda b, i: (0, 0))],
        out_specs=pl.BlockSpec((tile, C_WIDTH), row_map),
        out_shape=jax.ShapeDtypeStruct((n_batch * seq, C_WIDTH), _F32),
        scratch_shapes=[pltpu.VMEM((tile, seq), _F32)],
        compiler_params=_params(2),
        name="nsa_selwin_prompt",
    )(qr, gates, sel, o_cmp, ks, vs, kw, vw, _sel_expand(seq))


def _page_specs(block, n_pages, page0=0):
    return [pl.BlockSpec(block, functools.partial(lambda b, pt, p: (page0 + pt[b, p],) + (0,) * (len(block) - 1), p=p))
            for p in range(n_pages)]


def _pad_rows_to(x, rows):
    return jnp.concatenate([x, jnp.zeros((rows - x.shape[0], x.shape[1]), x.dtype)], axis=0)


def _pages_by_head(pool):
    e, n_pool, page, heads, dh = pool.shape
    return jnp.transpose(pool, (0, 1, 3, 4, 2)).reshape(e * n_pool, heads, dh, page)


_NN = (((1,), (0,)), ((), ()))
_NT = (((1,), (1,)), ((), ()))


def _fox_sample_kernel(pt_ref, q_ref, kn_ref, vn_ref, cn_ref, *refs, n_pages, page):
    del pt_ref
    k_refs, v_refs, c_refs = refs[:n_pages], refs[n_pages:2 * n_pages], refs[2 * n_pages:3 * n_pages]
    o_ref, kall_ref, vall_ref, call_ref = refs[3 * n_pages:]
    sl = q_ref.shape[0]
    carry = jnp.zeros((FOX_HEADS, 1), _F32)
    for p in range(n_pages):
        kall_ref[:, :, p * page:(p + 1) * page] = k_refs[p][0].astype(_BF16)
        vall_ref[:, :, p * page:(p + 1) * page] = v_refs[p][0].astype(_BF16)
        c_page = c_refs[p][0] + carry
        call_ref[:, p * page:(p + 1) * page] = c_page
        carry = c_page[:, page - 1:page]
    c_new = cn_ref[0] + carry
    lane = lax.broadcasted_iota(jnp.int32, (sl, LANES), 1)
    q_idx = lax.broadcasted_iota(jnp.int32, (sl, LANES), 0)
    new_bias = jnp.where(lane <= q_idx, 0.0, MASKED)
    scale = FOX_DH ** -0.5
    for h in range(FOX_HEADS):
        cols = slice(h * FOX_DH, (h + 1) * FOX_DH)
        q = (q_ref[:, cols] * scale).astype(_BF16)
        s_past = lax.dot_general(q, kall_ref[h], _NN, preferred_element_type=_F32) - call_ref[h:h + 1, :]
        k_new = _pad_rows_to(kn_ref[:, cols], LANES).astype(_BF16)
        v_new = _pad_rows_to(vn_ref[:, cols], LANES).astype(_BF16)
        s_new = lax.dot_general(q, k_new, _NT, preferred_element_type=_F32) - c_new[h:h + 1, :] + new_bias
        m = jnp.maximum(jnp.max(s_past, axis=-1, keepdims=True), jnp.max(s_new, axis=-1, keepdims=True))
        p_past = jnp.exp(s_past - m)
        p_new = jnp.exp(s_new - m)
        l = jnp.sum(p_past, axis=-1, keepdims=True) + jnp.sum(p_new, axis=-1, keepdims=True)
        acc = (lax.dot_general(p_past.astype(_BF16), vall_ref[h], _NT, preferred_element_type=_F32)
               + jnp.dot(p_new.astype(_BF16), v_new, preferred_element_type=_F32))
        o_ref[:, cols] = acc / l


def _fox_sample(q, k, v, c_new, k_pages, v_pages, c_pool, page_table, *, row0, sl, page0):
    nb, n_pages = page_table.shape
    page = k_pages.shape[3]
    assert row0 % sl == 0 and sl <= LANES
    r0 = row0 // sl
    row_spec = pl.BlockSpec((sl, B_WIDTH), lambda b, pt: (r0 + b, 0))
    past = n_pages * page
    return pl.pallas_call(
        functools.partial(_fox_sample_kernel, n_pages=n_pages, page=page),
        grid_spec=pltpu.PrefetchScalarGridSpec(
            num_scalar_prefetch=1,
            grid=(nb,),
            in_specs=[row_spec, row_spec, row_spec, pl.BlockSpec((1, FOX_HEADS, LANES), lambda b, pt: (b, 0, 0))]
                     + _page_specs((1, FOX_HEADS, FOX_DH, page), n_pages, page0) * 2
                     + _page_specs((1, FOX_HEADS, page), n_pages, page0),
            out_specs=pl.BlockSpec((sl, B_WIDTH), lambda b, pt: (b, 0)),
            scratch_shapes=[pltpu.VMEM((FOX_HEADS, FOX_DH, past), _BF16), pltpu.VMEM((FOX_HEADS, FOX_DH, past), _BF16),
                            pltpu.VMEM((FOX_HEADS, past), _F32)]),
        out_shape=jax.ShapeDtypeStruct((nb * sl, B_WIDTH), _F32),
        compiler_params=_params(1),
        name="fox_sample",
    )(page_table, q, k, v, c_new, *([k_pages] * n_pages), *([v_pages] * n_pages), *([c_pool] * n_pages))


def _nsa_sample_kernel(pt_ref, q_ref, qr_ref, gate_ref, ksn_ref, vsn_ref, kwn_ref, vwn_ref, kwc_ref, vwc_ref,
                       wpk_ref, w1k_ref, w2k_ref, posk_ref, wpv_ref, w1v_ref, w2v_ref, posv_ref, c2s_ref, exp_ref,
                       *refs, n_pages, page, past_len, n_sel):
    del pt_ref
    ck_refs, cv_refs = refs[:n_pages], refs[n_pages:2 * n_pages]
    sk_refs, sv_refs = refs[2 * n_pages:3 * n_pages], refs[3 * n_pages:4 * n_pages]
    o_ref, xk_ref, xv_ref, kcb_ref, vcb_ref, ksel_ref, vsel_ref, ocmp_ref, sel_ref = refs[4 * n_pages:]
    sl = q_ref.shape[0]
    scale = NSA_DH ** -0.5
    assert page == LANES and 2 * NSA_DH == LANES
    eye = jnp.where(lax.broadcasted_iota(jnp.int32, (page, page), 0)
                    == lax.broadcasted_iota(jnp.int32, (page, page), 1), 1.0, 0.0).astype(_BF16)
    for p in range(n_pages):
        for src, dst in ((ck_refs[p], xk_ref), (cv_refs[p], xv_ref)):
            for pair in range(NSA_KV_HEADS // 2):
                slab = src[0, 2 * pair:2 * pair + 2].reshape(2 * NSA_DH, page).astype(_BF16)
                r0 = pair * past_len + p * page
                dst[r0:r0 + page, :] = lax.dot_general(eye, slab, _NT, preferred_element_type=_F32)
        ksel_ref[:, :, p * page:(p + 1) * page] = sk_refs[p][0].astype(_BF16)
        vsel_ref[:, :, p * page:(p + 1) * page] = sv_refs[p][0].astype(_BF16)
    n_sub = past_len // CMP_STRIDE
    _compress_into(xk_ref, n_sub, wpk_ref, w1k_ref, w2k_ref, posk_ref, kcb_ref, row_stride=1, pair_stride=past_len)
    _compress_into(xv_ref, n_sub, wpv_ref, w1v_ref, w2v_ref, posv_ref, vcb_ref, row_stride=1, pair_stride=past_len)
    q_pos = past_len + lax.broadcasted_iota(jnp.int32, (sl, 1), 0)
    _cmp_branch(q_ref, gate_ref, kcb_ref, vcb_ref, c2s_ref, ocmp_ref, sel_ref, q_pos, n_sel)

    lane = lax.broadcasted_iota(jnp.int32, (sl, LANES), 1)
    q_idx = lax.broadcasted_iota(jnp.int32, (sl, LANES), 0)
    new_ok = lane <= q_idx
    cache_rows = kwc_ref.shape[3]
    c_lane = lax.broadcasted_iota(jnp.int32, (sl, cache_rows), 1)
    c_qidx = lax.broadcasted_iota(jnp.int32, (sl, cache_rows), 0)
    c_dist = (past_len + c_qidx) - (past_len - cache_rows + c_lane)
    win_cache_bias = jnp.where((c_dist >= 0) & (c_dist < WINDOW), 0.0, MASKED)
    win_new_bias = jnp.where(new_ok, 0.0, MASKED)
    tile4 = lambda b: jnp.concatenate([b] * NSA_GROUP, axis=0)
    nt = (((1,), (1,)), ((), ()))
    for kv in range(NSA_KV_HEADS):
        kcols = slice(kv * NSA_DH, (kv + 1) * NSA_DH)
        q = jnp.concatenate(
            [(qr_ref[:, (kv * NSA_GROUP + g) * NSA_DH:(kv * NSA_GROUP + g + 1) * NSA_DH] * scale).astype(_BF16)
             for g in range(NSA_GROUP)], axis=0)
        sel = sel_ref[:, kv * LANES:(kv + 1) * LANES]
        picked = jnp.dot(sel.astype(_BF16), exp_ref[...], preferred_element_type=_F32)
        new_blk = past_len // SEL_BLOCK
        sel_new_bias = jnp.where(new_ok & (sel[:, new_blk:new_blk + 1] > 0.5), 0.0, MASKED)
        s_a = lax.dot_general(q, ksel_ref[kv], _NN, preferred_element_type=_F32) + tile4((1.0 - picked) * MASKED)
        k_new = _pad_rows_to(ksn_ref[:, kcols], LANES).astype(_BF16)
        s_b = lax.dot_general(q, k_new, nt, preferred_element_type=_F32) + tile4(sel_new_bias)
        m = jnp.maximum(jnp.maximum(jnp.max(s_a, axis=-1, keepdims=True), jnp.max(s_b, axis=-1, keepdims=True)), NEG_BIG)
        p_a, p_b = jnp.exp(s_a - m), jnp.exp(s_b - m)
        l = jnp.sum(p_a, axis=-1, keepdims=True) + jnp.sum(p_b, axis=-1, keepdims=True)
        o_sel = (lax.dot_general(p_a.astype(_BF16), vsel_ref[kv], _NT, preferred_element_type=_F32)
                 + jnp.dot(p_b.astype(_BF16), _pad_rows_to(vsn_ref[:, kcols], LANES).astype(_BF16),
                           preferred_element_type=_F32)) / jnp.maximum(l, 1e-30)
        s_a = lax.dot_general(q, kwc_ref[0, kv].astype(_BF16), _NN, preferred_element_type=_F32) + tile4(win_cache_bias)
        k_new = _pad_rows_to(kwn_ref[:, kcols], LANES).astype(_BF16)
        s_b = lax.dot_general(q, k_new, nt, preferred_element_type=_F32) + tile4(win_new_bias)
        m = jnp.maximum(jnp.maximum(jnp.max(s_a, axis=-1, keepdims=True), jnp.max(s_b, axis=-1, keepdims=True)), NEG_BIG)
        p_a, p_b = jnp.exp(s_a - m), jnp.exp(s_b - m)
        l = jnp.sum(p_a, axis=-1, keepdims=True) + jnp.sum(p_b, axis=-1, keepdims=True)
        o_win = (lax.dot_general(p_a.astype(_BF16), vwc_ref[0, kv].astype(_BF16), _NT, preferred_element_type=_F32)
                 + jnp.dot(p_b.astype(_BF16), _pad_rows_to(vwn_ref[:, kcols], LANES).astype(_BF16),
                           preferred_element_type=_F32)) / jnp.maximum(l, 1e-30)
        for g in range(NSA_GROUP):
            h = kv * NSA_GROUP + g
            hcols = slice(h * NSA_DH, (h + 1) * NSA_DH)
            rws = slice(g * sl, (g + 1) * sl)
            o_ref[:, hcols] = (ocmp_ref[:, hcols]
                               + o_sel[rws] * gate_ref[:, NSA_HEADS + h:NSA_HEADS + h + 1]
                               + o_win[rws] * gate_ref[:, 2 * NSA_HEADS + h:2 * NSA_HEADS + h + 1])


def _nsa_sample(q, qr, gates, ks, vs, kw, vw, win_k, win_v, cmp_k_pages, cmp_v_pages, sel_k_pages, sel_v_pages,
                page_table, cmp_pos, cmp_w1, cmp_w2, *, row0, sl, page0, win_block0):
    nb, n_pages = page_table.shape
    page = sel_k_pages.shape[3]
    past_len = n_pages * page
    cache_rows = win_k.shape[3]
    assert row0 % sl == 0 and sl <= SEL_BLOCK and past_len % SEL_BLOCK == 0 and past_len // CMP_STRIDE == LANES
    assert past_len >= cache_rows
    n_sel = past_len // SEL_BLOCK + 1
    r0 = row0 // sl
    row = lambda w: pl.BlockSpec((sl, w), lambda b, pt: (r0 + b, 0))
    const = lambda b, pt: (0, 0)
    return pl.pallas_call(
        functools.partial(_nsa_sample_kernel, n_pages=n_pages, page=page, past_len=past_len, n_sel=n_sel),
        grid_spec=pltpu.PrefetchScalarGridSpec(
            num_scalar_prefetch=1,
            grid=(nb,),
            in_specs=[row(C_WIDTH), row(C_WIDTH), row(LANES)] + [row(C_KV_WIDTH)] * 4
                     + [pl.BlockSpec((1, NSA_KV_HEADS, NSA_DH, cache_rows),
                                     lambda b, pt: (win_block0 + b, 0, 0, 0))] * 2
                     + _cmp_weight_specs(const)
                     + [pl.BlockSpec((LANES, LANES), const), pl.BlockSpec((LANES, past_len), const)]
                     + _page_specs((1, NSA_KV_HEADS, NSA_DH, page), n_pages, page0) * 4,
            out_specs=pl.BlockSpec((sl, C_WIDTH), lambda b, pt: (b, 0)),
            scratch_shapes=[pltpu.VMEM((2 * past_len, LANES), _F32), pltpu.VMEM((2 * past_len, LANES), _F32),
                            pltpu.VMEM((LANES, C_KV_WIDTH), _F32), pltpu.VMEM((LANES, C_KV_WIDTH), _F32),
                            pltpu.VMEM((NSA_KV_HEADS, NSA_DH, past_len), _BF16),
                            pltpu.VMEM((NSA_KV_HEADS, NSA_DH, past_len), _BF16),
                            pltpu.VMEM((sl, C_WIDTH), _F32), pltpu.VMEM((sl, NSA_KV_HEADS * LANES), _F32)]),
        out_shape=jax.ShapeDtypeStruct((nb * sl, C_WIDTH), _F32),
        compiler_params=_params(1),
        name="nsa_sample",
    )(page_table, q, qr, gates, ks, vs, kw, vw, win_k, win_v, *_cmp_weights(cmp_pos, cmp_w1, cmp_w2),
      _cmp_to_sel(LANES), _sel_expand(past_len),
      *([cmp_k_pages] * n_pages), *([cmp_v_pages] * n_pages), *([sel_k_pages] * n_pages), *([sel_v_pages] * n_pages))


SUBLANES = 8


def _split_hi_lo(x):
    hi = x.astype(_BF16)
    return hi, (x - hi.astype(_F32)).astype(_BF16)


def _split_hi_mid_lo(x):
    hi = x.astype(_BF16)
    r1 = x - hi.astype(_F32)
    mid = r1.astype(_BF16)
    return hi, mid, (r1 - mid.astype(_F32)).astype(_BF16)


def _dot_two_term(a, b):
    a_hi, a_lo = _split_hi_lo(a)
    b_hi, b_lo = _split_hi_lo(b)
    dot = lambda x, y: jnp.dot(x, y, preferred_element_type=_F32)
    return dot(a_hi, b_hi) + dot(a_hi, b_lo) + dot(a_lo, b_hi)


def _unit_lower_inverse(at, n_rows):
    c = at.shape[0]
    sub = lax.broadcasted_iota(jnp.int32, (SUBLANES, c), 0)
    lane = lax.broadcasted_iota(jnp.int32, (SUBLANES, c), 1)
    slabs = [jnp.where(lane == sub + SUBLANES * r, 1.0, 0.0) for r in range(c // SUBLANES)]
    for i in range(1, n_rows):
        n_slab = (i + SUBLANES - 1) // SUBLANES
        acc = at[0:SUBLANES, i:i + 1] * slabs[0]
        for r in range(1, n_slab):
            acc = acc + at[r * SUBLANES:(r + 1) * SUBLANES, i:i + 1] * slabs[r]
        row = jnp.where(lane[0:1] == i, 1.0, 0.0) - jnp.sum(acc, axis=0, keepdims=True)
        r_i = i // SUBLANES
        slabs[r_i] = jnp.where(sub == i % SUBLANES, row, slabs[r_i])
    return slabs


def _gdn_kernel(x_ref, z_ref, gb_ref, buf_ref, s0_ref, convw_ref, alog_ref, dtb_ref, norm_ref,
                o_ref, sfin_ref, bufout_ref, state_ref, tail_ref, *, rows, n_chunks):
    c_idx = pl.program_id(1)
    C = GDN_CHUNK
    valid = rows

    @pl.when(c_idx == 0)
    def _():
        state_ref[...] = s0_ref[0]
        tail_ref[...] = jnp.zeros_like(tail_ref)
        tail_ref[SUBLANES - (GDN_CONV - 1):SUBLANES, :] = buf_ref[0]

    pad = lambda a: a if rows == C else jnp.concatenate([a, jnp.zeros((C - rows, a.shape[1]), a.dtype)], axis=0)
    x = pad(x_ref[...])
    ext = jnp.concatenate([tail_ref[...], x], axis=0)
    w = convw_ref[...]
    y = ext[SUBLANES:SUBLANES + C] * w[GDN_CONV - 1:GDN_CONV]
    for i in range(1, GDN_CONV):
        y = y + ext[SUBLANES - i:SUBLANES - i + C] * w[GDN_CONV - 1 - i:GDN_CONV - i]
    y = y * jax.nn.sigmoid(y)
    tail_ref[...] = x[C - SUBLANES:C]

    @pl.when(c_idx == n_chunks - 1)
    def _():
        bufout_ref[0] = ext[SUBLANES + valid - (GDN_CONV - 1):SUBLANES + valid]

    row_i = lax.broadcasted_iota(jnp.int32, (C, 1), 0)
    live = jnp.where(row_i < valid, 1.0, 0.0)
    gb_in = pad(gb_ref[...])
    lane = lax.broadcasted_iota(jnp.int32, (C, LANES), 1)
    t = gb_in + dtb_ref[...]
    softplus = jnp.maximum(t, 0.0) + jnp.log(1.0 + jnp.exp(-jnp.abs(t)))
    g_all = jnp.where(lane < GDN_HEADS, -jnp.exp(alog_ref[...]) * softplus, 0.0) * live
    beta_all = jnp.where((lane >= GDN_HEADS) & (lane < 2 * GDN_HEADS), jax.nn.sigmoid(gb_in), 0.0) * live
    ii = lax.broadcasted_iota(jnp.int32, (C, C), 0)
    jj = lax.broadcasted_iota(jnp.int32, (C, C), 1)
    tri = jnp.where(ii >= jj, 1.0, 0.0).astype(_BF16)
    gc_all = sum(jnp.dot(tri, part, preferred_element_type=_F32)
                 for part in _split_hi_mid_lo(g_all))
    mix = gc_all + beta_all
    eye = jnp.where(lax.broadcasted_iota(jnp.int32, (SUBLANES, LANES), 0)
                    == lax.broadcasted_iota(jnp.int32, (SUBLANES, LANES), 1), 1.0, 0.0).astype(_BF16)
    nt = (((1,), (1,)), ((), ()))
    mix_rows = sum(lax.dot_general(eye, part, nt, preferred_element_type=_F32)
                   for part in _split_hi_mid_lo(mix))
    scale = GDN_DK ** -0.5
    for h in range(GDN_HEADS):
        q = y[:, h * GDN_DK:(h + 1) * GDN_DK]
        k = y[:, A_QK + h * GDN_DK:A_QK + (h + 1) * GDN_DK]
        v = y[:, 2 * A_QK + h * GDN_DV:2 * A_QK + (h + 1) * GDN_DV] * live
        q = q * lax.rsqrt(jnp.sum(q * q, axis=-1, keepdims=True) + 1e-6) * (live * scale)
        k = k * lax.rsqrt(jnp.sum(k * k, axis=-1, keepdims=True) + 1e-6) * live
        g_col, b_col = gc_all[:, h:h + 1], beta_all[:, GDN_HEADS + h:GDN_HEADS + h + 1]
        g_row, b_row = mix_rows[h:h + 1, :], mix_rows[GDN_HEADS + h:GDN_HEADS + h + 1, :]
        k_bf = k.astype(_BF16)
        kk = lax.dot_general(k_bf, k_bf, nt, preferred_element_type=_F32)
        qk = lax.dot_general(q.astype(_BF16), k_bf, nt, preferred_element_type=_F32)
        qk = qk * jnp.exp(jnp.where(ii >= jj, g_col - g_row, NEG_BIG))
        at = b_row * kk * jnp.exp(jnp.where(jj > ii, g_row - g_col, NEG_BIG))
        t_inv = jnp.concatenate(_unit_lower_inverse(at, valid), axis=0)
        rhs = jnp.concatenate([v * b_col, k * (b_col * jnp.exp(g_col))], axis=1)
        sol = _dot_two_term(t_inv, rhs)
        u, w_ = sol[:, :GDN_DV], sol[:, GDN_DV:]
        s = state_ref[h]
        s_bf = s.astype(_BF16)
        v_new = u - jnp.dot(w_.astype(_BF16), s_bf, preferred_element_type=_F32)
        v_new_bf = v_new.astype(_BF16)
        o = (jnp.dot((q * jnp.exp(g_col)).astype(_BF16), s_bf, preferred_element_type=_F32)
             + jnp.dot(qk.astype(_BF16), v_new_bf, preferred_element_type=_F32))
        g_last = g_col[C - 1:C, :]
        k_dec = (k * jnp.exp(g_last - g_col)).astype(_BF16)
        state_ref[h] = s * jnp.exp(g_last) + lax.dot_general(k_dec, v_new_bf, (((0,), (0,)), ((), ())),
                                                             preferred_element_type=_F32)
        o = o * lax.rsqrt(jnp.mean(o * o, axis=-1, keepdims=True) + RMS_EPS) * norm_ref[...]
        zh = pad(z_ref[:, h * GDN_DV:(h + 1) * GDN_DV])
        o = o * (zh * jax.nn.sigmoid(zh))
        o_ref[:, h * GDN_DV:(h + 1) * GDN_DV] = o[:rows]

    @pl.when(c_idx == n_chunks - 1)
    def _():
        sfin_ref[0] = state_ref[...]


def _gdn(qkv, z, gates, conv_buf, s0, conv_w, a_log, dt_bias, gdn_norm, *, n_batch, seq, row0):
    rows = min(GDN_CHUNK, seq)
    assert seq % rows == 0 and row0 % rows == 0 and (rows == GDN_CHUNK or seq == rows) and rows % SUBLANES == 0
    n_chunks = seq // rows
    r0 = row0 // rows
    row_map = lambda b, c: (r0 + b * n_chunks + c, 0)
    const = lambda b, c: (0, 0)
    lane_pad = lambda vec: jnp.zeros((1, LANES), _F32).at[0, :vec.shape[0]].set(vec)
    return pl.pallas_call(
        functools.partial(_gdn_kernel, rows=rows, n_chunks=n_chunks),
        grid=(n_batch, n_chunks),
        in_specs=[pl.BlockSpec((rows, A_CONV_CH), row_map),
                  pl.BlockSpec((rows, A_WIDTH), row_map),
                  pl.BlockSpec((rows, LANES), row_map),
                  pl.BlockSpec((1, GDN_CONV - 1, A_CONV_CH), lambda b, c: (b, 0, 0)),
                  pl.BlockSpec((1, GDN_HEADS, GDN_DK, GDN_DV), lambda b, c: (b, 0, 0, 0)),
                  pl.BlockSpec((GDN_CONV, A_CONV_CH), const),
                  pl.BlockSpec((1, LANES), const),
                  pl.BlockSpec((1, LANES), const),
                  pl.BlockSpec((1, GDN_DV), const)],
        out_specs=[pl.BlockSpec((rows, A_WIDTH), lambda b, c: (b * n_chunks + c, 0)),
                   pl.BlockSpec((1, GDN_HEADS, GDN_DK, GDN_DV), lambda b, c: (b, 0, 0, 0)),
                   pl.BlockSpec((1, GDN_CONV - 1, A_CONV_CH), lambda b, c: (b, 0, 0))],
        out_shape=[jax.ShapeDtypeStruct((n_batch * seq, A_WIDTH), _F32),
                   jax.ShapeDtypeStruct((n_batch, GDN_HEADS, GDN_DK, GDN_DV), _F32),
                   jax.ShapeDtypeStruct((n_batch, GDN_CONV - 1, A_CONV_CH), _F32)],
        scratch_shapes=[pltpu.VMEM((GDN_HEADS, GDN_DK, GDN_DV), _F32), pltpu.VMEM((SUBLANES, A_CONV_CH), _F32)],
        compiler_params=_params(2),
        name="gdn",
    )(qkv, z, gates, conv_buf, s0, conv_w, lane_pad(a_log), lane_pad(dt_bias), gdn_norm.reshape(1, GDN_DV))


def _stack_layers(states):
    return [jnp.stack(a) for a in zip(*states)]


def kernel(x_prompt, x_sample, state_gdn_s, state_gdn_conv, cache_fox_k, cache_fox_v, cache_fox_logf, cache_nsa_cmp_k, cache_nsa_cmp_v, cache_nsa_sel_k, cache_nsa_sel_v, cache_nsa_win_k, cache_nsa_win_v, cache_mem_k, cache_mem_v, page_table, mem_prompt, norm_mix, norm_xattn, norm_mem, norm_ffn, norm_final, w_in_even, b_forget, gdn_conv_w, gdn_a_log, gdn_dt_bias, gdn_norm, w_out_even, w_in_odd, nsa_gate_bias, nsa_cmp_pos, nsa_cmp_w1, nsa_cmp_w2, w_out_odd, w_mem_q, w_mem_kv, w_mem_o, w_ffn_in, w_ffn_out):
    B, L, D = x_prompt.shape
    SB, SL, _ = x_sample.shape
    depth = norm_mix.shape[0]
    n_p = B * L
    n_s = SB * SL
    past_len = page_table.shape[1] * cache_fox_k.shape[2]
    pos_p = jnp.arange(L, dtype=jnp.int32)
    pos_s = past_len + jnp.arange(SL, dtype=jnp.int32)
    ffn_hidden = w_ffn_out.shape[1]
    row_tile = min(ROW_TILE, L)
    attn_tile = min(ATTN_TILE, L)
    mem_width = MEM_HEADS * MEM_DH
    keep = min(WINDOW, L)

    def rows_p(a, *shape):
        return a[:n_p].reshape(B, L, *shape)

    def rows_s(a, *shape):
        return a[n_p:].reshape(SB, SL, *shape)

    n_pool, page = cache_fox_k.shape[1:3]
    w_buf = cache_nsa_win_k.shape[2]
    fox_k_pages, fox_v_pages = _pages_by_head(cache_fox_k), _pages_by_head(cache_fox_v)
    fox_c_pool = jnp.transpose(jnp.cumsum(cache_fox_logf, axis=2), (0, 1, 3, 2)).reshape(-1, FOX_HEADS, page)
    sel_k_pages, sel_v_pages = _pages_by_head(cache_nsa_sel_k), _pages_by_head(cache_nsa_sel_v)
    cmp_k_pages, cmp_v_pages = _pages_by_head(cache_nsa_cmp_k), _pages_by_head(cache_nsa_cmp_v)
    win_k_pages, win_v_pages = _pages_by_head(cache_nsa_win_k), _pages_by_head(cache_nsa_win_v)
    mem_k_rows = cache_mem_k.reshape(-1, MEM_HEADS * MEM_DH)
    mem_v_rows = cache_mem_v.reshape(-1, MEM_HEADS * MEM_DH)

    x = jnp.concatenate([x_prompt.reshape(n_p, D), x_sample.reshape(n_s, D)], axis=0)
    mem_flat = mem_prompt.reshape(B * MEM_TOKENS, D)
    even_p, even_s, odd_p, odd_s, mem_p = [], [], [], [], []
    for layer in range(depth):
        if layer % 2 == 0:
            e = layer // 2
            w = w_in_even[e]
            gate_cols = A_CONV_CH + 2 * GDN_HEADS
            w_perm = jnp.concatenate(
                [w[:, :A_CONV_CH], w[:, gate_cols:gate_cols + A_WIDTH + 3 * B_WIDTH], w[:, A_CONV_CH:gate_cols],
                 w[:, gate_cols + A_WIDTH + 3 * B_WIDTH:],
                 jnp.zeros((D, LANES - 2 * GDN_HEADS - FOX_HEADS), w.dtype)], axis=1).astype(_BF16)
            qkv_a, z, q_b, k_b, v_b, small = _rms_matmul_split(
                x, norm_mix[layer], w_perm, [A_CONV_CH, A_WIDTH, B_WIDTH, B_WIDTH, B_WIDTH, LANES], row_tile)
            logf = jax.nn.log_sigmoid(small[:, 2 * GDN_HEADS:2 * GDN_HEADS + FOX_HEADS] + b_forget[e])
            gdn_w = (gdn_conv_w[e], gdn_a_log[e], gdn_dt_bias[e], gdn_norm[e])
            c_rows = jnp.transpose(jnp.cumsum(rows_p(logf, FOX_HEADS), axis=1), (0, 2, 1))
            o_b_p = _fox_prompt(q_b, k_b, v_b, c_rows, n_batch=B, seq=L, tile=attn_tile)
            o_a_p, s_p, buf_p = _gdn(qkv_a, z, small, jnp.zeros((B, GDN_CONV - 1, A_CONV_CH), _F32),
                                     jnp.zeros((B, GDN_HEADS, GDN_DK, GDN_DV), _F32), *gdn_w,
                                     n_batch=B, seq=L, row0=0)
            even_p.append((s_p, buf_p, rows_p(k_b, FOX_HEADS, FOX_DH), rows_p(v_b, FOX_HEADS, FOX_DH),
                           rows_p(logf, FOX_HEADS)))
            c_new = jnp.transpose(jnp.cumsum(rows_s(logf, FOX_HEADS), axis=1), (0, 2, 1))
            c_new = jnp.pad(c_new, ((0, 0), (0, 0), (0, LANES - SL)))
            o_b_s = _fox_sample(q_b, k_b, v_b, c_new, fox_k_pages, fox_v_pages, fox_c_pool, page_table,
                                row0=n_p, sl=SL, page0=e * n_pool)
            o_a_s, s_s, buf_s = _gdn(qkv_a, z, small, state_gdn_conv[e], state_gdn_s[e], *gdn_w,
                                     n_batch=SB, seq=SL, row0=n_p)
            even_s.append((s_s, buf_s, rows_s(k_b, FOX_HEADS, FOX_DH), rows_s(v_b, FOX_HEADS, FOX_DH),
                           rows_s(logf, FOX_HEADS)))
            o_a = jnp.concatenate([o_a_p, o_a_s], axis=0)
            o_b = jnp.concatenate([o_b_p, o_b_s], axis=0)
            w_out = w_out_even[e].astype(_BF16)
            x = _matmul_residual([o_a, o_b], [w_out[:A_WIDTH], w_out[A_WIDTH:]], x, row_tile)
        else:
            o = layer // 2
            q, qr, kc, vc, ks, vs, kw, vw, gates = _odd_proj(
                x, norm_mix[layer], w_in_odd[o], nsa_gate_bias[o], pos_p, pos_s, n_p, row_tile)
            cmp_w = (nsa_cmp_pos[o], nsa_cmp_w1[o], nsa_cmp_w2[o])
            o_cmp, sel = _nsa_cmp_prompt(q, gates, kc, vc, *cmp_w, n_batch=B, seq=L, tile=attn_tile)
            o_p = _nsa_selwin_prompt(qr, gates, sel, o_cmp, ks, vs, kw, vw, n_batch=B, seq=L, tile=attn_tile)
            kv4 = (NSA_KV_HEADS, NSA_DH)
            odd_p.append((rows_p(kc, *kv4), rows_p(vc, *kv4), rows_p(ks, *kv4), rows_p(vs, *kv4),
                          rows_p(kw, *kv4)[:, L - keep:], rows_p(vw, *kv4)[:, L - keep:]))
            o_s = _nsa_sample(q, qr, gates, ks, vs, kw, vw, win_k_pages, win_v_pages,
                              cmp_k_pages, cmp_v_pages, sel_k_pages, sel_v_pages, page_table, *cmp_w,
                              row0=n_p, sl=SL, page0=o * n_pool, win_block0=o * SB)
            new_wk = jnp.concatenate([cache_nsa_win_k[o], rows_s(kw, *kv4)], axis=1)[:, SL:]
            new_wv = jnp.concatenate([cache_nsa_win_v[o], rows_s(vw, *kv4)], axis=1)[:, SL:]
            odd_s.append((rows_s(kc, *kv4), rows_s(vc, *kv4), rows_s(ks, *kv4), rows_s(vs, *kv4), new_wk, new_wv))
            o_all = jnp.concatenate([o_p, o_s], axis=0)
            x = _matmul_residual([o_all], [w_out_odd[o].astype(_BF16)], x, row_tile)

        mkv = _rms_matmul(mem_flat, norm_mem[layer], w_mem_kv[layer].astype(_BF16), row_tile=min(ROW_TILE, B * MEM_TOKENS))
        mem_p.append((mkv[:, :mem_width].reshape(B, MEM_TOKENS, MEM_HEADS, MEM_DH),
                      mkv[:, mem_width:].reshape(B, MEM_TOKENS, MEM_HEADS, MEM_DH)))
        q = _rms_matmul(x, norm_xattn[layer], w_mem_q[layer].astype(_BF16), row_tile=row_tile)
        o_p = _mem_attn(q, mkv, mkv, n_batch=B, q_len=L, q_row0=0, q_tile=row_tile, k_col_block=0, v_col_block=1)
        o_s = _mem_attn(q, mem_k_rows, mem_v_rows, n_batch=SB, q_len=SL, q_row0=n_p, q_tile=SL,
                        k_col_block=0, v_col_block=0, kv_block0=layer * SB)
        o_mem = jnp.concatenate([o_p.astype(_F32), o_s], axis=0)
        x = _matmul_residual([o_mem], [w_mem_o[layer].astype(_BF16)], x, row_tile)

        x = _ffn(x, norm_ffn[layer], w_ffn_in[layer][:, :ffn_hidden].astype(_BF16),
                 w_ffn_in[layer][:, ffn_hidden:].astype(_BF16), w_ffn_out[layer].astype(_BF16), row_tile)

    y = _rms(x, norm_final, row_tile)
    y_prompt = y[:n_p].reshape(B, L, D)
    y_sample = y[n_p:].reshape(SB, SL, D)
    p_gdn_s, p_gdn_conv, p_fox_k, p_fox_v, p_fox_logf = _stack_layers(even_p)
    s_gdn_s, s_gdn_conv, s_fox_k, s_fox_v, s_fox_logf = _stack_layers(even_s)
    p_cmp_k, p_cmp_v, p_sel_k, p_sel_v, p_win_k, p_win_v = _stack_layers(odd_p)
    s_cmp_k, s_cmp_v, s_sel_k, s_sel_v, s_win_k, s_win_v = _stack_layers(odd_s)
    p_mem_k, p_mem_v = _stack_layers(mem_p)
    return (y_prompt, y_sample,
            p_gdn_s, p_gdn_conv, p_fox_k, p_fox_v, p_fox_logf,
            p_cmp_k, p_cmp_v, p_sel_k, p_sel_v, p_win_k, p_win_v, p_mem_k, p_mem_v,
            s_gdn_s, s_gdn_conv, s_fox_k, s_fox_v, s_fox_logf,
            s_cmp_k, s_cmp_v, s_sel_k, s_sel_v, s_win_k, s_win_v)
```

```python
import functools

import jax
import jax.numpy as jnp
import numpy as np
from jax import lax
from jax.experimental import pallas as pl
from jax.experimental.pallas import tpu as pltpu

D_MODEL = 1024
RMS_EPS = 1e-6
NEG_BIG = -1e30

GDN_HEADS = 4
GDN_DK = 128
GDN_DV = 128
GDN_CONV = 4
GDN_CHUNK = 64
A_QK = GDN_HEADS * GDN_DK
A_WIDTH = GDN_HEADS * GDN_DV
A_CONV_CH = 2 * A_QK + A_WIDTH

FOX_HEADS = 8
FOX_DH = 64
B_WIDTH = FOX_HEADS * FOX_DH

NSA_HEADS = 16
NSA_KV_HEADS = 4
NSA_GROUP = NSA_HEADS // NSA_KV_HEADS
NSA_DH = 64
C_WIDTH = NSA_HEADS * NSA_DH
C_KV_WIDTH = NSA_KV_HEADS * NSA_DH
CMP_LEN = 32
CMP_STRIDE = CMP_LEN // 2
SEL_BLOCK = 64
SEL_TOP_N = 16
WINDOW = 512

ROPE_THETA = 500000.0
ROT_DIM = NSA_DH // 4

MEM_TOKENS = 256
MEM_HEADS = 4
MEM_DH = D_MODEL // MEM_HEADS

LANES = 128
VMEM_LIMIT_BYTES = 56 * 1024 * 1024
ROW_TILE = 512
ATTN_TILE = 256

_BF16 = jnp.bfloat16
_F32 = jnp.float32


def _params(n_grid_dims):
    return pltpu.CompilerParams(
        dimension_semantics=("arbitrary",) * n_grid_dims,
        vmem_limit_bytes=VMEM_LIMIT_BYTES)


def _rms_rows(x, g):
    return x * lax.rsqrt(jnp.mean(x * x, axis=-1, keepdims=True) + RMS_EPS) * g


def _rms_matmul_kernel(x_ref, g_ref, w_ref, o_ref, *, col_chunk):
    xn = _rms_rows(x_ref[...], g_ref[...]).astype(_BF16)
    width = w_ref.shape[1]
    for c0 in range(0, width, col_chunk):
        c1 = min(width, c0 + col_chunk)
        o_ref[:, c0:c1] = jnp.dot(xn, w_ref[:, c0:c1], preferred_element_type=_F32).astype(o_ref.dtype)


def _rms_matmul(x, g, w_bf16, out_dtype=_F32, row_tile=ROW_TILE):
    n, d = x.shape
    width = w_bf16.shape[1]
    assert n % row_tile == 0 and width % LANES == 0
    return pl.pallas_call(
        functools.partial(_rms_matmul_kernel, col_chunk=512),
        grid=(n // row_tile,),
        in_specs=[pl.BlockSpec((row_tile, d), lambda i: (i, 0)),
                  pl.BlockSpec((1, d), lambda i: (0, 0)),
                  pl.BlockSpec((d, width), lambda i: (0, 0))],
        out_specs=pl.BlockSpec((row_tile, width), lambda i: (i, 0)),
        out_shape=jax.ShapeDtypeStruct((n, width), out_dtype),
        compiler_params=_params(1),
        name="rms_matmul",
    )(x, g.reshape(1, d), w_bf16)


def _matmul_residual_kernel(*refs, n_in, n_p_tiles):
    r_ref, o_ref = refs[3 * n_in], refs[3 * n_in + 1]
    is_prompt = pl.program_id(0) < n_p_tiles
    acc = r_ref[...]
    for idx in range(n_in):
        ap_ref, as_ref, w_ref = refs[3 * idx:3 * idx + 3]
        a = jnp.where(is_prompt, ap_ref[...].astype(_BF16), as_ref[...].astype(_BF16))
        acc = acc + jnp.dot(a, w_ref[...], preferred_element_type=_F32)
    o_ref[...] = acc


def _matmul_residual(a_pairs, w_list, resid, row_tile=ROW_TILE):
    n, d = resid.shape
    n_in = len(a_pairs)
    n_p = a_pairs[0][0].shape[0]
    assert n % row_tile == 0 and n_p % row_tile == 0 and all(ap.shape[0] == n_p and ap.shape[0] + a_s.shape[0] == n
                                                             for ap, a_s in a_pairs)
    n_p_tiles = n_p // row_tile
    in_specs, args = [], []
    for (ap, a_s), w in zip(a_pairs, w_list):
        k = ap.shape[1]
        in_specs += [pl.BlockSpec((row_tile, k), lambda i: (jnp.minimum(i, n_p_tiles - 1), 0)),
                     pl.BlockSpec((row_tile, k), lambda i: (jnp.maximum(i - n_p_tiles, 0), 0)),
                     pl.BlockSpec(w.shape, lambda i: (0, 0))]
        args += [ap, a_s, w]
    return pl.pallas_call(
        functools.partial(_matmul_residual_kernel, n_in=n_in, n_p_tiles=n_p_tiles),
        grid=(n // row_tile,),
        in_specs=in_specs + [pl.BlockSpec((row_tile, d), lambda i: (i, 0))],
        out_specs=pl.BlockSpec((row_tile, d), lambda i: (i, 0)),
        out_shape=jax.ShapeDtypeStruct((n, d), _F32),
        input_output_aliases={3 * n_in: 0},
        compiler_params=_params(1),
        name="matmul_residual",
    )(*args, resid)


def _ffn_kernel(x_ref, g_ref, wg_ref, wu_ref, wo_ref, o_ref, *, chunk):
    x = x_ref[...]
    xn = _rms_rows(x, g_ref[...]).astype(_BF16)
    o_ref[...] = x
    hidden = wg_ref.shape[1]
    for c0 in range(0, hidden, chunk):
        gate = jnp.dot(xn, wg_ref[:, c0:c0 + chunk], preferred_element_type=_F32)
        up = jnp.dot(xn, wu_ref[:, c0:c0 + chunk], preferred_element_type=_F32)
        h = (gate * jax.nn.sigmoid(gate) * up).astype(_BF16)
        o_ref[...] += jnp.dot(h, wo_ref[c0:c0 + chunk, :], preferred_element_type=_F32)


def _ffn(x, g, wg, wu, wo, row_tile=ROW_TILE):
    n, d = x.shape
    hidden = wg.shape[1]
    chunk = 256
    assert n % row_tile == 0 and hidden % chunk == 0
    const = lambda i: (0, 0)
    return pl.pallas_call(
        functools.partial(_ffn_kernel, chunk=chunk),
        grid=(n // row_tile,),
        in_specs=[pl.BlockSpec((row_tile, d), lambda i: (i, 0)),
                  pl.BlockSpec((1, d), const),
                  pl.BlockSpec((d, hidden), const, pipeline_mode=pl.Buffered(1)),
                  pl.BlockSpec((d, hidden), const, pipeline_mode=pl.Buffered(1)),
                  pl.BlockSpec((hidden, d), const, pipeline_mode=pl.Buffered(1))],
        out_specs=pl.BlockSpec((row_tile, d), lambda i: (i, 0)),
        out_shape=jax.ShapeDtypeStruct((n, d), _F32),
        input_output_aliases={0: 0},
        compiler_params=_params(1),
        name="ffn",
    )(x, g.reshape(1, d), wg, wu, wo)


def _mem_attn_kernel(q_ref, k_ref, v_ref, o_ref):
    scale = MEM_DH ** -0.5
    for h in range(MEM_HEADS):
        cols = slice(h * MEM_DH, (h + 1) * MEM_DH)
        q = q_ref[:, cols].astype(_BF16)
        k = k_ref[:, cols].astype(_BF16)
        v = v_ref[:, cols].astype(_BF16)
        s = lax.dot_general(q, k, (((1,), (1,)), ((), ())), preferred_element_type=_F32) * scale
        p = jnp.exp(s - jnp.max(s, axis=-1, keepdims=True))
        inv = 1.0 / jnp.sum(p, axis=-1, keepdims=True)
        o = jnp.dot(p.astype(_BF16), v, preferred_element_type=_F32) * inv
        o_ref[:, cols] = o.astype(o_ref.dtype)


def _mem_attn(q, k, v, *, n_batch, q_len, q_row0, q_tile, k_col_block, v_col_block, kv_block0=0):
    assert q_len % q_tile == 0 and q_row0 % q_tile == 0
    tiles = q_len // q_tile
    t0 = q_row0 // q_tile
    width = MEM_HEADS * MEM_DH
    return pl.pallas_call(
        _mem_attn_kernel,
        grid=(n_batch, tiles),
        in_specs=[pl.BlockSpec((q_tile, width), lambda b, i: (t0 + b * tiles + i, 0)),
                  pl.BlockSpec((MEM_TOKENS, width), lambda b, i: (kv_block0 + b, k_col_block)),
                  pl.BlockSpec((MEM_TOKENS, width), lambda b, i: (kv_block0 + b, v_col_block))],
        out_specs=pl.BlockSpec((q_tile, width), lambda b, i: (b * tiles + i, 0)),
        out_shape=jax.ShapeDtypeStruct((n_batch * q_len, width), _BF16 if q_tile % 16 == 0 else _F32),
        compiler_params=_params(2),
        name="mem_attn",
    )(q, k, v)


def _rms_kernel(x_ref, g_ref, o_ref):
    o_ref[...] = _rms_rows(x_ref[...], g_ref[...])


def _rms(x, g, row_tile=ROW_TILE):
    n, d = x.shape
    return pl.pallas_call(
        _rms_kernel,
        grid=(n // row_tile,),
        in_specs=[pl.BlockSpec((row_tile, d), lambda i: (i, 0)),
                  pl.BlockSpec((1, d), lambda i: (0, 0))],
        out_specs=pl.BlockSpec((row_tile, d), lambda i: (i, 0)),
        out_shape=jax.ShapeDtypeStruct((n, d), _F32),
        compiler_params=_params(1),
        name="final_rms",
    )(x, g.reshape(1, d))


def _rms_matmul_split_kernel(x_ref, g_ref, w_ref, *o_refs, col_chunk):
    xn = _rms_rows(x_ref[...], g_ref[...]).astype(_BF16)
    c0 = 0
    for o_ref in o_refs:
        width = o_ref.shape[1]
        for s0 in range(0, width, col_chunk):
            s1 = min(width, s0 + col_chunk)
            o_ref[:, s0:s1] = jnp.dot(xn, w_ref[:, c0 + s0:c0 + s1], preferred_element_type=_F32)
        c0 += width


def _rms_matmul_split(x, g, w_bf16, widths, row_tile=ROW_TILE):
    n, d = x.shape
    assert n % row_tile == 0 and sum(widths) == w_bf16.shape[1] and all(w % LANES == 0 for w in widths)
    return pl.pallas_call(
        functools.partial(_rms_matmul_split_kernel, col_chunk=512),
        grid=(n // row_tile,),
        in_specs=[pl.BlockSpec((row_tile, d), lambda i: (i, 0)),
                  pl.BlockSpec((1, d), lambda i: (0, 0)),
                  pl.BlockSpec((d, w_bf16.shape[1]), lambda i: (0, 0))],
        out_specs=[pl.BlockSpec((row_tile, w), lambda i: (i, 0)) for w in widths],
        out_shape=[jax.ShapeDtypeStruct((n, w), _F32) for w in widths],
        compiler_params=_params(1),
        name="rms_matmul_split",
    )(x, g.reshape(1, d), w_bf16)


def _rope_apply(x, cos, sin_up, sin_dn):
    width = x.shape[1]
    reps = width // cos.shape[1]
    if reps > 1:
        cos, sin_up, sin_dn = [jnp.concatenate([t] * reps, axis=1) for t in (cos, sin_up, sin_dn)]
    return x * cos + pltpu.roll(x, width - ROT_DIM // 2, 1) * sin_up + pltpu.roll(x, ROT_DIM // 2, 1) * sin_dn


def _odd_proj_kernel(x_ref, g_ref, w_ref, gb_ref, cos_ref, sup_ref, sdn_ref,
                     q_ref, qr_ref, kc_ref, vc_ref, ks_ref, vs_ref, kw_ref, vw_ref, gate_ref):
    xn = _rms_rows(x_ref[...], g_ref[...]).astype(_BF16)
    cos, sup, sdn = cos_ref[...], sup_ref[...], sdn_ref[...]
    kvw = C_KV_WIDTH
    for h0 in range(0, C_WIDTH, 512):
        q = jnp.dot(xn, w_ref[:, h0:h0 + 512], preferred_element_type=_F32)
        q_ref[:, h0:h0 + 512] = q
        qr_ref[:, h0:h0 + 512] = _rope_apply(q, cos, sup, sdn)
    c0 = C_WIDTH
    for o_ref, rot in ((kc_ref, False), (vc_ref, False), (ks_ref, True), (vs_ref, False), (kw_ref, True), (vw_ref, False)):
        y = jnp.dot(xn, w_ref[:, c0:c0 + kvw], preferred_element_type=_F32)
        o_ref[...] = _rope_apply(y, cos, sup, sdn) if rot else y
        c0 += kvw
    gl = jnp.dot(xn, w_ref[:, c0:c0 + LANES], preferred_element_type=_F32)
    gate_ref[...] = jax.nn.sigmoid(gl + gb_ref[...])


def _rope_tables(pos):
    half = ROT_DIM // 2
    inv_freq = ROPE_THETA ** (-jnp.arange(half, dtype=_F32) / half)
    ang = pos.astype(_F32)[:, None] * inv_freq[None, :]
    cos, sin = jnp.cos(ang), jnp.sin(ang)
    rest = NSA_DH - ROT_DIM
    ones = jnp.ones((pos.shape[0], rest), _F32)
    zeros = jnp.zeros((pos.shape[0], rest), _F32)
    z8 = jnp.zeros_like(sin)
    head_cos = jnp.concatenate([cos, cos, ones], axis=1)
    head_up = jnp.concatenate([-sin, z8, zeros], axis=1)
    head_dn = jnp.concatenate([z8, sin, zeros], axis=1)
    return [jnp.tile(t, (1, NSA_KV_HEADS)) for t in (head_cos, head_up, head_dn)]


def _odd_proj(x, g, w_in, gate_bias, pos_p, pos_s, n_p, row_tile=ROW_TILE):
    n, d = x.shape
    L, SL = pos_p.shape[0], pos_s.shape[0]
    assert L % row_tile == 0 and n_p % row_tile == 0 and (n - n_p) % row_tile == 0 and row_tile % SL == 0
    head = np.arange(NSA_HEADS)
    gl_cols = np.concatenate([C_WIDTH + 6 * C_KV_WIDTH + head * 3 + br for br in range(3)])
    w = jnp.concatenate([w_in[:, :C_WIDTH + 6 * C_KV_WIDTH], w_in[:, gl_cols],
                         jnp.zeros((d, LANES - 3 * NSA_HEADS), w_in.dtype)], axis=1).astype(_BF16)
    gb = jnp.concatenate([gate_bias[gl_cols - (C_WIDTH + 6 * C_KV_WIDTH)], jnp.zeros((LANES - 3 * NSA_HEADS,), _F32)]).reshape(1, LANES)
    tables = _rope_tables(jnp.concatenate([pos_p, jnp.tile(pos_s, row_tile // SL)]))
    tiles_per_seq = L // row_tile
    n_p_tiles = n_p // row_tile
    tab_map = lambda i: (jnp.where(i < n_p_tiles, i % tiles_per_seq, tiles_per_seq), 0)
    row_map = lambda i: (i, 0)
    const = lambda i: (0, 0)
    widths = [C_WIDTH, C_WIDTH] + [C_KV_WIDTH] * 6 + [LANES]
    return pl.pallas_call(
        _odd_proj_kernel,
        grid=(n // row_tile,),
        in_specs=[pl.BlockSpec((row_tile, d), row_map),
                  pl.BlockSpec((1, d), const),
                  pl.BlockSpec((d, w.shape[1]), const),
                  pl.BlockSpec((1, LANES), const)]
                 + [pl.BlockSpec((row_tile, C_KV_WIDTH), tab_map)] * 3,
        out_specs=[pl.BlockSpec((row_tile, wd), row_map) for wd in widths],
        out_shape=[jax.ShapeDtypeStruct((n, wd), _F32) for wd in widths],
        compiler_params=_params(1),
        name="odd_proj",
    )(x, g.reshape(1, d), w, gb, *tables)


MASKED = -2e30


def _softmax_step(s, v_bf16, m, l, acc):
    m_new = jnp.maximum(m, jnp.max(s, axis=-1, keepdims=True))
    alpha = jnp.exp(m - m_new)
    p = jnp.exp(s - m_new)
    l = alpha * l + jnp.sum(p, axis=-1, keepdims=True)
    acc = alpha * acc + jnp.dot(p.astype(_BF16), v_bf16, preferred_element_type=_F32)
    return m_new, l, acc


def _fox_prompt_kernel(q_ref, k_ref, v_ref, c_ref, o_ref, *, tile):
    i = pl.program_id(1)
    scale = FOX_DH ** -0.5
    row = lax.broadcasted_iota(jnp.int32, (tile, tile), 0)
    col = lax.broadcasted_iota(jnp.int32, (tile, tile), 1)
    diag_bias = jnp.where(col <= row, 0.0, MASKED)
    for h in range(FOX_HEADS):
        cols = slice(h * FOX_DH, (h + 1) * FOX_DH)
        q = (q_ref[:, cols] * scale).astype(_BF16)

        def logits(j, cols=cols, q=q, h=h):
            k0 = pl.multiple_of(j * tile, tile)
            k = k_ref[pl.ds(k0, tile), cols].astype(_BF16)
            v = v_ref[pl.ds(k0, tile), cols].astype(_BF16)
            s = lax.dot_general(q, k, (((1,), (1,)), ((), ())), preferred_element_type=_F32)
            return s - c_ref[0, h:h + 1, pl.ds(k0, tile)], v

        def body(j, carry, logits=logits):
            s, v = logits(j)
            return _softmax_step(s, v, *carry)

        init = (jnp.full((tile, 1), NEG_BIG, _F32), jnp.zeros((tile, 1), _F32), jnp.zeros((tile, FOX_DH), _F32))
        carry = lax.fori_loop(0, i, body, init)
        s, v = logits(i)
        m, l, acc = _softmax_step(s + diag_bias, v, *carry)
        o_ref[:, cols] = acc / jnp.maximum(l, 1e-30)


def _fox_prompt(q, k, v, c_rows, *, n_batch, seq, tile):
    assert seq % tile == 0
    tiles = seq // tile
    return pl.pallas_call(
        functools.partial(_fox_prompt_kernel, tile=tile),
        grid=(n_batch, tiles),
        in_specs=[pl.BlockSpec((tile, B_WIDTH), lambda b, i: (b * tiles + i, 0)),
                  pl.BlockSpec((seq, B_WIDTH), lambda b, i: (b, 0)),
                  pl.BlockSpec((seq, B_WIDTH), lambda b, i: (b, 0)),
                  pl.BlockSpec((1, FOX_HEADS, seq), lambda b, i: (b, 0, 0))],
        out_specs=pl.BlockSpec((tile, B_WIDTH), lambda b, i: (b * tiles + i, 0)),
        out_shape=jax.ShapeDtypeStruct((n_batch * seq, B_WIDTH), _F32),
        compiler_params=_params(2),
        name="fox_prompt",
    )(q, k, v, c_rows)


def _split3_dot(x, w_bf16):
    hi = x.astype(_BF16)
    r1 = x - hi.astype(_F32)
    mid = r1.astype(_BF16)
    lo = (r1 - mid.astype(_F32)).astype(_BF16)
    dot = lambda a: jnp.dot(a, w_bf16, preferred_element_type=_F32)
    return dot(hi) + dot(mid) + dot(lo)


def _compress_into(x2_ref, n_sub, wpair_ref, w1_ref, w2_ref, pos_ref, out_ref, row_stride=2, pair_stride=1):
    pair_rows = []
    for p in range(NSA_KV_HEADS // 2):
        pieces = [x2_ref[pl.ds(r * row_stride + p * pair_stride, n_sub, stride=row_stride * CMP_STRIDE), :]
                  for r in range(CMP_STRIDE)]
        pair_rows.append(jnp.concatenate(pieces, axis=1))
    x = jnp.concatenate(pair_rows, axis=0).astype(_BF16)
    both = jnp.dot(x, wpair_ref[...], preferred_element_type=_F32)
    bias = jnp.dot(pos_ref[...].astype(_BF16), w1_ref[...], preferred_element_type=_F32)
    hid_w = 2 * NSA_DH
    for kv in range(NSA_KV_HEADS):
        p, s = divmod(kv, 2)
        rows = slice(p * n_sub, (p + 1) * n_sub)
        first = both[rows, s * hid_w:(s + 1) * hid_w]
        second = both[rows, (2 + s) * hid_w:(3 + s) * hid_w]
        pre = first + pltpu.roll(second, n_sub - 1, 0) + bias
        hid = (pre * jax.nn.sigmoid(pre)).astype(_BF16)
        out_ref[0:n_sub, kv * NSA_DH:(kv + 1) * NSA_DH] = jnp.dot(hid, w2_ref[...], preferred_element_type=_F32)


def _cmp_branch(q_ref, gate_ref, kcb_ref, vcb_ref, c2s_ref, o_ref, sel_ref, q_pos, n_sel):
    rows = q_ref.shape[0]
    nb = kcb_ref.shape[0]
    scale = NSA_DH ** -0.5
    blk_n = lax.broadcasted_iota(jnp.int32, (rows, nb), 1)
    cbias = jnp.where(blk_n * CMP_STRIDE + (CMP_LEN - 1) <= q_pos, 0.0, MASKED)
    lane = lax.broadcasted_iota(jnp.int32, (rows, LANES), 1)
    cur = q_pos // SEL_BLOCK
    causal = (lane <= cur) & (lane < n_sel)
    forced = (lane == 0) | (lane == cur) | (lane == cur - 1)
    for kv in range(NSA_KV_HEADS):
        kcols = slice(kv * NSA_DH, (kv + 1) * NSA_DH)
        kcb = kcb_ref[:, kcols].astype(_BF16)
        vcb = vcb_ref[:, kcols].astype(_BF16)
        p_sum = jnp.zeros((rows, nb), _F32)
        for g in range(NSA_GROUP):
            h = kv * NSA_GROUP + g
            hcols = slice(h * NSA_DH, (h + 1) * NSA_DH)
            q = (q_ref[:, hcols] * scale).astype(_BF16)
            s = lax.dot_general(q, kcb, (((1,), (1,)), ((), ())), preferred_element_type=_F32) + cbias
            m = jnp.maximum(jnp.max(s, axis=-1, keepdims=True), NEG_BIG)
            p = jnp.exp(s - m)
            p = p / jnp.maximum(jnp.sum(p, axis=-1, keepdims=True), 1e-30)
            p_sum = p_sum + p
            o = jnp.dot(p.astype(_BF16), vcb, preferred_element_type=_F32)
            o_ref[:, hcols] = o * gate_ref[:, h:h + 1]
        imp = _split3_dot(p_sum, c2s_ref[...])
        imp = jnp.where(causal, jnp.where(forced, -NEG_BIG, imp), NEG_BIG)
        ahead = jnp.zeros((rows, LANES), _F32)
        for j in range(n_sel):
            cj = imp[:, j:j + 1]
            after_j = jnp.where(lane[0:1] > j, 1.0, 0.0)
            ahead = ahead + jnp.where(cj > imp, 1.0, 0.0) + jnp.where(cj == imp, after_j, 0.0)
        sel = jnp.where((ahead < float(min(SEL_TOP_N, n_sel))) & (imp > 0.5 * NEG_BIG), 1.0, 0.0)
        sel_ref[:, kv * LANES:(kv + 1) * LANES] = sel


def _cmp_to_sel(nb):
    c_start = np.arange(nb)[:, None] * CMP_STRIDE
    s_start = np.arange(LANES)[None, :] * SEL_BLOCK
    shared = np.minimum(c_start + CMP_LEN, s_start + SEL_BLOCK) - np.maximum(c_start, s_start)
    return jnp.asarray(np.clip(shared, 0, None).astype(np.float32) / CMP_LEN, _BF16)


def _sel_expand(n_keys):
    return jnp.asarray((np.arange(n_keys)[None, :] // SEL_BLOCK == np.arange(LANES)[:, None]).astype(np.float32), _BF16)


def _nsa_cmp_prompt_kernel(q_ref, gate_ref, kc_ref, vc_ref, wpk_ref, w1k_ref, w2k_ref, posk_ref,
                           wpv_ref, w1v_ref, w2v_ref, posv_ref, c2s_ref, o_ref, sel_ref, kcb_ref, vcb_ref,
                           *, tile, n_sub, n_sel):
    i = pl.program_id(1)

    @pl.when(i == 0)
    def _():
        kcb_ref[...] = jnp.zeros_like(kcb_ref)
        vcb_ref[...] = jnp.zeros_like(vcb_ref)
        _compress_into(kc_ref, n_sub, wpk_ref, w1k_ref, w2k_ref, posk_ref, kcb_ref)
        _compress_into(vc_ref, n_sub, wpv_ref, w1v_ref, w2v_ref, posv_ref, vcb_ref)

    q_pos = i * tile + lax.broadcasted_iota(jnp.int32, (tile, 1), 0)
    _cmp_branch(q_ref, gate_ref, kcb_ref, vcb_ref, c2s_ref, o_ref, sel_ref, q_pos, n_sel)


def _cmp_weights(cmp_pos, cmp_w1, cmp_w2):
    out = []
    eye2 = jnp.eye(2, dtype=_F32)
    for idx in range(2):
        w1r = cmp_w1[idx].reshape(CMP_LEN, NSA_DH, 2 * NSA_DH)
        halves = [jnp.einsum('rdj,st->rsdtj', w1r[h * CMP_STRIDE:(h + 1) * CMP_STRIDE], eye2).reshape(
            CMP_STRIDE * 2 * NSA_DH, 4 * NSA_DH) for h in range(2)]
        out += [jnp.concatenate(halves, axis=1).astype(_BF16), cmp_w1[idx].astype(_BF16), cmp_w2[idx].astype(_BF16),
                cmp_pos[idx].reshape(1, CMP_LEN * NSA_DH)]
    return out


def _cmp_weight_specs(const):
    return [pl.BlockSpec((CMP_STRIDE * 2 * NSA_DH, 8 * NSA_DH), const),
            pl.BlockSpec((CMP_LEN * NSA_DH, 2 * NSA_DH), const),
            pl.BlockSpec((2 * NSA_DH, NSA_DH), const),
            pl.BlockSpec((1, CMP_LEN * NSA_DH), const)] * 2


def _nsa_cmp_prompt(q, gates, kc, vc, cmp_pos, cmp_w1, cmp_w2, *, n_batch, seq, tile):
    assert seq % tile == 0 and seq % SEL_BLOCK == 0
    tiles = seq // tile
    n_sub = seq // CMP_STRIDE
    nb = -(-n_sub // LANES) * LANES
    n_sel = seq // SEL_BLOCK
    row_map = lambda b, i: (b * tiles + i, 0)
    seq_map = lambda b, i: (b, 0)
    const = lambda b, i: (0, 0)
    return pl.pallas_call(
        functools.partial(_nsa_cmp_prompt_kernel, tile=tile, n_sub=n_sub, n_sel=n_sel),
        grid=(n_batch, tiles),
        in_specs=[pl.BlockSpec((tile, C_WIDTH), row_map),
                  pl.BlockSpec((tile, LANES), row_map),
                  pl.BlockSpec((2 * seq, LANES), seq_map),
                  pl.BlockSpec((2 * seq, LANES), seq_map)] + _cmp_weight_specs(const)
                 + [pl.BlockSpec((nb, LANES), const)],
        out_specs=[pl.BlockSpec((tile, C_WIDTH), row_map),
                   pl.BlockSpec((tile, NSA_KV_HEADS * LANES), row_map)],
        out_shape=[jax.ShapeDtypeStruct((n_batch * seq, C_WIDTH), _F32),
                   jax.ShapeDtypeStruct((n_batch * seq, NSA_KV_HEADS * LANES), _F32)],
        scratch_shapes=[pltpu.VMEM((nb, C_KV_WIDTH), _F32), pltpu.VMEM((nb, C_KV_WIDTH), _F32)],
        compiler_params=_params(2),
        name="nsa_cmp_prompt",
    )(q, gates, kc.reshape(-1, LANES), vc.reshape(-1, LANES), *_cmp_weights(cmp_pos, cmp_w1, cmp_w2), _cmp_to_sel(nb))


def _nsa_selwin_prompt_kernel(qr_ref, gate_ref, sel_ref, ocmp_ref, ks_ref, vs_ref, kw_ref, vw_ref, exp_ref,
                              o_ref, selbias_ref, *, tile):
    i = pl.program_id(1)
    scale = NSA_DH ** -0.5
    q0 = i * tile
    row = lax.broadcasted_iota(jnp.int32, (tile, tile), 0) + q0
    col = lax.broadcasted_iota(jnp.int32, (tile, tile), 1)
    rows4 = NSA_GROUP * tile
    init = (jnp.full((rows4, 1), NEG_BIG, _F32), jnp.zeros((rows4, 1), _F32), jnp.zeros((rows4, NSA_DH), _F32))
    first_win = jnp.maximum(i - (WINDOW + tile - 1) // tile, 0)
    for kv in range(NSA_KV_HEADS):
        kcols = slice(kv * NSA_DH, (kv + 1) * NSA_DH)
        picked = jnp.dot(sel_ref[:, kv * LANES:(kv + 1) * LANES].astype(_BF16), exp_ref[...], preferred_element_type=_F32)
        selbias_ref[...] = (1.0 - picked) * MASKED
        q = jnp.concatenate(
            [(qr_ref[:, (kv * NSA_GROUP + g) * NSA_DH:(kv * NSA_GROUP + g + 1) * NSA_DH] * scale).astype(_BF16)
             for g in range(NSA_GROUP)], axis=0)

        def sel_body(j, carry, q=q, kcols=kcols):
            k0 = pl.multiple_of(j * tile, tile)
            k = ks_ref[pl.ds(k0, tile), kcols].astype(_BF16)
            v = vs_ref[pl.ds(k0, tile), kcols].astype(_BF16)
            bias = selbias_ref[:, pl.ds(k0, tile)] + jnp.where(col + k0 <= row, 0.0, MASKED)
            s = lax.dot_general(q, k, (((1,), (1,)), ((), ())), preferred_element_type=_F32)
            return _softmax_step(s + jnp.concatenate([bias] * NSA_GROUP, axis=0), v, *carry)

        def win_body(j, carry, q=q, kcols=kcols):
            k0 = pl.multiple_of(j * tile, tile)
            k = kw_ref[pl.ds(k0, tile), kcols].astype(_BF16)
            v = vw_ref[pl.ds(k0, tile), kcols].astype(_BF16)
            dist = row - (col + k0)
            bias = jnp.where((dist >= 0) & (dist < WINDOW), 0.0, MASKED)
            s = lax.dot_general(q, k, (((1,), (1,)), ((), ())), preferred_element_type=_F32)
            return _softmax_step(s + jnp.concatenate([bias] * NSA_GROUP, axis=0), v, *carry)

        _, l_s, acc_s = lax.fori_loop(0, i + 1, sel_body, init)
        _, l_w, acc_w = lax.fori_loop(first_win, i + 1, win_body, init)
        o_s = acc_s / jnp.maximum(l_s, 1e-30)
        o_w = acc_w / jnp.maximum(l_w, 1e-30)
        for g in range(NSA_GROUP):
            h = kv * NSA_GROUP + g
            hcols = slice(h * NSA_DH, (h + 1) * NSA_DH)
            rws = slice(g * tile, (g + 1) * tile)
            o_ref[:, hcols] = (ocmp_ref[:, hcols]
                               + o_s[rws] * gate_ref[:, NSA_HEADS + h:NSA_HEADS + h + 1]
                               + o_w[rws] * gate_ref[:, 2 * NSA_HEADS + h:2 * NSA_HEADS + h + 1])


def _nsa_selwin_prompt(qr, gates, sel, o_cmp, ks, vs, kw, vw, *, n_batch, seq, tile):
    assert seq % tile == 0
    tiles = seq // tile
    row_map = lambda b, i: (b * tiles + i, 0)
    seq_map = lambda b, i: (b, 0)
    return pl.pallas_call(
        functools.partial(_nsa_selwin_prompt_kernel, tile=tile),
        grid=(n_batch, tiles),
        in_specs=[pl.BlockSpec((tile, C_WIDTH), row_map),
                  pl.BlockSpec((tile, LANES), row_map),
                  pl.BlockSpec((tile, NSA_KV_HEADS * LANES), row_map),
                  pl.BlockSpec((tile, C_WIDTH), row_map)]
                 + [pl.BlockSpec((seq, C_KV_WIDTH), seq_map)] * 4
                 + [pl.BlockSpec((LANES, seq), lambda b, i: (0, 0))],
        out_specs=pl.BlockSpec((tile, C_WIDTH), row_map),
        out_shape=jax.ShapeDtypeStruct((n_batch * seq, C_WIDTH), _F32),
        scratch_shapes=[pltpu.VMEM((tile, seq), _F32)],
        compiler_params=_params(2),
        name="nsa_selwin_prompt",
    )(qr, gates, sel, o_cmp, ks, vs, kw, vw, _sel_expand(seq))


def _page_specs(block, n_pages, page0=0):
    return [pl.BlockSpec(block, functools.partial(lambda b, pt, p: (page0 + pt[b, p],) + (0,) * (len(block) - 1), p=p))
            for p in range(n_pages)]


def _pad_rows_to(x, rows):
    return jnp.concatenate([x, jnp.zeros((rows - x.shape[0], x.shape[1]), x.dtype)], axis=0)


def _pages_by_head(pool):
    e, n_pool, page, heads, dh = pool.shape
    return jnp.transpose(pool, (0, 1, 3, 4, 2)).reshape(e * n_pool, heads, dh, page)


_NN = (((1,), (0,)), ((), ()))
_NT = (((1,), (1,)), ((), ()))


def _fox_sample_kernel(pt_ref, q_ref, kn_ref, vn_ref, cn_ref, *refs, n_pages, page):
    del pt_ref
    k_refs, v_refs, c_refs = refs[:n_pages], refs[n_pages:2 * n_pages], refs[2 * n_pages:3 * n_pages]
    o_ref, kall_ref, vall_ref, call_ref = refs[3 * n_pages:]
    sl = q_ref.shape[0]
    carry = jnp.zeros((FOX_HEADS, 1), _F32)
    for p in range(n_pages):
        kall_ref[:, :, p * page:(p + 1) * page] = k_refs[p][0].astype(_BF16)
        vall_ref[:, :, p * page:(p + 1) * page] = v_refs[p][0].astype(_BF16)
        c_page = c_refs[p][0] + carry
        call_ref[:, p * page:(p + 1) * page] = c_page
        carry = c_page[:, page - 1:page]
    c_new = cn_ref[0] + carry
    lane = lax.broadcasted_iota(jnp.int32, (sl, LANES), 1)
    q_idx = lax.broadcasted_iota(jnp.int32, (sl, LANES), 0)
    new_bias = jnp.where(lane <= q_idx, 0.0, MASKED)
    scale = FOX_DH ** -0.5
    for h in range(FOX_HEADS):
        cols = slice(h * FOX_DH, (h + 1) * FOX_DH)
        q = (q_ref[:, cols] * scale).astype(_BF16)
        s_past = lax.dot_general(q, kall_ref[h], _NN, preferred_element_type=_F32) - call_ref[h:h + 1, :]
        k_new = _pad_rows_to(kn_ref[:, cols], LANES).astype(_BF16)
        v_new = _pad_rows_to(vn_ref[:, cols], LANES).astype(_BF16)
        s_new = lax.dot_general(q, k_new, _NT, preferred_element_type=_F32) - c_new[h:h + 1, :] + new_bias
        m = jnp.maximum(jnp.max(s_past, axis=-1, keepdims=True), jnp.max(s_new, axis=-1, keepdims=True))
        p_past = jnp.exp(s_past - m)
        p_new = jnp.exp(s_new - m)
        l = jnp.sum(p_past, axis=-1, keepdims=True) + jnp.sum(p_new, axis=-1, keepdims=True)
        acc = (lax.dot_general(p_past.astype(_BF16), vall_ref[h], _NT, preferred_element_type=_F32)
               + jnp.dot(p_new.astype(_BF16), v_new, preferred_element_type=_F32))
        o_ref[:, cols] = acc / l


def _fox_sample(q, k, v, c_new, k_pages, v_pages, c_pool, page_table, *, row0, sl, page0):
    nb, n_pages = page_table.shape
    page = k_pages.shape[3]
    assert row0 % sl == 0 and sl <= LANES
    r0 = row0 // sl
    row_spec = pl.BlockSpec((sl, B_WIDTH), lambda b, pt: (r0 + b, 0))
    past = n_pages * page
    return pl.pallas_call(
        functools.partial(_fox_sample_kernel, n_pages=n_pages, page=page),
        grid_spec=pltpu.PrefetchScalarGridSpec(
            num_scalar_prefetch=1,
            grid=(nb,),
            in_specs=[row_spec, row_spec, row_spec, pl.BlockSpec((1, FOX_HEADS, LANES), lambda b, pt: (b, 0, 0))]
                     + _page_specs((1, FOX_HEADS, FOX_DH, page), n_pages, page0) * 2
                     + _page_specs((1, FOX_HEADS, page), n_pages, page0),
            out_specs=pl.BlockSpec((sl, B_WIDTH), lambda b, pt: (b, 0)),
            scratch_shapes=[pltpu.VMEM((FOX_HEADS, FOX_DH, past), _BF16), pltpu.VMEM((FOX_HEADS, FOX_DH, past), _BF16),
                            pltpu.VMEM((FOX_HEADS, past), _F32)]),
        out_shape=jax.ShapeDtypeStruct((nb * sl, B_WIDTH), _F32),
        compiler_params=_params(1),
        name="fox_sample",
    )(page_table, q, k, v, c_new, *([k_pages] * n_pages), *([v_pages] * n_pages), *([c_pool] * n_pages))


def _nsa_sample_kernel(pt_ref, q_ref, qr_ref, gate_ref, ksn_ref, vsn_ref, kwn_ref, vwn_ref, kwc_ref, vwc_ref,
                       wpk_ref, w1k_ref, w2k_ref, posk_ref, wpv_ref, w1v_ref, w2v_ref, posv_ref, c2s_ref, exp_ref,
                       *refs, n_pages, page, past_len, n_sel):
    del pt_ref
    ck_refs, cv_refs = refs[:n_pages], refs[n_pages:2 * n_pages]
    sk_refs, sv_refs = refs[2 * n_pages:3 * n_pages], refs[3 * n_pages:4 * n_pages]
    o_ref, xk_ref, xv_ref, kcb_ref, vcb_ref, ksel_ref, vsel_ref, ocmp_ref, sel_ref = refs[4 * n_pages:]
    sl = q_ref.shape[0]
    scale = NSA_DH ** -0.5
    assert page == LANES and 2 * NSA_DH == LANES
    eye = jnp.where(lax.broadcasted_iota(jnp.int32, (page, page), 0)
                    == lax.broadcasted_iota(jnp.int32, (page, page), 1), 1.0, 0.0).astype(_BF16)
    for p in range(n_pages):
        for src, dst in ((ck_refs[p], xk_ref), (cv_refs[p], xv_ref)):
            for pair in range(NSA_KV_HEADS // 2):
                slab = src[0, 2 * pair:2 * pair + 2].reshape(2 * NSA_DH, page).astype(_BF16)
                r0 = pair * past_len + p * page
                dst[r0:r0 + page, :] = lax.dot_general(eye, slab, _NT, preferred_element_type=_F32)
        ksel_ref[:, :, p * page:(p + 1) * page] = sk_refs[p][0].astype(_BF16)
        vsel_ref[:, :, p * page:(p + 1) * page] = sv_refs[p][0].astype(_BF16)
    n_sub = past_len // CMP_STRIDE
    _compress_into(xk_ref, n_sub, wpk_ref, w1k_ref, w2k_ref, posk_ref, kcb_ref, row_stride=1, pair_stride=past_len)
    _compress_into(xv_ref, n_sub, wpv_ref, w1v_ref, w2v_ref, posv_ref, vcb_ref, row_stride=1, pair_stride=past_len)
    q_pos = past_len + lax.broadcasted_iota(jnp.int32, (sl, 1), 0)
    _cmp_branch(q_ref, gate_ref, kcb_ref, vcb_ref, c2s_ref, ocmp_ref, sel_ref, q_pos, n_sel)

    lane = lax.broadcasted_iota(jnp.int32, (sl, LANES), 1)
    q_idx = lax.broadcasted_iota(jnp.int32, (sl, LANES), 0)
    new_ok = lane <= q_idx
    cache_rows = kwc_ref.shape[3]
    c_lane = lax.broadcasted_iota(jnp.int32, (sl, cache_rows), 1)
    c_qidx = lax.broadcasted_iota(jnp.int32, (sl, cache_rows), 0)
    c_dist = (past_len + c_qidx) - (past_len - cache_rows + c_lane)
    win_cache_bias = jnp.where((c_dist >= 0) & (c_dist < WINDOW), 0.0, MASKED)
    win_new_bias = jnp.where(new_ok, 0.0, MASKED)
    tile4 = lambda b: jnp.concatenate([b] * NSA_GROUP, axis=0)
    nt = (((1,), (1,)), ((), ()))
    for kv in range(NSA_KV_HEADS):
        kcols = slice(kv * NSA_DH, (kv + 1) * NSA_DH)
        q = jnp.concatenate(
            [(qr_ref[:, (kv * NSA_GROUP + g) * NSA_DH:(kv * NSA_GROUP + g + 1) * NSA_DH] * scale).astype(_BF16)
             for g in range(NSA_GROUP)], axis=0)
        sel = sel_ref[:, kv * LANES:(kv + 1) * LANES]
        picked = jnp.dot(sel.astype(_BF16), exp_ref[...], preferred_element_type=_F32)
        new_blk = past_len // SEL_BLOCK
        sel_new_bias = jnp.where(new_ok & (sel[:, new_blk:new_blk + 1] > 0.5), 0.0, MASKED)
        s_a = lax.dot_general(q, ksel_ref[kv], _NN, preferred_element_type=_F32) + tile4((1.0 - picked) * MASKED)
        k_new = _pad_rows_to(ksn_ref[:, kcols], LANES).astype(_BF16)
        s_b = lax.dot_general(q, k_new, nt, preferred_element_type=_F32) + tile4(sel_new_bias)
        m = jnp.maximum(jnp.maximum(jnp.max(s_a, axis=-1, keepdims=True), jnp.max(s_b, axis=-1, keepdims=True)), NEG_BIG)
        p_a, p_b = jnp.exp(s_a - m), jnp.exp(s_b - m)
        l = jnp.sum(p_a, axis=-1, keepdims=True) + jnp.sum(p_b, axis=-1, keepdims=True)
        o_sel = (lax.dot_general(p_a.astype(_BF16), vsel_ref[kv], _NT, preferred_element_type=_F32)
                 + jnp.dot(p_b.astype(_BF16), _pad_rows_to(vsn_ref[:, kcols], LANES).astype(_BF16),
                           preferred_element_type=_F32)) / jnp.maximum(l, 1e-30)
        s_a = lax.dot_general(q, kwc_ref[0, kv].astype(_BF16), _NN, preferred_element_type=_F32) + tile4(win_cache_bias)
        k_new = _pad_rows_to(kwn_ref[:, kcols], LANES).astype(_BF16)
        s_b = lax.dot_general(q, k_new, nt, preferred_element_type=_F32) + tile4(win_new_bias)
        m = jnp.maximum(jnp.maximum(jnp.max(s_a, axis=-1, keepdims=True), jnp.max(s_b, axis=-1, keepdims=True)), NEG_BIG)
        p_a, p_b = jnp.exp(s_a - m), jnp.exp(s_b - m)
        l = jnp.sum(p_a, axis=-1, keepdims=True) + jnp.sum(p_b, axis=-1, keepdims=True)
        o_win = (lax.dot_general(p_a.astype(_BF16), vwc_ref[0, kv].astype(_BF16), _NT, preferred_element_type=_F32)
                 + jnp.dot(p_b.astype(_BF16), _pad_rows_to(vwn_ref[:, kcols], LANES).astype(_BF16),
                           preferred_element_type=_F32)) / jnp.maximum(l, 1e-30)
        for g in range(NSA_GROUP):
            h = kv * NSA_GROUP + g
            hcols = slice(h * NSA_DH, (h + 1) * NSA_DH)
            rws = slice(g * sl, (g + 1) * sl)
            o_ref[:, hcols] = (ocmp_ref[:, hcols]
                               + o_sel[rws] * gate_ref[:, NSA_HEADS + h:NSA_HEADS + h + 1]
                               + o_win[rws] * gate_ref[:, 2 * NSA_HEADS + h:2 * NSA_HEADS + h + 1])


def _nsa_sample(q, qr, gates, ks, vs, kw, vw, win_k, win_v, cmp_k_pages, cmp_v_pages, sel_k_pages, sel_v_pages,
                page_table, cmp_pos, cmp_w1, cmp_w2, *, row0, sl, page0, win_block0):
    nb, n_pages = page_table.shape
    page = sel_k_pages.shape[3]
    past_len = n_pages * page
    cache_rows = win_k.shape[3]
    assert row0 % sl == 0 and sl <= SEL_BLOCK and past_len % SEL_BLOCK == 0 and past_len // CMP_STRIDE == LANES
    assert past_len >= cache_rows
    n_sel = past_len // SEL_BLOCK + 1
    r0 = row0 // sl
    row = lambda w: pl.BlockSpec((sl, w), lambda b, pt: (r0 + b, 0))
    const = lambda b, pt: (0, 0)
    return pl.pallas_call(
        functools.partial(_nsa_sample_kernel, n_pages=n_pages, page=page, past_len=past_len, n_sel=n_sel),
        grid_spec=pltpu.PrefetchScalarGridSpec(
            num_scalar_prefetch=1,
            grid=(nb,),
            in_specs=[row(C_WIDTH), row(C_WIDTH), row(LANES)] + [row(C_KV_WIDTH)] * 4
                     + [pl.BlockSpec((1, NSA_KV_HEADS, NSA_DH, cache_rows),
                                     lambda b, pt: (win_block0 + b, 0, 0, 0))] * 2
                     + _cmp_weight_specs(const)
                     + [pl.BlockSpec((LANES, LANES), const), pl.BlockSpec((LANES, past_len), const)]
                     + _page_specs((1, NSA_KV_HEADS, NSA_DH, page), n_pages, page0) * 4,
            out_specs=pl.BlockSpec((sl, C_WIDTH), lambda b, pt: (b, 0)),
            scratch_shapes=[pltpu.VMEM((2 * past_len, LANES), _F32), pltpu.VMEM((2 * past_len, LANES), _F32),
                            pltpu.VMEM((LANES, C_KV_WIDTH), _F32), pltpu.VMEM((LANES, C_KV_WIDTH), _F32),
                            pltpu.VMEM((NSA_KV_HEADS, NSA_DH, past_len), _BF16),
                            pltpu.VMEM((NSA_KV_HEADS, NSA_DH, past_len), _BF16),
                            pltpu.VMEM((sl, C_WIDTH), _F32), pltpu.VMEM((sl, NSA_KV_HEADS * LANES), _F32)]),
        out_shape=jax.ShapeDtypeStruct((nb * sl, C_WIDTH), _F32),
        compiler_params=_params(1),
        name="nsa_sample",
    )(page_table, q, qr, gates, ks, vs, kw, vw, win_k, win_v, *_cmp_weights(cmp_pos, cmp_w1, cmp_w2),
      _cmp_to_sel(LANES), _sel_expand(past_len),
      *([cmp_k_pages] * n_pages), *([cmp_v_pages] * n_pages), *([sel_k_pages] * n_pages), *([sel_v_pages] * n_pages))


SUBLANES = 8


def _split_hi_lo(x):
    hi = x.astype(_BF16)
    return hi, (x - hi.astype(_F32)).astype(_BF16)


def _split_hi_mid_lo(x):
    hi = x.astype(_BF16)
    r1 = x - hi.astype(_F32)
    mid = r1.astype(_BF16)
    return hi, mid, (r1 - mid.astype(_F32)).astype(_BF16)


def _dot_two_term(a, b):
    a_hi, a_lo = _split_hi_lo(a)
    b_hi, b_lo = _split_hi_lo(b)
    dot = lambda x, y: jnp.dot(x, y, preferred_element_type=_F32)
    return dot(a_hi, b_hi) + dot(a_hi, b_lo) + dot(a_lo, b_hi)


def _unit_lower_inverse(at, n_rows):
    c = at.shape[0]
    sub = lax.broadcasted_iota(jnp.int32, (SUBLANES, c), 0)
    lane = lax.broadcasted_iota(jnp.int32, (SUBLANES, c), 1)
    slabs = [jnp.where(lane == sub + SUBLANES * r, 1.0, 0.0) for r in range(c // SUBLANES)]
    for i in range(1, n_rows):
        n_slab = (i + SUBLANES - 1) // SUBLANES
        acc = at[0:SUBLANES, i:i + 1] * slabs[0]
        for r in range(1, n_slab):
            acc = acc + at[r * SUBLANES:(r + 1) * SUBLANES, i:i + 1] * slabs[r]
        row = jnp.where(lane[0:1] == i, 1.0, 0.0) - jnp.sum(acc, axis=0, keepdims=True)
        r_i = i // SUBLANES
        slabs[r_i] = jnp.where(sub == i % SUBLANES, row, slabs[r_i])
    return slabs


def _gdn_kernel(x_ref, z_ref, gb_ref, buf_ref, s0_ref, convw_ref, alog_ref, dtb_ref, norm_ref,
                o_ref, sfin_ref, bufout_ref, state_ref, tail_ref, *, rows, n_chunks):
    c_idx = pl.program_id(1)
    C = GDN_CHUNK
    valid = rows

    @pl.when(c_idx == 0)
    def _():
        state_ref[...] = s0_ref[0]
        tail_ref[...] = jnp.zeros_like(tail_ref)
        tail_ref[SUBLANES - (GDN_CONV - 1):SUBLANES, :] = buf_ref[0]

    pad = lambda a: a if rows == C else jnp.concatenate([a, jnp.zeros((C - rows, a.shape[1]), a.dtype)], axis=0)
    x = pad(x_ref[...])
    ext = jnp.concatenate([tail_ref[...], x], axis=0)
    w = convw_ref[...]
    y = ext[SUBLANES:SUBLANES + C] * w[GDN_CONV - 1:GDN_CONV]
    for i in range(1, GDN_CONV):
        y = y + ext[SUBLANES - i:SUBLANES - i + C] * w[GDN_CONV - 1 - i:GDN_CONV - i]
    y = y * jax.nn.sigmoid(y)
    tail_ref[...] = x[C - SUBLANES:C]

    @pl.when(c_idx == n_chunks - 1)
    def _():
        bufout_ref[0] = ext[SUBLANES + valid - (GDN_CONV - 1):SUBLANES + valid]

    row_i = lax.broadcasted_iota(jnp.int32, (C, 1), 0)
    live = jnp.where(row_i < valid, 1.0, 0.0)
    gb_in = pad(gb_ref[...])
    lane = lax.broadcasted_iota(jnp.int32, (C, LANES), 1)
    t = gb_in + dtb_ref[...]
    softplus = jnp.maximum(t, 0.0) + jnp.log(1.0 + jnp.exp(-jnp.abs(t)))
    g_all = jnp.where(lane < GDN_HEADS, -jnp.exp(alog_ref[...]) * softplus, 0.0) * live
    beta_all = jnp.where((lane >= GDN_HEADS) & (lane < 2 * GDN_HEADS), jax.nn.sigmoid(gb_in), 0.0) * live
    ii = lax.broadcasted_iota(jnp.int32, (C, C), 0)
    jj = lax.broadcasted_iota(jnp.int32, (C, C), 1)
    tri = jnp.where(ii >= jj, 1.0, 0.0).astype(_BF16)
    gc_all = sum(jnp.dot(tri, part, preferred_element_type=_F32)
                 for part in _split_hi_mid_lo(g_all))
    mix = gc_all + beta_all
    eye = jnp.where(lax.broadcasted_iota(jnp.int32, (SUBLANES, LANES), 0)
                    == lax.broadcasted_iota(jnp.int32, (SUBLANES, LANES), 1), 1.0, 0.0).astype(_BF16)
    nt = (((1,), (1,)), ((), ()))
    mix_rows = sum(lax.dot_general(eye, part, nt, preferred_element_type=_F32)
                   for part in _split_hi_mid_lo(mix))
    scale = GDN_DK ** -0.5
    for h in range(GDN_HEADS):
        q = y[:, h * GDN_DK:(h + 1) * GDN_DK]
        k = y[:, A_QK + h * GDN_DK:A_QK + (h + 1) * GDN_DK]
        v = y[:, 2 * A_QK + h * GDN_DV:2 * A_QK + (h + 1) * GDN_DV] * live
        q = q * lax.rsqrt(jnp.sum(q * q, axis=-1, keepdims=True) + 1e-6) * (live * scale)
        k = k * lax.rsqrt(jnp.sum(k * k, axis=-1, keepdims=True) + 1e-6) * live
        g_col, b_col = gc_all[:, h:h + 1], beta_all[:, GDN_HEADS + h:GDN_HEADS + h + 1]
        g_row, b_row = mix_rows[h:h + 1, :], mix_rows[GDN_HEADS + h:GDN_HEADS + h + 1, :]
        k_bf = k.astype(_BF16)
        kk = lax.dot_general(k_bf, k_bf, nt, preferred_element_type=_F32)
        qk = lax.dot_general(q.astype(_BF16), k_bf, nt, preferred_element_type=_F32)
        qk = qk * jnp.exp(jnp.where(ii >= jj, g_col - g_row, NEG_BIG))
        at = b_row * kk * jnp.exp(jnp.where(jj > ii, g_row - g_col, NEG_BIG))
        t_inv = jnp.concatenate(_unit_lower_inverse(at, valid), axis=0)
        rhs = jnp.concatenate([v * b_col, k * (b_col * jnp.exp(g_col))], axis=1)
        sol = _dot_two_term(t_inv, rhs)
        u, w_ = sol[:, :GDN_DV], sol[:, GDN_DV:]
        s = state_ref[h]
        s_bf = s.astype(_BF16)
        v_new = u - jnp.dot(w_.astype(_BF16), s_bf, preferred_element_type=_F32)
        v_new_bf = v_new.astype(_BF16)
        o = (jnp.dot((q * jnp.exp(g_col)).astype(_BF16), s_bf, preferred_element_type=_F32)
             + jnp.dot(qk.astype(_BF16), v_new_bf, preferred_element_type=_F32))
        g_last = g_col[C - 1:C, :]
        k_dec = (k * jnp.exp(g_last - g_col)).astype(_BF16)
        state_ref[h] = s * jnp.exp(g_last) + lax.dot_general(k_dec, v_new_bf, (((0,), (0,)), ((), ())),
                                                             preferred_element_type=_F32)
        o = o * lax.rsqrt(jnp.mean(o * o, axis=-1, keepdims=True) + RMS_EPS) * norm_ref[...]
        zh = pad(z_ref[:, h * GDN_DV:(h + 1) * GDN_DV])
        o = o * (zh * jax.nn.sigmoid(zh))
        o_ref[:, h * GDN_DV:(h + 1) * GDN_DV] = o[:rows]

    @pl.when(c_idx == n_chunks - 1)
    def _():
        sfin_ref[0] = state_ref[...]


def _gdn(qkv, z, gates, conv_buf, s0, conv_w, a_log, dt_bias, gdn_norm, *, n_batch, seq, row0):
    rows = min(GDN_CHUNK, seq)
    assert seq % rows == 0 and row0 % rows == 0 and (rows == GDN_CHUNK or seq == rows) and rows % SUBLANES == 0
    n_chunks = seq // rows
    r0 = row0 // rows
    row_map = lambda b, c: (r0 + b * n_chunks + c, 0)
    const = lambda b, c: (0, 0)
    lane_pad = lambda vec: jnp.zeros((1, LANES), _F32).at[0, :vec.shape[0]].set(vec)
    return pl.pallas_call(
        functools.partial(_gdn_kernel, rows=rows, n_chunks=n_chunks),
        grid=(n_batch, n_chunks),
        in_specs=[pl.BlockSpec((rows, A_CONV_CH), row_map),
                  pl.BlockSpec((rows, A_WIDTH), row_map),
                  pl.BlockSpec((rows, LANES), row_map),
                  pl.BlockSpec((1, GDN_CONV - 1, A_CONV_CH), lambda b, c: (b, 0, 0)),
                  pl.BlockSpec((1, GDN_HEADS, GDN_DK, GDN_DV), lambda b, c: (b, 0, 0, 0)),
                  pl.BlockSpec((GDN_CONV, A_CONV_CH), const),
                  pl.BlockSpec((1, LANES), const),
                  pl.BlockSpec((1, LANES), const),
                  pl.BlockSpec((1, GDN_DV), const)],
        out_specs=[pl.BlockSpec((rows, A_WIDTH), lambda b, c: (b * n_chunks + c, 0)),
                   pl.BlockSpec((1, GDN_HEADS, GDN_DK, GDN_DV), lambda b, c: (b, 0, 0, 0)),
                   pl.BlockSpec((1, GDN_CONV - 1, A_CONV_CH), lambda b, c: (b, 0, 0))],
        out_shape=[jax.ShapeDtypeStruct((n_batch * seq, A_WIDTH), _F32),
                   jax.ShapeDtypeStruct((n_batch, GDN_HEADS, GDN_DK, GDN_DV), _F32),
                   jax.ShapeDtypeStruct((n_batch, GDN_CONV - 1, A_CONV_CH), _F32)],
        scratch_shapes=[pltpu.VMEM((GDN_HEADS, GDN_DK, GDN_DV), _F32), pltpu.VMEM((SUBLANES, A_CONV_CH), _F32)],
        compiler_params=_params(2),
        name="gdn",
    )(qkv, z, gates, conv_buf, s0, conv_w, lane_pad(a_log), lane_pad(dt_bias), gdn_norm.reshape(1, GDN_DV))


def _stack_layers(states):
    return [jnp.stack(a) for a in zip(*states)]


def kernel(x_prompt, x_sample, state_gdn_s, state_gdn_conv, cache_fox_k, cache_fox_v, cache_fox_logf, cache_nsa_cmp_k, cache_nsa_cmp_v, cache_nsa_sel_k, cache_nsa_sel_v, cache_nsa_win_k, cache_nsa_win_v, cache_mem_k, cache_mem_v, page_table, mem_prompt, norm_mix, norm_xattn, norm_mem, norm_ffn, norm_final, w_in_even, b_forget, gdn_conv_w, gdn_a_log, gdn_dt_bias, gdn_norm, w_out_even, w_in_odd, nsa_gate_bias, nsa_cmp_pos, nsa_cmp_w1, nsa_cmp_w2, w_out_odd, w_mem_q, w_mem_kv, w_mem_o, w_ffn_in, w_ffn_out):
    B, L, D = x_prompt.shape
    SB, SL, _ = x_sample.shape
    depth = norm_mix.shape[0]
    n_p = B * L
    n_s = SB * SL
    past_len = page_table.shape[1] * cache_fox_k.shape[2]
    pos_p = jnp.arange(L, dtype=jnp.int32)
    pos_s = past_len + jnp.arange(SL, dtype=jnp.int32)
    ffn_hidden = w_ffn_out.shape[1]
    row_tile = min(ROW_TILE, L)
    attn_tile = min(ATTN_TILE, L)
    mem_width = MEM_HEADS * MEM_DH
    keep = min(WINDOW, L)

    def rows_p(a, *shape):
        return a[:n_p].reshape(B, L, *shape)

    def rows_s(a, *shape):
        return a[n_p:].reshape(SB, SL, *shape)

    n_pool, page = cache_fox_k.shape[1:3]
    w_buf = cache_nsa_win_k.shape[2]
    fox_k_pages, fox_v_pages = _pages_by_head(cache_fox_k), _pages_by_head(cache_fox_v)
    fox_c_pool = jnp.transpose(jnp.cumsum(cache_fox_logf, axis=2), (0, 1, 3, 2)).reshape(-1, FOX_HEADS, page)
    sel_k_pages, sel_v_pages = _pages_by_head(cache_nsa_sel_k), _pages_by_head(cache_nsa_sel_v)
    cmp_k_pages, cmp_v_pages = _pages_by_head(cache_nsa_cmp_k), _pages_by_head(cache_nsa_cmp_v)
    win_k_pages, win_v_pages = _pages_by_head(cache_nsa_win_k), _pages_by_head(cache_nsa_win_v)
    mem_k_rows = cache_mem_k.reshape(-1, MEM_HEADS * MEM_DH)
    mem_v_rows = cache_mem_v.reshape(-1, MEM_HEADS * MEM_DH)

    x = jnp.concatenate([x_prompt.reshape(n_p, D), x_sample.reshape(n_s, D)], axis=0)
    mem_flat = mem_prompt.reshape(B * MEM_TOKENS, D)
    even_p, even_s, odd_p, odd_s, mem_p = [], [], [], [], []
    for layer in range(depth):
        if layer % 2 == 0:
            e = layer // 2
            w = w_in_even[e]
            gate_cols = A_CONV_CH + 2 * GDN_HEADS
            w_perm = jnp.concatenate(
                [w[:, :A_CONV_CH], w[:, gate_cols:gate_cols + A_WIDTH + 3 * B_WIDTH], w[:, A_CONV_CH:gate_cols],
                 w[:, gate_cols + A_WIDTH + 3 * B_WIDTH:],
                 jnp.zeros((D, LANES - 2 * GDN_HEADS - FOX_HEADS), w.dtype)], axis=1).astype(_BF16)
            qkv_a, z, q_b, k_b, v_b, small = _rms_matmul_split(
                x, norm_mix[layer], w_perm, [A_CONV_CH, A_WIDTH, B_WIDTH, B_WIDTH, B_WIDTH, LANES], row_tile)
            logf = jax.nn.log_sigmoid(small[:, 2 * GDN_HEADS:2 * GDN_HEADS + FOX_HEADS] + b_forget[e])
            gdn_w = (gdn_conv_w[e], gdn_a_log[e], gdn_dt_bias[e], gdn_norm[e])
            c_rows = jnp.transpose(jnp.cumsum(rows_p(logf, FOX_HEADS), axis=1), (0, 2, 1))
            o_b_p = _fox_prompt(q_b, k_b, v_b, c_rows, n_batch=B, seq=L, tile=attn_tile)
            o_a_p, s_p, buf_p = _gdn(qkv_a, z, small, jnp.zeros((B, GDN_CONV - 1, A_CONV_CH), _F32),
                                     jnp.zeros((B, GDN_HEADS, GDN_DK, GDN_DV), _F32), *gdn_w,
                                     n_batch=B, seq=L, row0=0)
            even_p.append((s_p, buf_p, rows_p(k_b, FOX_HEADS, FOX_DH), rows_p(v_b, FOX_HEADS, FOX_DH),
                           rows_p(logf, FOX_HEADS)))
            c_new = jnp.transpose(jnp.cumsum(rows_s(logf, FOX_HEADS), axis=1), (0, 2, 1))
            c_new = jnp.pad(c_new, ((0, 0), (0, 0), (0, LANES - SL)))
            o_b_s = _fox_sample(q_b, k_b, v_b, c_new, fox_k_pages, fox_v_pages, fox_c_pool, page_table,
                                row0=n_p, sl=SL, page0=e * n_pool)
            o_a_s, s_s, buf_s = _gdn(qkv_a, z, small, state_gdn_conv[e], state_gdn_s[e], *gdn_w,
                                     n_batch=SB, seq=SL, row0=n_p)
            even_s.append((s_s, buf_s, rows_s(k_b, FOX_HEADS, FOX_DH), rows_s(v_b, FOX_HEADS, FOX_DH),
                           rows_s(logf, FOX_HEADS)))
            w_out = w_out_even[e].astype(_BF16)
            x = _matmul_residual([(o_a_p, o_a_s), (o_b_p, o_b_s)], [w_out[:A_WIDTH], w_out[A_WIDTH:]], x, row_tile)
        else:
            o = layer // 2
            q, qr, kc, vc, ks, vs, kw, vw, gates = _odd_proj(
                x, norm_mix[layer], w_in_odd[o], nsa_gate_bias[o], pos_p, pos_s, n_p, row_tile)
            cmp_w = (nsa_cmp_pos[o], nsa_cmp_w1[o], nsa_cmp_w2[o])
            o_cmp, sel = _nsa_cmp_prompt(q, gates, kc, vc, *cmp_w, n_batch=B, seq=L, tile=attn_tile)
            o_p = _nsa_selwin_prompt(qr, gates, sel, o_cmp, ks, vs, kw, vw, n_batch=B, seq=L, tile=attn_tile)
            kv4 = (NSA_KV_HEADS, NSA_DH)
            odd_p.append((rows_p(kc, *kv4), rows_p(vc, *kv4), rows_p(ks, *kv4), rows_p(vs, *kv4),
                          rows_p(kw, *kv4)[:, L - keep:], rows_p(vw, *kv4)[:, L - keep:]))
            o_s = _nsa_sample(q, qr, gates, ks, vs, kw, vw, win_k_pages, win_v_pages,
                              cmp_k_pages, cmp_v_pages, sel_k_pages, sel_v_pages, page_table, *cmp_w,
                              row0=n_p, sl=SL, page0=o * n_pool, win_block0=o * SB)
            new_wk = jnp.concatenate([cache_nsa_win_k[o], rows_s(kw, *kv4)], axis=1)[:, SL:]
            new_wv = jnp.concatenate([cache_nsa_win_v[o], rows_s(vw, *kv4)], axis=1)[:, SL:]
            odd_s.append((rows_s(kc, *kv4), rows_s(vc, *kv4), rows_s(ks, *kv4), rows_s(vs, *kv4), new_wk, new_wv))
            x = _matmul_residual([(o_p, o_s)], [w_out_odd[o].astype(_BF16)], x, row_tile)

        mkv = _rms_matmul(mem_flat, norm_mem[layer], w_mem_kv[layer].astype(_BF16), row_tile=min(ROW_TILE, B * MEM_TOKENS))
        mem_p.append((mkv[:, :mem_width].reshape(B, MEM_TOKENS, MEM_HEADS, MEM_DH),
                      mkv[:, mem_width:].reshape(B, MEM_TOKENS, MEM_HEADS, MEM_DH)))
        q = _rms_matmul(x, norm_xattn[layer], w_mem_q[layer].astype(_BF16), row_tile=row_tile)
        o_p = _mem_attn(q, mkv, mkv, n_batch=B, q_len=L, q_row0=0, q_tile=row_tile, k_col_block=0, v_col_block=1)
        o_s = _mem_attn(q, mem_k_rows, mem_v_rows, n_batch=SB, q_len=SL, q_row0=n_p, q_tile=SL,
                        k_col_block=0, v_col_block=0, kv_block0=layer * SB)
        x = _matmul_residual([(o_p, o_s)], [w_mem_o[layer].astype(_BF16)], x, row_tile)

        x = _ffn(x, norm_ffn[layer], w_ffn_in[layer][:, :ffn_hidden].astype(_BF16),
                 w_ffn_in[layer][:, ffn_hidden:].astype(_BF16), w_ffn_out[layer].astype(_BF16), row_tile)

    y = _rms(x, norm_final, row_tile)
    y_prompt = y[:n_p].reshape(B, L, D)
    y_sample = y[n_p:].reshape(SB, SL, D)
    p_gdn_s, p_gdn_conv, p_fox_k, p_fox_v, p_fox_logf = _stack_layers(even_p)
    s_gdn_s, s_gdn_conv, s_fox_k, s_fox_v, s_fox_logf = _stack_layers(even_s)
    p_cmp_k, p_cmp_v, p_sel_k, p_sel_v, p_win_k, p_win_v = _stack_layers(odd_p)
    s_cmp_k, s_cmp_v, s_sel_k, s_sel_v, s_win_k, s_win_v = _stack_layers(odd_s)
    p_mem_k, p_mem_v = _stack_layers(mem_p)
    return (y_prompt, y_sample,
            p_gdn_s, p_gdn_conv, p_fox_k, p_fox_v, p_fox_logf,
            p_cmp_k, p_cmp_v, p_sel_k, p_sel_v, p_win_k, p_win_v, p_mem_k, p_mem_v,
            s_gdn_s, s_gdn_conv, s_fox_k, s_fox_v, s_fox_logf,
            s_cmp_k, s_cmp_v, s_sel_k, s_sel_v, s_win_k, s_win_v)
```
